```python
import jax, jax.numpy as jnp
from jax import lax
import numpy as np

D_MODEL = 1024
BATCH = 4
SEQ = 4096
DEPTH = 1

HEAD_DIM = 64
HEADS_PER_GROUP = D_MODEL // 128
ATT_GROUPS = ((128, 1), (512, 4), (2048, 16))
N_ATT_GROUPS = len(ATT_GROUPS)
N_ATT_HEADS = N_ATT_GROUPS * HEADS_PER_GROUP
ATT_WIDTH = N_ATT_HEADS * HEAD_DIM
ATT_OUT_WIDTH = HEADS_PER_GROUP * HEAD_DIM
ROT_DIM = HEAD_DIM // 4
ROPE_THETA = 500000.0
Q_BLOCK = 128
POOL_WINDOWS = (2, 4, 8, 16)
POOL_GROUPS = len(POOL_WINDOWS)
POOL_GROUP_WIDTH = D_MODEL // 8
POOL_WIDTH = POOL_GROUPS * POOL_GROUP_WIDTH
IN_WIDTH = 3 * ATT_WIDTH + POOL_WIDTH + 2 * D_MODEL
D_FF = -(-8 * D_MODEL // (3 * 256)) * 256
NORM_EPS = 1e-6

kernel_name = "hybrid_dilated_attn_pool_gated_block"


def rms_norm(x, g):
    xf = x.astype(jnp.float32)
    y = xf * lax.rsqrt(jnp.mean(xf * xf, axis=-1, keepdims=True) + NORM_EPS)
    return (y * g.astype(jnp.float32)).astype(x.dtype)


def partial_rope(x, pos):
    half = ROT_DIM // 2
    inv_freq = ROPE_THETA ** (-jnp.arange(0, ROT_DIM, 2, dtype=jnp.float32) / ROT_DIM)
    ang = pos.astype(jnp.float32)[:, None] * inv_freq[None, :]
    cos = jnp.cos(ang)[None, :, None, :]
    sin = jnp.sin(ang)[None, :, None, :]
    xr = x[..., :ROT_DIM].astype(jnp.float32)
    x1, x2 = xr[..., :half], xr[..., half:]
    rot = jnp.concatenate([x1 * cos - x2 * sin, x2 * cos + x1 * sin], axis=-1)
    return jnp.concatenate([rot.astype(x.dtype), x[..., ROT_DIM:]], axis=-1)


def dilated_attention(q, k, v):
    B, S = q.shape[0], q.shape[1]
    n_blocks = S // Q_BLOCK
    scale = HEAD_DIM ** -0.5
    k_pad = [jnp.pad(k[:, :, g], ((0, 0), (w, 0), (0, 0), (0, 0))) for g, (w, _) in enumerate(ATT_GROUPS)]
    v_pad = [jnp.pad(v[:, :, g], ((0, 0), (w, 0), (0, 0), (0, 0))) for g, (w, _) in enumerate(ATT_GROUPS)]
    q_blocks = q.reshape(B, n_blocks, Q_BLOCK, N_ATT_GROUPS, HEADS_PER_GROUP, HEAD_DIM).transpose(1, 0, 2, 3, 4, 5)

    def block_fn(args):
        n, q_blk = args
        start = n * Q_BLOCK
        qi = jnp.arange(Q_BLOCK)
        outs, lses = [], []
        for g, (w, d) in enumerate(ATT_GROUPS):
            n_keys = w // d + 1
            offs = d * jnp.arange(n_keys)
            idx = qi[:, None] + w - offs[None, :]
            band_k = lax.dynamic_slice_in_dim(k_pad[g], start, Q_BLOCK + w, axis=1)
            band_v = lax.dynamic_slice_in_dim(v_pad[g], start, Q_BLOCK + w, axis=1)
            kg = band_k[:, idx]
            vg = band_v[:, idx]
            valid = ((start + qi)[:, None] - offs[None, :]) >= 0
            s = jnp.einsum('bqhd,bqkhd->bhqk', q_blk[:, :, g], kg,
                           preferred_element_type=jnp.float32) * scale
            s = jnp.where(valid, s, -jnp.inf)
            m = jnp.max(s, axis=-1, keepdims=True)
            p = jnp.exp(s - m)
            l = jnp.sum(p, axis=-1, keepdims=True)
            o = jnp.einsum('bhqk,bqkhd->bhqd', p, vg.astype(jnp.float32)) / l
            outs.append(o)
            lses.append((m + jnp.log(l))[..., 0])
        alpha = jax.nn.softmax(jnp.stack(lses, axis=0), axis=0)
        o = jnp.sum(alpha[..., None] * jnp.stack(outs, axis=0), axis=0)
        return o.transpose(0, 2, 1, 3).astype(q_blk.dtype)

    out = lax.map(block_fn, (jnp.arange(n_blocks), q_blocks))
    return out.transpose(1, 0, 2, 3, 4).reshape(B, S, ATT_OUT_WIDTH)


def multiscale_pool(z):
    B, S = z.shape[0], z.shape[1]
    zf = z.astype(jnp.float32).reshape(B, S, POOL_GROUPS, POOL_GROUP_WIDTH)
    csum = jnp.concatenate([jnp.zeros_like(zf[:, :1]), jnp.cumsum(zf, axis=1)], axis=1)
    outs = []
    for g, w in enumerate(POOL_WINDOWS):
        cg = csum[:, :, g]
        lower = jnp.concatenate([jnp.zeros_like(cg[:, :w - 1]), cg[:, :S + 1 - w]], axis=1)
        count = jnp.minimum(jnp.arange(1, S + 1), w).astype(jnp.float32)
        outs.append((cg[:, 1:] - lower) / count[None, :, None] - zf[:, :, g])
    return jnp.stack(outs, axis=2)


def setup_inputs(seed: int = 0) -> dict:
    key = jax.random.key(seed)
    ks = jax.random.split(key, 16)
    f32 = jnp.float32

    def dense(k, shape, fan_in):
        return jax.random.normal(k, shape, f32) * (fan_in ** -0.5)

    def gain(k, shape):
        return jnp.ones(shape, f32) + 0.02 * jax.random.normal(k, shape, f32)

    return {
        "x": jax.random.normal(ks[0], (BATCH, SEQ, D_MODEL), f32),
        "norm_mix": gain(ks[1], (DEPTH, D_MODEL)),
        "w_in": dense(ks[2], (DEPTH, D_MODEL, IN_WIDTH), D_MODEL),
        "w_pool_group": dense(ks[3], (DEPTH, POOL_GROUPS, POOL_GROUP_WIDTH, POOL_GROUP_WIDTH), POOL_GROUP_WIDTH),
        "pool_scale": gain(ks[4], (DEPTH, POOL_WIDTH)),
        "w_att_branch": dense(ks[5], (DEPTH, ATT_OUT_WIDTH, D_MODEL), ATT_OUT_WIDTH),
        "w_pool_branch": dense(ks[6], (DEPTH, POOL_WIDTH, D_MODEL), POOL_WIDTH),
        "w_out": dense(ks[7], (DEPTH, D_MODEL, D_MODEL), D_MODEL),
        "norm_ffn": gain(ks[8], (DEPTH, D_MODEL)),
        "w_ffn_gate": dense(ks[9], (DEPTH, D_MODEL, D_FF), D_MODEL),
        "w_ffn_up": dense(ks[10], (DEPTH, D_MODEL, D_FF), D_MODEL),
        "w_ffn_down": dense(ks[11], (DEPTH, D_FF, D_MODEL), D_FF),
        "norm_final": gain(ks[12], (D_MODEL,)),
    }


def reference(x, norm_mix, w_in, w_pool_group, pool_scale, w_att_branch, w_pool_branch,
              w_out, norm_ffn, w_ffn_gate, w_ffn_up, w_ffn_down, norm_final):
    B, S, _ = x.shape
    pos = jnp.arange(S)
    h = x
    for l in range(DEPTH):
        u = rms_norm(h, norm_mix[l])
        proj = u @ w_in[l]
        q, k, v, z, g_att, g_pool = jnp.split(
            proj, np.cumsum([ATT_WIDTH, ATT_WIDTH, ATT_WIDTH, POOL_WIDTH, D_MODEL]).tolist(), axis=-1)
        q = partial_rope(q.reshape(B, S, N_ATT_HEADS, HEAD_DIM), pos)
        k = partial_rope(k.reshape(B, S, N_ATT_HEADS, HEAD_DIM), pos)
        grp = (B, S, N_ATT_GROUPS, HEADS_PER_GROUP, HEAD_DIM)
        att = dilated_attention(q.reshape(grp), k.reshape(grp), v.reshape(grp))
        pooled = multiscale_pool(z)
        pooled = jnp.einsum('bsgc,gce->bsge', pooled, w_pool_group[l].astype(jnp.float32))
        pooled = (pooled.reshape(B, S, POOL_WIDTH) * pool_scale[l].astype(jnp.float32)).astype(h.dtype)
        y_att = att @ w_att_branch[l]
        y_pool = pooled @ w_pool_branch[l]
        merged = jax.nn.sigmoid(g_att) * y_att + jax.nn.sigmoid(g_pool) * y_pool
        h = h + merged @ w_out[l]
        f = rms_norm(h, norm_ffn[l])
        h = h + (jax.nn.silu(f @ w_ffn_gate[l]) * (f @ w_ffn_up[l])) @ w_ffn_down[l]
    return rms_norm(h, norm_final)
```

```python
import functools

import jax
import jax.numpy as jnp
from jax import lax
from jax.experimental import pallas as pl
from jax.experimental.pallas import tpu as pltpu

F32 = jnp.float32
BF16 = jnp.bfloat16

HEAD_DIM = 64
HEADS_PER_GROUP = 8
ATT_GROUPS = ((128, 1), (512, 4), (2048, 16))
N_ATT_GROUPS = len(ATT_GROUPS)
ATT_OUT_WIDTH = HEADS_PER_GROUP * HEAD_DIM
ATT_WIDTH = N_ATT_GROUPS * ATT_OUT_WIDTH
QKV_WIDTH = 3 * ATT_WIDTH
ROT_DIM = HEAD_DIM // 4
ROT_HALF = ROT_DIM // 2
ROPE_THETA = 500000.0
POOL_WINDOWS = (2, 4, 8, 16)
POOL_GROUP_WIDTH = 128
POOL_WIDTH = len(POOL_WINDOWS) * POOL_GROUP_WIDTH
NORM_EPS = 1e-6
WINDOW_KEYS = 128

LANES = 128
V7X_VMEM_LIMIT_BYTES = 56 * 1024 * 1024

PROJ_ROWS = 256
PROJ_COLS = 512
POOL_HALO = 16
ATT_ROWS = 256
Q_BLOCK = 128
FFN_ROWS = 256
FFN_CHUNKS = ((0, 1024), (1024, 2048), (2048, 2816))


def _rms_norm(xf, gain):
    ms = jnp.mean(xf * xf, axis=-1, keepdims=True)
    return (xf * lax.rsqrt(ms + NORM_EPS)) * gain


def _in_proj_kernel(x_ref, gain_ref, w_ref, wpg_ref, pscale_ref, rope_q_ref, rope_k_ref,
                    qkv_ref, pool_ref, gate_ref,
                    u_ref, z0_ref, z1_ref, z2_ref, z3_ref, *, tiles_per_seq):
    rows = x_ref.shape[0]
    seq_tile = pl.program_id(0) % tiles_per_seq

    u_ref[...] = _rms_norm(x_ref[...], gain_ref[...]).astype(BF16)

    def project(col0):
        return jnp.dot(u_ref[...], w_ref[:, col0:col0 + PROJ_COLS], preferred_element_type=F32)

    def rope(acc, tab_ref):
        outs = []
        for j in range(PROJ_COLS // LANES):
            xs = acc[:, j * LANES:(j + 1) * LANES]
            up = pltpu.roll(xs, LANES - ROT_HALF, axis=1)
            down = pltpu.roll(xs, ROT_HALF, axis=1)
            outs.append(xs * tab_ref[0] + up * tab_ref[1] + down * tab_ref[2])
        return outs

    n_att_chunks = ATT_WIDTH // PROJ_COLS
    for c in range(3 * n_att_chunks):
        col0 = c * PROJ_COLS
        acc = project(col0)
        if c < n_att_chunks:
            parts = rope(acc, rope_q_ref)
        elif c < 2 * n_att_chunks:
            parts = rope(acc, rope_k_ref)
        else:
            parts = [acc[:, j * LANES:(j + 1) * LANES] for j in range(PROJ_COLS // LANES)]
        for j, part in enumerate(parts):
            qkv_ref[:, col0 + j * LANES:col0 + (j + 1) * LANES] = part.astype(BF16)

    z = project(QKV_WIDTH)
    levels = (z0_ref, z1_ref, z2_ref, z3_ref)

    @pl.when(seq_tile == 0)
    def _():
        for ref in levels:
            ref[0:POOL_HALO, :] = jnp.zeros((POOL_HALO, POOL_WIDTH), F32)

    cur = slice(POOL_HALO, POOL_HALO + rows)
    z0_ref[cur, :] = z
    sums = []
    prev = z
    for lvl, ref in enumerate(levels):
        shift = 1 << lvl
        lo = lvl * POOL_GROUP_WIDTH
        shifted = ref[POOL_HALO - shift:POOL_HALO - shift + rows, lo:]
        prev = prev[:, (POOL_GROUP_WIDTH if lvl else 0):] + shifted
        sums.append(prev[:, 0:POOL_GROUP_WIDTH])
        if lvl + 1 < len(levels):
            levels[lvl + 1][cur, lo:] = prev
    for lvl, ref in enumerate(levels):
        lo = lvl * POOL_GROUP_WIDTH
        ref[0:POOL_HALO, lo:] = ref[rows:rows + POOL_HALO, lo:]

    pos = seq_tile * rows + lax.broadcasted_iota(jnp.int32, (rows, 1), 0)
    for g, w in enumerate(POOL_WINDOWS):
        cols = slice(g * POOL_GROUP_WIDTH, (g + 1) * POOL_GROUP_WIDTH)
        count = jnp.minimum(pos + 1, w).astype(F32)
        pooled = sums[g] / count - z[:, cols]
        mapped = jnp.dot(pooled.astype(BF16), wpg_ref[g], preferred_element_type=F32)
        pool_ref[:, cols] = (mapped * pscale_ref[:, cols]).astype(BF16)

    gate0 = QKV_WIDTH + POOL_WIDTH
    for c in range(gate_ref.shape[1] // PROJ_COLS):
        gate_ref[:, c * PROJ_COLS:(c + 1) * PROJ_COLS] = project(gate0 + c * PROJ_COLS)


def _in_proj(x2, gain, w_in, w_pool_group, pool_scale, rope_q, rope_k, seq_len):
    tokens, d_model = x2.shape
    in_width = w_in.shape[1]
    gate_width = in_width - QKV_WIDTH - POOL_WIDTH
    rows = PROJ_ROWS
    tiles_per_seq = seq_len // rows
    const = dict(pipeline_mode=pl.Buffered(1))
    return pl.pallas_call(
        functools.partial(_in_proj_kernel, tiles_per_seq=tiles_per_seq),
        name="in_proj",
        grid=(tokens // rows,),
        in_specs=[
            pl.BlockSpec((rows, d_model), lambda i: (i, 0)),
            pl.BlockSpec((1, d_model), lambda i: (0, 0), **const),
            pl.BlockSpec((d_model, in_width), lambda i: (0, 0), **const),
            pl.BlockSpec(w_pool_group.shape, lambda i: (0, 0, 0), **const),
            pl.BlockSpec((1, POOL_WIDTH), lambda i: (0, 0), **const),
            pl.BlockSpec((3, rows, LANES), lambda i: (0, i % tiles_per_seq, 0)),
            pl.BlockSpec((3, rows, LANES), lambda i: (0, i % tiles_per_seq, 0)),
        ],
        out_specs=[
            pl.BlockSpec((rows, QKV_WIDTH), lambda i: (i, 0)),
            pl.BlockSpec((rows, POOL_WIDTH), lambda i: (i, 0)),
            pl.BlockSpec((rows, gate_width), lambda i: (i, 0)),
        ],
        out_shape=[
            jax.ShapeDtypeStruct((tokens, QKV_WIDTH), BF16),
            jax.ShapeDtypeStruct((tokens, POOL_WIDTH), BF16),
            jax.ShapeDtypeStruct((tokens, gate_width), F32),
        ],
        scratch_shapes=[pltpu.VMEM((rows, d_model), BF16)]
        + [pltpu.VMEM((POOL_HALO + rows, POOL_WIDTH), F32) for _ in POOL_WINDOWS],
        compiler_params=pltpu.CompilerParams(
            dimension_semantics=("arbitrary",), vmem_limit_bytes=V7X_VMEM_LIMIT_BYTES),
    )(x2, gain, w_in, w_pool_group, pool_scale, rope_q, rope_k)


def _attention_kernel(q_ref, k_ref, v_ref, bias_ref, o_ref, lse_ref, kband_ref, vband_ref):
    first = pl.program_id(2) == 0

    @pl.when(first)
    def _():
        kband_ref[0:Q_BLOCK, :] = jnp.zeros((Q_BLOCK, ATT_OUT_WIDTH), BF16)
        vband_ref[0:Q_BLOCK, :] = jnp.zeros((Q_BLOCK, ATT_OUT_WIDTH), BF16)

    kband_ref[Q_BLOCK:2 * Q_BLOCK, :] = k_ref[0:Q_BLOCK, :]
    vband_ref[Q_BLOCK:2 * Q_BLOCK, :] = v_ref[0:Q_BLOCK, :]

    lane = lax.broadcasted_iota(jnp.int32, (Q_BLOCK, LANES), 1)
    low_half = lane < HEAD_DIM
    nt_dims = (((1,), (1,)), ((), ()))

    for sub in range(ATT_ROWS // Q_BLOCK):
        if sub == 0:
            kb_ref, vb_ref = kband_ref, vband_ref
            bias_idx = jnp.where(first, 1, 0)
        else:
            kb_ref, vb_ref = k_ref, v_ref
            bias_idx = 0
        rows = slice(sub * Q_BLOCK, (sub + 1) * Q_BLOCK)
        lse_tile = jnp.zeros((Q_BLOCK, LANES), F32)
        for pair in range(HEADS_PER_GROUP // 2):
            cols = slice(pair * LANES, (pair + 1) * LANES)
            q_pair = q_ref[rows, cols]
            k_pair = kb_ref[:, cols]
            v_pair = vb_ref[:, cols]
            outs = []
            for half in range(2):
                keep = low_half if half == 0 else jnp.logical_not(low_half)
                q_head = jnp.where(keep, q_pair, jnp.zeros_like(q_pair))
                s = lax.dot_general(q_head, k_pair, nt_dims, preferred_element_type=F32)
                s = s + bias_ref[bias_idx]
                m = jnp.max(s, axis=-1, keepdims=True)
                p = jnp.exp(s - m)
                l = jnp.sum(p, axis=-1, keepdims=True)
                pv = jnp.dot(p.astype(BF16), v_pair, preferred_element_type=F32)
                outs.append(pv / l)
                lse = m + jnp.log(l)
                lse_tile = lse_tile + jnp.where(lane == 2 * pair + half, lse, 0.0)
            o_ref[rows, cols] = jnp.where(low_half, outs[0], outs[1])
        lse_ref[rows, :] = lse_tile

    kband_ref[0:Q_BLOCK, :] = k_ref[Q_BLOCK:2 * Q_BLOCK, :]
    vband_ref[0:Q_BLOCK, :] = v_ref[Q_BLOCK:2 * Q_BLOCK, :]


def _attention_group(qkv, bias, batch, seq_len, group):
    _, dilation = ATT_GROUPS[group]
    stream_len = seq_len // dilation
    qkv_view = qkv.reshape(batch, stream_len, dilation * QKV_WIDTH)
    col_blocks = QKV_WIDTH // ATT_OUT_WIDTH
    groups = N_ATT_GROUPS

    def qkv_spec(which):
        return pl.BlockSpec(
            (None, ATT_ROWS, ATT_OUT_WIDTH),
            lambda b, r, m: (b, m, r * col_blocks + which * groups + group))

    o, lse = pl.pallas_call(
        _attention_kernel,
        name=f"attention_g{group}",
        grid=(batch, dilation, stream_len // ATT_ROWS),
        in_specs=[
            qkv_spec(0), qkv_spec(1), qkv_spec(2),
            pl.BlockSpec(bias.shape, lambda b, r, m: (0, 0, 0), pipeline_mode=pl.Buffered(1)),
        ],
        out_specs=[
            pl.BlockSpec((None, ATT_ROWS, ATT_OUT_WIDTH), lambda b, r, m: (b, m, r)),
            pl.BlockSpec((None, ATT_ROWS, LANES), lambda b, r, m: (b, m, r)),
        ],
        out_shape=[
            jax.ShapeDtypeStruct((batch, stream_len, dilation * ATT_OUT_WIDTH), F32),
            jax.ShapeDtypeStruct((batch, stream_len, dilation * LANES), F32),
        ],
        scratch_shapes=[
            pltpu.VMEM((2 * Q_BLOCK, ATT_OUT_WIDTH), BF16),
            pltpu.VMEM((2 * Q_BLOCK, ATT_OUT_WIDTH), BF16),
        ],
        compiler_params=pltpu.CompilerParams(
            dimension_semantics=("arbitrary", "arbitrary", "arbitrary")),
    )(qkv_view, qkv_view, qkv_view, bias)
    tokens = batch * seq_len
    return o.reshape(tokens, ATT_OUT_WIDTH), lse.reshape(tokens, LANES)


def _merge_ffn_kernel(x_ref, o0_ref, o1_ref, o2_ref, l0_ref, l1_ref, l2_ref, pool_ref, gate_ref,
                      wab_ref, wpb_ref, wout_ref, gffn_ref, wg_ref, wu_ref, wd_ref, gfin_ref,
                      out_ref, att_ref):
    rows, d_model = x_ref.shape
    o_refs = (o0_ref, o1_ref, o2_ref)

    lses = [l0_ref[...], l1_ref[...], l2_ref[...]]
    top = jnp.maximum(jnp.maximum(lses[0], lses[1]), lses[2])
    es = [jnp.exp(v - top) for v in lses]
    den = es[0] + es[1] + es[2]
    alphas = [e / den for e in es]

    low_half = lax.broadcasted_iota(jnp.int32, (rows, LANES), 1) < HEAD_DIM
    for pair in range(HEADS_PER_GROUP // 2):
        cols = slice(pair * LANES, (pair + 1) * LANES)
        acc = jnp.zeros((rows, LANES), F32)
        for g in range(N_ATT_GROUPS):
            a_even = jnp.broadcast_to(alphas[g][:, 2 * pair:2 * pair + 1], (rows, LANES))
            a_odd = jnp.broadcast_to(alphas[g][:, 2 * pair + 1:2 * pair + 2], (rows, LANES))
            acc = acc + jnp.where(low_half, a_even, a_odd) * o_refs[g][:, cols]
        att_ref[:, cols] = acc.astype(BF16)

    y_att = jnp.dot(att_ref[...], wab_ref[...], preferred_element_type=F32)
    y_pool = jnp.dot(pool_ref[...], wpb_ref[...], preferred_element_type=F32)
    merged = (jax.nn.sigmoid(gate_ref[:, 0:d_model]) * y_att
              + jax.nn.sigmoid(gate_ref[:, d_model:2 * d_model]) * y_pool)
    h = x_ref[...] + jnp.dot(merged.astype(BF16), wout_ref[...], preferred_element_type=F32)

    f = _rms_norm(h, gffn_ref[...]).astype(BF16)
    ffn = jnp.zeros((rows, d_model), F32)
    for lo, hi in FFN_CHUNKS:
        gate = jnp.dot(f, wg_ref[:, lo:hi], preferred_element_type=F32)
        up = jnp.dot(f, wu_ref[:, lo:hi], preferred_element_type=F32)
        hidden = (jax.nn.silu(gate) * up).astype(BF16)
        ffn = ffn + jnp.dot(hidden, wd_ref[lo:hi, :], preferred_element_type=F32)
    out_ref[...] = _rms_norm(h + ffn, gfin_ref[...])


def _merge_ffn(x2, outs, lses, pool_feat, gates, w_att_branch, w_pool_branch, w_out, norm_ffn,
               w_gate, w_up, w_down, norm_final):
    tokens, d_model = x2.shape
    rows = FFN_ROWS

    def tile(width):
        return pl.BlockSpec((rows, width), lambda i: (i, 0))

    def whole(arr):
        return pl.BlockSpec(arr.shape, lambda i: (0, 0), pipeline_mode=pl.Buffered(1))

    weights = (w_att_branch, w_pool_branch, w_out, norm_ffn, w_gate, w_up, w_down, norm_final)
    return pl.pallas_call(
        _merge_ffn_kernel,
        name="merge_ffn",
        grid=(tokens // rows,),
        in_specs=[tile(d_model)]
        + [tile(ATT_OUT_WIDTH)] * N_ATT_GROUPS
        + [tile(LANES)] * N_ATT_GROUPS
        + [tile(POOL_WIDTH), tile(gates.shape[1])]
        + [whole(w) for w in weights],
        out_specs=tile(d_model),
        out_shape=jax.ShapeDtypeStruct((tokens, d_model), F32),
        scratch_shapes=[pltpu.VMEM((rows, ATT_OUT_WIDTH), BF16)],
        compiler_params=pltpu.CompilerParams(
            dimension_semantics=("arbitrary",), vmem_limit_bytes=V7X_VMEM_LIMIT_BYTES),
    )(x2, *outs, *lses, pool_feat, gates, *weights)


def _rope_tables(seq_len):
    inv_freq = ROPE_THETA ** (-jnp.arange(0, ROT_DIM, 2, dtype=F32) / ROT_DIM)
    ang = jnp.arange(seq_len).astype(F32)[:, None] * inv_freq[None, :]
    cos, sin = jnp.cos(ang), jnp.sin(ang)
    zeros = jnp.zeros((seq_len, HEAD_DIM - ROT_DIM), F32)
    half0 = jnp.zeros((seq_len, ROT_HALF), F32)
    c_head = jnp.concatenate([cos, cos, zeros + 1.0], axis=1)
    lo_head = jnp.concatenate([-sin, half0, zeros], axis=1)
    hi_head = jnp.concatenate([half0, sin, zeros], axis=1)
    tab = jnp.stack([c_head, lo_head, hi_head])
    return jnp.concatenate([tab] * (LANES // HEAD_DIM), axis=2)


def _band_bias():
    qi = jnp.arange(Q_BLOCK)[:, None]
    kb = jnp.arange(2 * Q_BLOCK)[None, :]
    dist = qi + Q_BLOCK - kb
    valid = (dist >= 0) & (dist <= WINDOW_KEYS)
    both = jnp.stack([valid, valid & (kb >= Q_BLOCK)])
    return jnp.where(both, 0.0, -jnp.inf).astype(F32)


def kernel(x, norm_mix, w_in, w_pool_group, pool_scale, w_att_branch, w_pool_branch, w_out,
           norm_ffn, w_ffn_gate, w_ffn_up, w_ffn_down, norm_final):
    batch, seq_len, d_model = x.shape
    depth = w_in.shape[0]
    tokens = batch * seq_len
    rope_k = _rope_tables(seq_len)
    rope_q = rope_k * (HEAD_DIM ** -0.5)
    bias = _band_bias()

    assert depth == 1, "stacked layers need the un-normalised residual between layers"
    h = x.reshape(tokens, d_model)
    qkv, pool_feat, gates = _in_proj(
        h, norm_mix[0][None, :], w_in[0].astype(BF16), w_pool_group[0].astype(BF16),
        pool_scale[0][None, :], rope_q, rope_k, seq_len)
    att = [_attention_group(qkv, bias, batch, seq_len, g) for g in range(N_ATT_GROUPS)]
    out = _merge_ffn(
        h, [a[0] for a in att], [a[1] for a in att], pool_feat, gates,
        w_att_branch[0].astype(BF16), w_pool_branch[0].astype(BF16), w_out[0].astype(BF16),
        norm_ffn[0][None, :], w_ffn_gate[0].astype(BF16), w_ffn_up[0].astype(BF16),
        w_ffn_down[0].astype(BF16), norm_final[None, :])
    return out.reshape(batch, seq_len, d_model)
```

```python
import functools

import jax
import jax.numpy as jnp
from jax import lax
from jax.experimental import pallas as pl
from jax.experimental.pallas import tpu as pltpu

F32 = jnp.float32
BF16 = jnp.bfloat16

HEAD_DIM = 64
HEADS_PER_GROUP = 8
ATT_GROUPS = ((128, 1), (512, 4), (2048, 16))
N_ATT_GROUPS = len(ATT_GROUPS)
ATT_OUT_WIDTH = HEADS_PER_GROUP * HEAD_DIM
ATT_WIDTH = N_ATT_GROUPS * ATT_OUT_WIDTH
QKV_WIDTH = 3 * ATT_WIDTH
GROUP_QKV_WIDTH = 3 * ATT_OUT_WIDTH
ROT_DIM = HEAD_DIM // 4
ROT_HALF = ROT_DIM // 2
ROPE_THETA = 500000.0
POOL_WINDOWS = (2, 4, 8, 16)
POOL_GROUP_WIDTH = 128
POOL_WIDTH = len(POOL_WINDOWS) * POOL_GROUP_WIDTH
NORM_EPS = 1e-6
WINDOW_KEYS = 128

LANES = 128
V7X_VMEM_LIMIT_BYTES = 56 * 1024 * 1024

PROJ_ROWS = 256
PROJ_COLS = 512
POOL_HALO = 16
ATT_ROWS = 256
Q_BLOCK = 128
HEAD_PAIRS = HEADS_PER_GROUP // 2
FFN_ROWS = 256
FFN_CHUNKS = ((0, 1024), (1024, 2048), (2048, 2816))


def _rms_norm(xf, gain):
    ms = jnp.mean(xf * xf, axis=-1, keepdims=True)
    return (xf * lax.rsqrt(ms + NORM_EPS)) * gain


def _in_proj_kernel(x_ref, gain_ref, w_ref, wpg_ref, pscale_ref, rope_ref,
                    qkv0_ref, qkv1_ref, qkv2_ref, pool_ref, gate_ref,
                    u_ref, u4_ref, u16_ref, uslab_ref, u4slab_ref, tab4_ref, tab16_ref,
                    z0_ref, z1_ref, z2_ref, z3_ref, *, tiles_per_seq):
    rows, d_model = x_ref.shape
    seq_tile = pl.program_id(0) % tiles_per_seq
    n_slabs = d_model // LANES
    run4 = rows // 4
    run16 = rows // 16

    uf = _rms_norm(x_ref[...], gain_ref[...])
    u_ref[...] = uf.astype(BF16)
    for s in range(n_slabs):
        uslab_ref[s] = uf[:, s * LANES:(s + 1) * LANES]
    for r in range(4):
        dst = slice(r * run4, (r + 1) * run4)
        for s in range(n_slabs):
            piece = uslab_ref[s, pl.ds(r, run4, stride=4), :]
            u4slab_ref[s, dst, :] = piece
            u4_ref[dst, s * LANES:(s + 1) * LANES] = piece.astype(BF16)
        for t in range(3):
            tab4_ref[t, dst, :] = rope_ref[t, pl.ds(r, run4, stride=4), :]
    for run in range(16):
        src0 = (run // 4) * run4 + run % 4
        dst = slice(run * run16, (run + 1) * run16)
        for s in range(n_slabs):
            piece = u4slab_ref[s, pl.ds(src0, run16, stride=4), :]
            u16_ref[dst, s * LANES:(s + 1) * LANES] = piece.astype(BF16)
        for t in range(3):
            tab16_ref[t, dst, :] = tab4_ref[t, pl.ds(src0, run16, stride=4), :]

    def project(lhs_ref, col0):
        return jnp.dot(lhs_ref[...], w_ref[:, col0:col0 + PROJ_COLS], preferred_element_type=F32)

    def rope(xs, tab_ref):
        up = pltpu.roll(xs, LANES - ROT_HALF, axis=1)
        down = pltpu.roll(xs, ROT_HALF, axis=1)
        return xs * tab_ref[0] + up * tab_ref[1] + down * tab_ref[2]

    def store_natural(col, part):
        qkv0_ref[:, col:col + LANES] = part

    def store_by4(col, part):
        for r in range(4):
            qkv1_ref[r, :, col:col + LANES] = part[r * run4:(r + 1) * run4]

    def store_by16(col, part):
        for run in range(16):
            stream = 4 * (run % 4) + run // 4
            qkv2_ref[stream, :, col:col + LANES] = part[run * run16:(run + 1) * run16]

    variants = ((u_ref, rope_ref, store_natural), (u4_ref, tab4_ref, store_by4),
                (u16_ref, tab16_ref, store_by16))
    for group, (lhs_ref, tab_ref, store) in enumerate(variants):
        for which in range(3):
            acc = project(lhs_ref, which * ATT_WIDTH + group * ATT_OUT_WIDTH)
            for j in range(PROJ_COLS // LANES):
                part = acc[:, j * LANES:(j + 1) * LANES]
                if which == 0:
                    part = rope(part, tab_ref) * (HEAD_DIM ** -0.5)
                elif which == 1:
                    part = rope(part, tab_ref)
                store(which * ATT_OUT_WIDTH + j * LANES, part.astype(BF16))

    z = project(u_ref, QKV_WIDTH)
    levels = (z0_ref, z1_ref, z2_ref, z3_ref)

    @pl.when(seq_tile == 0)
    def _():
        for ref in levels:
            ref[0:POOL_HALO, :] = jnp.zeros((POOL_HALO, POOL_WIDTH), F32)

    cur = slice(POOL_HALO, POOL_HALO + rows)
    z0_ref[cur, :] = z
    sums = []
    prev = z
    for lvl, ref in enumerate(levels):
        shift = 1 << lvl
        lo = lvl * POOL_GROUP_WIDTH
        shifted = ref[POOL_HALO - shift:POOL_HALO - shift + rows, lo:]
        prev = prev[:, (POOL_GROUP_WIDTH if lvl else 0):] + shifted
        sums.append(prev[:, 0:POOL_GROUP_WIDTH])
        if lvl + 1 < len(levels):
            levels[lvl + 1][cur, lo:] = prev
    for lvl, ref in enumerate(levels):
        lo = lvl * POOL_GROUP_WIDTH
        ref[0:POOL_HALO, lo:] = ref[rows:rows + POOL_HALO, lo:]

    pos = seq_tile * rows + lax.broadcasted_iota(jnp.int32, (rows, 1), 0)
    for g, w in enumerate(POOL_WINDOWS):
        cols = slice(g * POOL_GROUP_WIDTH, (g + 1) * POOL_GROUP_WIDTH)
        count = jnp.minimum(pos + 1, w).astype(F32)
        pooled = sums[g] / count - z[:, cols]
        mapped = jnp.dot(pooled.astype(BF16), wpg_ref[g], preferred_element_type=F32)
        pool_ref[:, cols] = (mapped * pscale_ref[:, cols]).astype(BF16)

    gate0 = QKV_WIDTH + POOL_WIDTH
    for c in range(gate_ref.shape[1] // PROJ_COLS):
        gate_ref[:, c * PROJ_COLS:(c + 1) * PROJ_COLS] = project(u_ref, gate0 + c * PROJ_COLS)


def _in_proj(x2, gain, w_in, w_pool_group, pool_scale, rope_tab, batch, seq_len):
    tokens, d_model = x2.shape
    in_width = w_in.shape[1]
    gate_width = in_width - QKV_WIDTH - POOL_WIDTH
    rows = PROJ_ROWS
    tiles_per_seq = seq_len // rows
    const = dict(pipeline_mode=pl.Buffered(1))

    def stream_spec(dilation):
        return pl.BlockSpec((None, dilation, rows // dilation, GROUP_QKV_WIDTH),
                            lambda i: (i // tiles_per_seq, 0, i % tiles_per_seq, 0))

    return pl.pallas_call(
        functools.partial(_in_proj_kernel, tiles_per_seq=tiles_per_seq),
        name="in_proj",
        grid=(tokens // rows,),
        in_specs=[
            pl.BlockSpec((rows, d_model), lambda i: (i, 0)),
            pl.BlockSpec((1, d_model), lambda i: (0, 0), **const),
            pl.BlockSpec((d_model, in_width), lambda i: (0, 0), **const),
            pl.BlockSpec(w_pool_group.shape, lambda i: (0, 0, 0), **const),
            pl.BlockSpec((1, POOL_WIDTH), lambda i: (0, 0), **const),
            pl.BlockSpec((3, rows, LANES), lambda i: (0, i % tiles_per_seq, 0)),
        ],
        out_specs=[
            pl.BlockSpec((rows, GROUP_QKV_WIDTH), lambda i: (i, 0)),
            stream_spec(4),
            stream_spec(16),
            pl.BlockSpec((rows, POOL_WIDTH), lambda i: (i, 0)),
            pl.BlockSpec((rows, gate_width), lambda i: (i, 0)),
        ],
        out_shape=[
            jax.ShapeDtypeStruct((tokens, GROUP_QKV_WIDTH), BF16),
            jax.ShapeDtypeStruct((batch, 4, seq_len // 4, GROUP_QKV_WIDTH), BF16),
            jax.ShapeDtypeStruct((batch, 16, seq_len // 16, GROUP_QKV_WIDTH), BF16),
            jax.ShapeDtypeStruct((tokens, POOL_WIDTH), BF16),
            jax.ShapeDtypeStruct((tokens, gate_width), F32),
        ],
        scratch_shapes=[pltpu.VMEM((rows, d_model), BF16)] * 3
        + [pltpu.VMEM((d_model // LANES, rows, LANES), F32)] * 2
        + [pltpu.VMEM((3, rows, LANES), F32)] * 2
        + [pltpu.VMEM((POOL_HALO + rows, POOL_WIDTH), F32) for _ in POOL_WINDOWS],
        compiler_params=pltpu.CompilerParams(
            dimension_semantics=("arbitrary",), vmem_limit_bytes=V7X_VMEM_LIMIT_BYTES),
    )(x2, gain, w_in, w_pool_group, pool_scale, rope_tab)


def _attention_kernel(q_ref, k_ref, v_ref, bias_ref, o_ref, lse_ref, kband_ref, vband_ref,
                      *, dilation):
    stream = pl.program_id(1)
    step = pl.program_id(2)
    first = step == 0

    @pl.when(first)
    def _():
        kband_ref[0:Q_BLOCK, :] = jnp.zeros((Q_BLOCK, ATT_OUT_WIDTH), BF16)
        vband_ref[0:Q_BLOCK, :] = jnp.zeros((Q_BLOCK, ATT_OUT_WIDTH), BF16)

    kband_ref[Q_BLOCK:2 * Q_BLOCK, :] = k_ref[0:Q_BLOCK, :]
    vband_ref[Q_BLOCK:2 * Q_BLOCK, :] = v_ref[0:Q_BLOCK, :]

    lane = lax.broadcasted_iota(jnp.int32, (Q_BLOCK, LANES), 1)
    low_half = lane < HEAD_DIM
    nt_dims = (((1,), (1,)), ((), ()))

    for sub in range(ATT_ROWS // Q_BLOCK):
        if sub == 0:
            kb_ref, vb_ref = kband_ref, vband_ref
            bias_idx = jnp.where(first, 1, 0)
        else:
            kb_ref, vb_ref = k_ref, v_ref
            bias_idx = 0
        rows = slice(sub * Q_BLOCK, (sub + 1) * Q_BLOCK)
        first_row = step * ATT_ROWS + sub * Q_BLOCK
        if dilation == 1:
            tokens = pl.ds(pl.multiple_of(first_row, Q_BLOCK), Q_BLOCK)
        else:
            tokens = pl.ds(first_row * dilation + stream, Q_BLOCK, stride=dilation)
        lse_tile = jnp.zeros((Q_BLOCK, LANES), F32)
        for pair in range(HEAD_PAIRS):
            cols = slice(pair * LANES, (pair + 1) * LANES)
            q_pair = q_ref[rows, cols]
            k_pair = kb_ref[:, cols]
            v_pair = vb_ref[:, cols]
            outs = []
            for half in range(2):
                keep = low_half if half == 0 else jnp.logical_not(low_half)
                q_head = jnp.where(keep, q_pair, jnp.zeros_like(q_pair))
                s = lax.dot_general(q_head, k_pair, nt_dims, preferred_element_type=F32)
                s = s + bias_ref[bias_idx]
                m = jnp.max(s, axis=-1, keepdims=True)
                p = jnp.exp(s - m)
                l = jnp.sum(p, axis=-1, keepdims=True)
                pv = jnp.dot(p.astype(BF16), v_pair, preferred_element_type=F32)
                outs.append(pv / l)
                lse = m + jnp.log(l)
                lse_tile = lse_tile + jnp.where(lane == 2 * pair + half, lse, 0.0)
            o_ref[pair, tokens, :] = jnp.where(low_half, outs[0], outs[1])
        lse_ref[tokens, :] = lse_tile

    kband_ref[0:Q_BLOCK, :] = k_ref[Q_BLOCK:2 * Q_BLOCK, :]
    vband_ref[0:Q_BLOCK, :] = v_ref[Q_BLOCK:2 * Q_BLOCK, :]


def _attention_group(qkv, bias, batch, seq_len, group):
    _, dilation = ATT_GROUPS[group]
    stream_len = seq_len // dilation

    def qkv_spec(which):
        return pl.BlockSpec((None, None, ATT_ROWS, ATT_OUT_WIDTH),
                            lambda b, r, m: (b, r, m, which))

    return pl.pallas_call(
        functools.partial(_attention_kernel, dilation=dilation),
        name=f"attention_g{group}",
        grid=(batch, dilation, stream_len // ATT_ROWS),
        in_specs=[
            qkv_spec(0), qkv_spec(1), qkv_spec(2),
            pl.BlockSpec(bias.shape, lambda b, r, m: (0, 0, 0), pipeline_mode=pl.Buffered(1)),
        ],
        out_specs=[
            pl.BlockSpec((None, HEAD_PAIRS, seq_len, LANES), lambda b, r, m: (b, 0, 0, 0)),
            pl.BlockSpec((None, seq_len, LANES), lambda b, r, m: (b, 0, 0)),
        ],
        out_shape=[
            jax.ShapeDtypeStruct((batch, HEAD_PAIRS, seq_len, LANES), F32),
            jax.ShapeDtypeStruct((batch, seq_len, LANES), F32),
        ],
        scratch_shapes=[
            pltpu.VMEM((2 * Q_BLOCK, ATT_OUT_WIDTH), BF16),
            pltpu.VMEM((2 * Q_BLOCK, ATT_OUT_WIDTH), BF16),
        ],
        compiler_params=pltpu.CompilerParams(
            dimension_semantics=("arbitrary", "arbitrary", "arbitrary"),
            vmem_limit_bytes=V7X_VMEM_LIMIT_BYTES),
    )(qkv, qkv, qkv, bias)


def _merge_ffn_kernel(x_ref, o0_ref, o1_ref, o2_ref, l0_ref, l1_ref, l2_ref, pool_ref, gate_ref,
                      wab_ref, wpb_ref, wout_ref, gffn_ref, wg_ref, wu_ref, wd_ref, gfin_ref,
                      out_ref, att_ref):
    rows, d_model = x_ref.shape
    o_refs = (o0_ref, o1_ref, o2_ref)

    lses = [l0_ref[...], l1_ref[...], l2_ref[...]]
    top = jnp.maximum(jnp.maximum(lses[0], lses[1]), lses[2])
    es = [jnp.exp(v - top) for v in lses]
    den = es[0] + es[1] + es[2]
    alphas = [e / den for e in es]

    low_half = lax.broadcasted_iota(jnp.int32, (rows, LANES), 1) < HEAD_DIM
    for pair in range(HEAD_PAIRS):
        acc = jnp.zeros((rows, LANES), F32)
        for g in range(N_ATT_GROUPS):
            a_even = jnp.broadcast_to(alphas[g][:, 2 * pair:2 * pair + 1], (rows, LANES))
            a_odd = jnp.broadcast_to(alphas[g][:, 2 * pair + 1:2 * pair + 2], (rows, LANES))
            acc = acc + jnp.where(low_half, a_even, a_odd) * o_refs[g][pair]
        att_ref[:, pair * LANES:(pair + 1) * LANES] = acc.astype(BF16)

    y_att = jnp.dot(att_ref[...], wab_ref[...], preferred_element_type=F32)
    y_pool = jnp.dot(pool_ref[...], wpb_ref[...], preferred_element_type=F32)
    merged = (jax.nn.sigmoid(gate_ref[:, 0:d_model]) * y_att
              + jax.nn.sigmoid(gate_ref[:, d_model:2 * d_model]) * y_pool)
    h = x_ref[...] + jnp.dot(merged.astype(BF16), wout_ref[...], preferred_element_type=F32)

    f = _rms_norm(h, gffn_ref[...]).astype(BF16)
    ffn = jnp.zeros((rows, d_model), F32)
    for lo, hi in FFN_CHUNKS:
        gate = jnp.dot(f, wg_ref[:, lo:hi], preferred_element_type=F32)
        up = jnp.dot(f, wu_ref[:, lo:hi], preferred_element_type=F32)
        hidden = (jax.nn.silu(gate) * up).astype(BF16)
        ffn = ffn + jnp.dot(hidden, wd_ref[lo:hi, :], preferred_element_type=F32)
    out_ref[...] = _rms_norm(h + ffn, gfin_ref[...])


def _merge_ffn(x2, outs, lses, pool_feat, gates, w_att_branch, w_pool_branch, w_out, norm_ffn,
               w_gate, w_up, w_down, norm_final, seq_len):
    tokens, d_model = x2.shape
    rows = FFN_ROWS
    tiles_per_seq = seq_len // rows

    def tile(width):
        return pl.BlockSpec((rows, width), lambda i: (i, 0))

    def whole(arr):
        return pl.BlockSpec(arr.shape, lambda i: (0, 0), pipeline_mode=pl.Buffered(1))

    o_spec = pl.BlockSpec((None, HEAD_PAIRS, rows, LANES),
                          lambda i: (i // tiles_per_seq, 0, i % tiles_per_seq, 0))
    lse_spec = pl.BlockSpec((None, rows, LANES),
                            lambda i: (i // tiles_per_seq, i % tiles_per_seq, 0))
    weights = (w_att_branch, w_pool_branch, w_out, norm_ffn, w_gate, w_up, w_down, norm_final)
    return pl.pallas_call(
        _merge_ffn_kernel,
        name="merge_ffn",
        grid=(tokens // rows,),
        in_specs=[tile(d_model)]
        + [o_spec] * N_ATT_GROUPS
        + [lse_spec] * N_ATT_GROUPS
        + [tile(POOL_WIDTH), tile(gates.shape[1])]
        + [whole(w) for w in weights],
        out_specs=tile(d_model),
        out_shape=jax.ShapeDtypeStruct((tokens, d_model), F32),
        scratch_shapes=[pltpu.VMEM((rows, ATT_OUT_WIDTH), BF16)],
        compiler_params=pltpu.CompilerParams(
            dimension_semantics=("arbitrary",), vmem_limit_bytes=V7X_VMEM_LIMIT_BYTES),
    )(x2, *outs, *lses, pool_feat, gates, *weights)


def _rope_tables(seq_len):
    inv_freq = ROPE_THETA ** (-jnp.arange(0, ROT_DIM, 2, dtype=F32) / ROT_DIM)
    ang = jnp.arange(seq_len).astype(F32)[:, None] * inv_freq[None, :]
    cos, sin = jnp.cos(ang), jnp.sin(ang)
    zeros = jnp.zeros((seq_len, HEAD_DIM - ROT_DIM), F32)
    half0 = jnp.zeros((seq_len, ROT_HALF), F32)
    c_head = jnp.concatenate([cos, cos, zeros + 1.0], axis=1)
    lo_head = jnp.concatenate([-sin, half0, zeros], axis=1)
    hi_head = jnp.concatenate([half0, sin, zeros], axis=1)
    tab = jnp.stack([c_head, lo_head, hi_head])
    return jnp.concatenate([tab] * (LANES // HEAD_DIM), axis=2)


def _band_bias():
    qi = jnp.arange(Q_BLOCK)[:, None]
    kb = jnp.arange(2 * Q_BLOCK)[None, :]
    dist = qi + Q_BLOCK - kb
    valid = (dist >= 0) & (dist <= WINDOW_KEYS)
    both = jnp.stack([valid, valid & (kb >= Q_BLOCK)])
    return jnp.where(both, 0.0, -jnp.inf).astype(F32)


def kernel(x, norm_mix, w_in, w_pool_group, pool_scale, w_att_branch, w_pool_branch, w_out,
           norm_ffn, w_ffn_gate, w_ffn_up, w_ffn_down, norm_final):
    batch, seq_len, d_model = x.shape
    depth = w_in.shape[0]
    tokens = batch * seq_len
    assert depth == 1, "stacked layers need the un-normalised residual between layers"
    bias = _band_bias()

    h = x.reshape(tokens, d_model)
    qkv0, qkv1, qkv2, pool_feat, gates = _in_proj(
        h, norm_mix[0][None, :], w_in[0].astype(BF16), w_pool_group[0].astype(BF16),
        pool_scale[0][None, :], _rope_tables(seq_len), batch, seq_len)
    qkv_groups = (qkv0.reshape(batch, 1, seq_len, GROUP_QKV_WIDTH), qkv1, qkv2)
    att = [_attention_group(qkv_groups[g], bias, batch, seq_len, g) for g in range(N_ATT_GROUPS)]
    out = _merge_ffn(
        h, [a[0] for a in att], [a[1] for a in att], pool_feat, gates,
        w_att_branch[0].astype(BF16), w_pool_branch[0].astype(BF16), w_out[0].astype(BF16),
        norm_ffn[0][None, :], w_ffn_gate[0].astype(BF16), w_ffn_up[0].astype(BF16),
        w_ffn_down[0].astype(BF16), norm_final[None, :], seq_len)
    return out.reshape(batch, seq_len, d_model)
```

```python
import functools

import jax
import jax.numpy as jnp
from jax import lax
from jax.experimental import pallas as pl
from jax.experimental.pallas import tpu as pltpu

F32 = jnp.float32
BF16 = jnp.bfloat16

HEAD_DIM = 64
HEADS_PER_GROUP = 8
ATT_GROUPS = ((128, 1), (512, 4), (2048, 16))
N_ATT_GROUPS = len(ATT_GROUPS)
ATT_OUT_WIDTH = HEADS_PER_GROUP * HEAD_DIM
ATT_WIDTH = N_ATT_GROUPS * ATT_OUT_WIDTH
QKV_WIDTH = 3 * ATT_WIDTH
GROUP_QKV_WIDTH = 3 * ATT_OUT_WIDTH
ROT_DIM = HEAD_DIM // 4
ROT_HALF = ROT_DIM // 2
ROPE_THETA = 500000.0
POOL_WINDOWS = (2, 4, 8, 16)
POOL_GROUP_WIDTH = 128
POOL_WIDTH = len(POOL_WINDOWS) * POOL_GROUP_WIDTH
NORM_EPS = 1e-6
WINDOW_KEYS = 128

LANES = 128
V7X_VMEM_LIMIT_BYTES = 56 * 1024 * 1024

PROJ_ROWS = 256
PROJ_COLS = 512
POOL_HALO = 16
ATT_ROWS = 512
Q_BLOCK = 128
HEAD_PAIRS = HEADS_PER_GROUP // 2
FFN_ROWS = 256
FFN_CHUNKS = ((0, 1024), (1024, 2048), (2048, 2816))


def _rms_norm(xf, gain):
    ms = jnp.mean(xf * xf, axis=-1, keepdims=True)
    return (xf * lax.rsqrt(ms + NORM_EPS)) * gain


def _in_proj_kernel(x_ref, gain_ref, w_ref, wpg_ref, pscale_ref, rope_ref,
                    qkv0_ref, qkv1_ref, qkv2_ref, pool_ref, gate_ref,
                    u_ref, u4_ref, u16_ref, uslab_ref, u4slab_ref, tab4_ref, tab16_ref,
                    z0_ref, z1_ref, z2_ref, z3_ref, *, tiles_per_seq):
    rows, d_model = x_ref.shape
    seq_tile = pl.program_id(0) % tiles_per_seq
    n_slabs = d_model // LANES
    run4 = rows // 4
    run16 = rows // 16

    uf = _rms_norm(x_ref[...], gain_ref[...])
    u_ref[...] = uf.astype(BF16)
    for s in range(n_slabs):
        uslab_ref[s] = uf[:, s * LANES:(s + 1) * LANES]
    for r in range(4):
        dst = slice(r * run4, (r + 1) * run4)
        for s in range(n_slabs):
            piece = uslab_ref[s, pl.ds(r, run4, stride=4), :]
            u4slab_ref[s, dst, :] = piece
            u4_ref[dst, s * LANES:(s + 1) * LANES] = piece.astype(BF16)
        for t in range(3):
            tab4_ref[t, dst, :] = rope_ref[t, pl.ds(r, run4, stride=4), :]
    for run in range(16):
        src0 = (run // 4) * run4 + run % 4
        dst = slice(run * run16, (run + 1) * run16)
        for s in range(n_slabs):
            piece = u4slab_ref[s, pl.ds(src0, run16, stride=4), :]
            u16_ref[dst, s * LANES:(s + 1) * LANES] = piece.astype(BF16)
        for t in range(3):
            tab16_ref[t, dst, :] = tab4_ref[t, pl.ds(src0, run16, stride=4), :]

    def project(lhs_ref, col0):
        return jnp.dot(lhs_ref[...], w_ref[:, col0:col0 + PROJ_COLS], preferred_element_type=F32)

    def rope(xs, tab_ref):
        up = pltpu.roll(xs, LANES - ROT_HALF, axis=1)
        down = pltpu.roll(xs, ROT_HALF, axis=1)
        return xs * tab_ref[0] + up * tab_ref[1] + down * tab_ref[2]

    def store_natural(col, part):
        qkv0_ref[:, col:col + LANES] = part

    def store_by4(col, part):
        for r in range(4):
            qkv1_ref[r, :, col:col + LANES] = part[r * run4:(r + 1) * run4]

    def store_by16(col, part):
        for run in range(16):
            stream = 4 * (run % 4) + run // 4
            qkv2_ref[stream, :, col:col + LANES] = part[run * run16:(run + 1) * run16]

    variants = ((u_ref, rope_ref, store_natural), (u4_ref, tab4_ref, store_by4),
                (u16_ref, tab16_ref, store_by16))
    for group, (lhs_ref, tab_ref, store) in enumerate(variants):
        for which in range(3):
            acc = project(lhs_ref, which * ATT_WIDTH + group * ATT_OUT_WIDTH)
            for j in range(PROJ_COLS // LANES):
                part = acc[:, j * LANES:(j + 1) * LANES]
                if which == 0:
                    part = rope(part, tab_ref) * (HEAD_DIM ** -0.5)
                elif which == 1:
                    part = rope(part, tab_ref)
                store(which * ATT_OUT_WIDTH + j * LANES, part.astype(BF16))

    z = project(u_ref, QKV_WIDTH)
    levels = (z0_ref, z1_ref, z2_ref, z3_ref)

    @pl.when(seq_tile == 0)
    def _():
        for ref in levels:
            ref[0:POOL_HALO, :] = jnp.zeros((POOL_HALO, POOL_WIDTH), F32)

    cur = slice(POOL_HALO, POOL_HALO + rows)
    z0_ref[cur, :] = z
    sums = []
    prev = z
    for lvl, ref in enumerate(levels):
        shift = 1 << lvl
        lo = lvl * POOL_GROUP_WIDTH
        shifted = ref[POOL_HALO - shift:POOL_HALO - shift + rows, lo:]
        prev = prev[:, (POOL_GROUP_WIDTH if lvl else 0):] + shifted
        sums.append(prev[:, 0:POOL_GROUP_WIDTH])
        if lvl + 1 < len(levels):
            levels[lvl + 1][cur, lo:] = prev
    for lvl, ref in enumerate(levels):
        lo = lvl * POOL_GROUP_WIDTH
        ref[0:POOL_HALO, lo:] = ref[rows:rows + POOL_HALO, lo:]

    pos = seq_tile * rows + lax.broadcasted_iota(jnp.int32, (rows, 1), 0)
    for g, w in enumerate(POOL_WINDOWS):
        cols = slice(g * POOL_GROUP_WIDTH, (g + 1) * POOL_GROUP_WIDTH)
        count = jnp.minimum(pos + 1, w).astype(F32)
        pooled = sums[g] / count - z[:, cols]
        mapped = jnp.dot(pooled.astype(BF16), wpg_ref[g], preferred_element_type=F32)
        pool_ref[:, cols] = (mapped * pscale_ref[:, cols]).astype(BF16)

    gate0 = QKV_WIDTH + POOL_WIDTH
    for c in range(gate_ref.shape[1] // PROJ_COLS):
        gate_ref[:, c * PROJ_COLS:(c + 1) * PROJ_COLS] = project(u_ref, gate0 + c * PROJ_COLS)


def _in_proj(x2, gain, w_in, w_pool_group, pool_scale, rope_tab, batch, seq_len):
    tokens, d_model = x2.shape
    in_width = w_in.shape[1]
    gate_width = in_width - QKV_WIDTH - POOL_WIDTH
    rows = PROJ_ROWS
    tiles_per_seq = seq_len // rows
    const = dict(pipeline_mode=pl.Buffered(1))

    def stream_spec(dilation):
        return pl.BlockSpec((None, dilation, rows // dilation, GROUP_QKV_WIDTH),
                            lambda i: (i // tiles_per_seq, 0, i % tiles_per_seq, 0))

    return pl.pallas_call(
        functools.partial(_in_proj_kernel, tiles_per_seq=tiles_per_seq),
        name="in_proj",
        grid=(tokens // rows,),
        in_specs=[
            pl.BlockSpec((rows, d_model), lambda i: (i, 0)),
            pl.BlockSpec((1, d_model), lambda i: (0, 0), **const),
            pl.BlockSpec((d_model, in_width), lambda i: (0, 0), **const),
            pl.BlockSpec(w_pool_group.shape, lambda i: (0, 0, 0), **const),
            pl.BlockSpec((1, POOL_WIDTH), lambda i: (0, 0), **const),
            pl.BlockSpec((3, rows, LANES), lambda i: (0, i % tiles_per_seq, 0)),
        ],
        out_specs=[
            pl.BlockSpec((rows, GROUP_QKV_WIDTH), lambda i: (i, 0)),
            stream_spec(4),
            stream_spec(16),
            pl.BlockSpec((rows, POOL_WIDTH), lambda i: (i, 0)),
            pl.BlockSpec((rows, gate_width), lambda i: (i, 0)),
        ],
        out_shape=[
            jax.ShapeDtypeStruct((tokens, GROUP_QKV_WIDTH), BF16),
            jax.ShapeDtypeStruct((batch, 4, seq_len // 4, GROUP_QKV_WIDTH), BF16),
            jax.ShapeDtypeStruct((batch, 16, seq_len // 16, GROUP_QKV_WIDTH), BF16),
            jax.ShapeDtypeStruct((tokens, POOL_WIDTH), BF16),
            jax.ShapeDtypeStruct((tokens, gate_width), F32),
        ],
        scratch_shapes=[pltpu.VMEM((rows, d_model), BF16)] * 3
        + [pltpu.VMEM((d_model // LANES, rows, LANES), F32)] * 2
        + [pltpu.VMEM((3, rows, LANES), F32)] * 2
        + [pltpu.VMEM((POOL_HALO + rows, POOL_WIDTH), F32) for _ in POOL_WINDOWS],
        compiler_params=pltpu.CompilerParams(
            dimension_semantics=("arbitrary",), vmem_limit_bytes=V7X_VMEM_LIMIT_BYTES),
    )(x2, gain, w_in, w_pool_group, pool_scale, rope_tab)


def _stat_lane(head):
    return head // 2 + (HEAD_DIM if head % 2 == 0 else 0)


def _attention_kernel(q_ref, k_ref, v_ref, bias_ref, o_ref, m_ref, l_ref, kcarry_ref, vaug_ref,
                      *, dilation, single_step):
    n_streams, rows, _ = q_ref.shape
    step = pl.program_id(2)
    first = step == 0

    lane = lax.broadcasted_iota(jnp.int32, (Q_BLOCK, LANES), 1)
    low_half = lane < HEAD_DIM
    low_rows = lax.broadcasted_iota(jnp.int32, (rows, LANES), 1) < HEAD_DIM
    one = jnp.ones((rows, LANES), BF16)
    nt_dims = (((1,), (1,)), ((), ()))
    zero_block = jnp.zeros((Q_BLOCK, ATT_OUT_WIDTH), BF16)

    def clear_carry():
        for st in range(n_streams):
            kcarry_ref[st, 0:Q_BLOCK, :] = zero_block
            vaug_ref[0, st, 0:Q_BLOCK, :] = zero_block
            vaug_ref[1, st, 0:Q_BLOCK, :] = zero_block

    if single_step:
        clear_carry()
    else:
        pl.when(first)(clear_carry)

    for st in range(n_streams):
        stream = pl.program_id(1) * n_streams + st
        kcarry_ref[st, Q_BLOCK:2 * Q_BLOCK, :] = k_ref[st, 0:Q_BLOCK, :]
        for pair in range(HEAD_PAIRS):
            cols = slice(pair * LANES, (pair + 1) * LANES)
            v_pair = v_ref[st, :, cols]
            vaug_ref[0, st, Q_BLOCK:Q_BLOCK + rows, cols] = jnp.where(low_rows, v_pair, one)
            vaug_ref[1, st, Q_BLOCK:Q_BLOCK + rows, cols] = jnp.where(low_rows, one, v_pair)

        for sub in range(rows // Q_BLOCK):
            band = slice(sub * Q_BLOCK, (sub + 2) * Q_BLOCK)
            q_rows = slice(sub * Q_BLOCK, (sub + 1) * Q_BLOCK)
            if sub == 0:
                bias_idx = 1 if single_step else jnp.where(first, 1, 0)
            else:
                bias_idx = 0
            first_row = step * rows + sub * Q_BLOCK
            if dilation == 1:
                tokens = pl.ds(pl.multiple_of(first_row, Q_BLOCK), Q_BLOCK)
            else:
                tokens = pl.ds(first_row * dilation + stream, Q_BLOCK, stride=dilation)
            m_tile = jnp.zeros((Q_BLOCK, LANES), F32)
            l_tile = jnp.ones((Q_BLOCK, LANES), F32)
            for pair in range(HEAD_PAIRS):
                cols = slice(pair * LANES, (pair + 1) * LANES)
                q_pair = q_ref[st, q_rows, cols]
                if sub == 0:
                    k_pair = kcarry_ref[st, :, cols]
                else:
                    k_pair = k_ref[st, (sub - 1) * Q_BLOCK:(sub + 1) * Q_BLOCK, cols]
                pvs = []
                for half in range(2):
                    keep = low_half if half == 0 else jnp.logical_not(low_half)
                    q_head = jnp.where(keep, q_pair, jnp.zeros_like(q_pair))
                    s = lax.dot_general(q_head, k_pair, nt_dims, preferred_element_type=F32)
                    s = s + bias_ref[bias_idx]
                    m = jnp.max(s, axis=-1, keepdims=True)
                    p = jnp.exp(s - m).astype(BF16)
                    pv = jnp.dot(p, vaug_ref[half, st, band, cols], preferred_element_type=F32)
                    pvs.append(pv)
                    at_stat = lane == _stat_lane(2 * pair + half)
                    m_tile = jnp.where(at_stat, m, m_tile)
                    l_tile = jnp.where(at_stat, pv, l_tile)
                o_ref[pair, tokens, :] = jnp.where(low_half, pvs[0], pvs[1])
            m_ref[tokens, :] = m_tile
            l_ref[tokens, :] = l_tile

        kcarry_ref[st, 0:Q_BLOCK, :] = k_ref[st, rows - Q_BLOCK:rows, :]
        for half in range(2):
            vaug_ref[half, st, 0:Q_BLOCK, :] = vaug_ref[half, st, rows:rows + Q_BLOCK, :]


def _attention_group(qkv, bias, batch, seq_len, group):
    _, dilation = ATT_GROUPS[group]
    stream_len = seq_len // dilation
    rows = min(ATT_ROWS, stream_len)
    n_streams = ATT_ROWS // rows
    single_step = rows == stream_len

    def qkv_spec(which):
        return pl.BlockSpec((None, n_streams, rows, ATT_OUT_WIDTH),
                            lambda b, r, m: (b, r, m, which))

    stat_spec = pl.BlockSpec((None, seq_len, LANES), lambda b, r, m: (b, 0, 0))
    stat_shape = jax.ShapeDtypeStruct((batch, seq_len, LANES), F32)
    return pl.pallas_call(
        functools.partial(_attention_kernel, dilation=dilation, single_step=single_step),
        name=f"attention_g{group}",
        grid=(batch, dilation // n_streams, stream_len // rows),
        in_specs=[
            qkv_spec(0), qkv_spec(1), qkv_spec(2),
            pl.BlockSpec(bias.shape, lambda b, r, m: (0, 0, 0), pipeline_mode=pl.Buffered(1)),
        ],
        out_specs=[
            pl.BlockSpec((None, HEAD_PAIRS, seq_len, LANES), lambda b, r, m: (b, 0, 0, 0)),
            stat_spec, stat_spec,
        ],
        out_shape=[
            jax.ShapeDtypeStruct((batch, HEAD_PAIRS, seq_len, LANES), F32),
            stat_shape, stat_shape,
        ],
        scratch_shapes=[
            pltpu.VMEM((n_streams, 2 * Q_BLOCK, ATT_OUT_WIDTH), BF16),
            pltpu.VMEM((2, n_streams, Q_BLOCK + rows, ATT_OUT_WIDTH), BF16),
        ],
        compiler_params=pltpu.CompilerParams(
            dimension_semantics=("arbitrary", "arbitrary", "arbitrary"),
            vmem_limit_bytes=V7X_VMEM_LIMIT_BYTES),
    )(qkv, qkv, qkv, bias)


def _merge_ffn_kernel(x_ref, o0_ref, o1_ref, o2_ref, m0_ref, m1_ref, m2_ref, l0_ref, l1_ref, l2_ref,
                      pool_ref, gate_ref,
                      wab_ref, wpb_ref, wout_ref, gffn_ref, wg_ref, wu_ref, wd_ref, gfin_ref,
                      out_ref, att_ref):
    rows, d_model = x_ref.shape
    o_refs = (o0_ref, o1_ref, o2_ref)

    ms = [m0_ref[...], m1_ref[...], m2_ref[...]]
    ls = [l0_ref[...], l1_ref[...], l2_ref[...]]
    top = jnp.maximum(jnp.maximum(ms[0], ms[1]), ms[2])
    es = [jnp.exp(m - top) for m in ms]
    den = es[0] * ls[0] + es[1] * ls[1] + es[2] * ls[2]
    weights = [e / den for e in es]

    low_half = lax.broadcasted_iota(jnp.int32, (rows, LANES), 1) < HEAD_DIM
    for pair in range(HEAD_PAIRS):
        even, odd = _stat_lane(2 * pair), _stat_lane(2 * pair + 1)
        acc = jnp.zeros((rows, LANES), F32)
        for g in range(N_ATT_GROUPS):
            w_even = jnp.broadcast_to(weights[g][:, even:even + 1], (rows, LANES))
            w_odd = jnp.broadcast_to(weights[g][:, odd:odd + 1], (rows, LANES))
            acc = acc + jnp.where(low_half, w_even, w_odd) * o_refs[g][pair]
        att_ref[:, pair * LANES:(pair + 1) * LANES] = acc.astype(BF16)

    y_att = jnp.dot(att_ref[...], wab_ref[...], preferred_element_type=F32)
    y_pool = jnp.dot(pool_ref[...], wpb_ref[...], preferred_element_type=F32)
    merged = (jax.nn.sigmoid(gate_ref[:, 0:d_model]) * y_att
              + jax.nn.sigmoid(gate_ref[:, d_model:2 * d_model]) * y_pool)
    h = x_ref[...] + jnp.dot(merged.astype(BF16), wout_ref[...], preferred_element_type=F32)

    f = _rms_norm(h, gffn_ref[...]).astype(BF16)
    ffn = jnp.zeros((rows, d_model), F32)
    for lo, hi in FFN_CHUNKS:
        gate = jnp.dot(f, wg_ref[:, lo:hi], preferred_element_type=F32)
        up = jnp.dot(f, wu_ref[:, lo:hi], preferred_element_type=F32)
        hidden = (jax.nn.silu(gate) * up).astype(BF16)
        ffn = ffn + jnp.dot(hidden, wd_ref[lo:hi, :], preferred_element_type=F32)
    out_ref[...] = _rms_norm(h + ffn, gfin_ref[...])


def _merge_ffn(x2, outs, maxes, sums, pool_feat, gates, w_att_branch, w_pool_branch, w_out, norm_ffn,
               w_gate, w_up, w_down, norm_final, seq_len):
    tokens, d_model = x2.shape
    rows = FFN_ROWS
    tiles_per_seq = seq_len // rows

    def tile(width):
        return pl.BlockSpec((rows, width), lambda i: (i, 0))

    def whole(arr):
        return pl.BlockSpec(arr.shape, lambda i: (0, 0), pipeline_mode=pl.Buffered(1))

    o_spec = pl.BlockSpec((None, HEAD_PAIRS, rows, LANES),
                          lambda i: (i // tiles_per_seq, 0, i % tiles_per_seq, 0))
    stat_spec = pl.BlockSpec((None, rows, LANES),
                             lambda i: (i // tiles_per_seq, i % tiles_per_seq, 0))
    weights = (w_att_branch, w_pool_branch, w_out, norm_ffn, w_gate, w_up, w_down, norm_final)
    return pl.pallas_call(
        _merge_ffn_kernel,
        name="merge_ffn",
        grid=(tokens // rows,),
        in_specs=[tile(d_model)]
        + [o_spec] * N_ATT_GROUPS
        + [stat_spec] * (2 * N_ATT_GROUPS)
        + [tile(POOL_WIDTH), tile(gates.shape[1])]
        + [whole(w) for w in weights],
        out_specs=tile(d_model),
        out_shape=jax.ShapeDtypeStruct((tokens, d_model), F32),
        scratch_shapes=[pltpu.VMEM((rows, ATT_OUT_WIDTH), BF16)],
        compiler_params=pltpu.CompilerParams(
            dimension_semantics=("arbitrary",), vmem_limit_bytes=V7X_VMEM_LIMIT_BYTES),
    )(x2, *outs, *maxes, *sums, pool_feat, gates, *weights)


def _rope_tables(seq_len):
    inv_freq = ROPE_THETA ** (-jnp.arange(0, ROT_DIM, 2, dtype=F32) / ROT_DIM)
    ang = jnp.arange(seq_len).astype(F32)[:, None] * inv_freq[None, :]
    cos, sin = jnp.cos(ang), jnp.sin(ang)
    zeros = jnp.zeros((seq_len, HEAD_DIM - ROT_DIM), F32)
    half0 = jnp.zeros((seq_len, ROT_HALF), F32)
    c_head = jnp.concatenate([cos, cos, zeros + 1.0], axis=1)
    lo_head = jnp.concatenate([-sin, half0, zeros], axis=1)
    hi_head = jnp.concatenate([half0, sin, zeros], axis=1)
    tab = jnp.stack([c_head, lo_head, hi_head])
    return jnp.concatenate([tab] * (LANES // HEAD_DIM), axis=2)


def _band_bias():
    qi = jnp.arange(Q_BLOCK)[:, None]
    kb = jnp.arange(2 * Q_BLOCK)[None, :]
    dist = qi + Q_BLOCK - kb
    valid = (dist >= 0) & (dist <= WINDOW_KEYS)
    both = jnp.stack([valid, valid & (kb >= Q_BLOCK)])
    return jnp.where(both, 0.0, -jnp.inf).astype(F32)


def kernel(x, norm_mix, w_in, w_pool_group, pool_scale, w_att_branch, w_pool_branch, w_out,
           norm_ffn, w_ffn_gate, w_ffn_up, w_ffn_down, norm_final):
    batch, seq_len, d_model = x.shape
    depth = w_in.shape[0]
    tokens = batch * seq_len
    assert depth == 1, "stacked layers need the un-normalised residual between layers"
    bias = _band_bias()

    h = x.reshape(tokens, d_model)
    qkv0, qkv1, qkv2, pool_feat, gates = _in_proj(
        h, norm_mix[0][None, :], w_in[0].astype(BF16), w_pool_group[0].astype(BF16),
        pool_scale[0][None, :], _rope_tables(seq_len), batch, seq_len)
    qkv_groups = (qkv0.reshape(batch, 1, seq_len, GROUP_QKV_WIDTH), qkv1, qkv2)
    att = [_attention_group(qkv_groups[g], bias, batch, seq_len, g) for g in range(N_ATT_GROUPS)]
    out = _merge_ffn(
        h, [a[0] for a in att], [a[1] for a in att], [a[2] for a in att], pool_feat, gates,
        w_att_branch[0].astype(BF16), w_pool_branch[0].astype(BF16), w_out[0].astype(BF16),
        norm_ffn[0][None, :], w_ffn_gate[0].astype(BF16), w_ffn_up[0].astype(BF16),
        w_ffn_down[0].astype(BF16), norm_final[None, :], seq_len)
    return out.reshape(batch, seq_len, d_model)
```

```python
import functools

import jax
import jax.numpy as jnp
from jax import lax
from jax.experimental import pallas as pl
from jax.experimental.pallas import tpu as pltpu

F32 = jnp.float32
BF16 = jnp.bfloat16

HEAD_DIM = 64
HEADS_PER_GROUP = 8
ATT_GROUPS = ((128, 1), (512, 4), (2048, 16))
N_ATT_GROUPS = len(ATT_GROUPS)
ATT_OUT_WIDTH = HEADS_PER_GROUP * HEAD_DIM
ATT_WIDTH = N_ATT_GROUPS * ATT_OUT_WIDTH
QKV_WIDTH = 3 * ATT_WIDTH
GROUP_QKV_WIDTH = 3 * ATT_OUT_WIDTH
ROT_DIM = HEAD_DIM // 4
ROT_HALF = ROT_DIM // 2
ROPE_THETA = 500000.0
POOL_WINDOWS = (2, 4, 8, 16)
POOL_GROUP_WIDTH = 128
POOL_WIDTH = len(POOL_WINDOWS) * POOL_GROUP_WIDTH
NORM_EPS = 1e-6
WINDOW_KEYS = 128

LANES = 128
V7X_VMEM_LIMIT_BYTES = 56 * 1024 * 1024

PROJ_ROWS = 256
PROJ_COLS = 512
POOL_HALO = 16
ATT_ROWS = 512
Q_BLOCK = 128
HEAD_PAIRS = HEADS_PER_GROUP // 2
FFN_ROWS = 512
FFN_SUB_ROWS = 256
FFN_CHUNKS = ((0, 1024), (1024, 2048), (2048, 2816))
SPLIT_PIECES = 3


def _rms_norm(xf, gain):
    ms = jnp.mean(xf * xf, axis=-1, keepdims=True)
    return (xf * lax.rsqrt(ms + NORM_EPS)) * gain


def _in_proj_kernel(x_ref, gain_ref, w_ref, wpg_ref, pscale_ref, rope_ref,
                    qkv0_ref, qkv1_ref, qkv2_ref, pool_ref, gate_ref,
                    u_ref, u4_ref, u16_ref, uslab_ref, u4slab_ref, tab4_ref, tab16_ref,
                    z0_ref, z1_ref, z2_ref, z3_ref, *, tiles_per_seq):
    rows, d_model = x_ref.shape
    seq_tile = pl.program_id(0) % tiles_per_seq
    n_slabs = d_model // LANES
    run4 = rows // 4
    run16 = rows // 16

    uf = _rms_norm(x_ref[...], gain_ref[...])
    u_ref[...] = uf.astype(BF16)
    for s in range(n_slabs):
        uslab_ref[s] = uf[:, s * LANES:(s + 1) * LANES]
    for r in range(4):
        dst = slice(r * run4, (r + 1) * run4)
        for s in range(n_slabs):
            piece = uslab_ref[s, pl.ds(r, run4, stride=4), :]
            u4slab_ref[s, dst, :] = piece
            u4_ref[dst, s * LANES:(s + 1) * LANES] = piece.astype(BF16)
        for t in range(3):
            tab4_ref[t, dst, :] = rope_ref[t, pl.ds(r, run4, stride=4), :]
    for run in range(16):
        src0 = (run // 4) * run4 + run % 4
        dst = slice(run * run16, (run + 1) * run16)
        for s in range(n_slabs):
            piece = u4slab_ref[s, pl.ds(src0, run16, stride=4), :]
            u16_ref[dst, s * LANES:(s + 1) * LANES] = piece.astype(BF16)
        for t in range(3):
            tab16_ref[t, dst, :] = tab4_ref[t, pl.ds(src0, run16, stride=4), :]

    def project(lhs_ref, col0):
        return jnp.dot(lhs_ref[...], w_ref[:, col0:col0 + PROJ_COLS], preferred_element_type=F32)

    def rope(xs, tab_ref):
        up = pltpu.roll(xs, LANES - ROT_HALF, axis=1)
        down = pltpu.roll(xs, ROT_HALF, axis=1)
        return xs * tab_ref[0] + up * tab_ref[1] + down * tab_ref[2]

    def store_natural(col, part):
        qkv0_ref[:, col:col + LANES] = part

    def store_by4(col, part):
        for r in range(4):
            qkv1_ref[r, :, col:col + LANES] = part[r * run4:(r + 1) * run4]

    def store_by16(col, part):
        for run in range(16):
            stream = 4 * (run % 4) + run // 4
            qkv2_ref[stream, :, col:col + LANES] = part[run * run16:(run + 1) * run16]

    variants = ((u_ref, rope_ref, store_natural), (u4_ref, tab4_ref, store_by4),
                (u16_ref, tab16_ref, store_by16))
    for group, (lhs_ref, tab_ref, store) in enumerate(variants):
        for which in range(3):
            acc = project(lhs_ref, which * ATT_WIDTH + group * ATT_OUT_WIDTH)
            for j in range(PROJ_COLS // LANES):
                part = acc[:, j * LANES:(j + 1) * LANES]
                if which == 0:
                    part = rope(part, tab_ref) * (HEAD_DIM ** -0.5)
                elif which == 1:
                    part = rope(part, tab_ref)
                store(which * ATT_OUT_WIDTH + j * LANES, part.astype(BF16))

    z = project(u_ref, QKV_WIDTH)
    levels = (z0_ref, z1_ref, z2_ref, z3_ref)

    @pl.when(seq_tile == 0)
    def _():
        for ref in levels:
            ref[0:POOL_HALO, :] = jnp.zeros((POOL_HALO, POOL_WIDTH), F32)

    cur = slice(POOL_HALO, POOL_HALO + rows)
    z0_ref[cur, :] = z
    sums = []
    prev = z
    for lvl, ref in enumerate(levels):
        shift = 1 << lvl
        lo = lvl * POOL_GROUP_WIDTH
        shifted = ref[POOL_HALO - shift:POOL_HALO - shift + rows, lo:]
        prev = prev[:, (POOL_GROUP_WIDTH if lvl else 0):] + shifted
        sums.append(prev[:, 0:POOL_GROUP_WIDTH])
        if lvl + 1 < len(levels):
            levels[lvl + 1][cur, lo:] = prev
    for lvl, ref in enumerate(levels):
        lo = lvl * POOL_GROUP_WIDTH
        ref[0:POOL_HALO, lo:] = ref[rows:rows + POOL_HALO, lo:]

    pos = seq_tile * rows + lax.broadcasted_iota(jnp.int32, (rows, 1), 0)
    for g, w in enumerate(POOL_WINDOWS):
        cols = slice(g * POOL_GROUP_WIDTH, (g + 1) * POOL_GROUP_WIDTH)
        count = jnp.minimum(pos + 1, w).astype(F32)
        pooled = sums[g] / count - z[:, cols]
        mapped = jnp.dot(pooled.astype(BF16), wpg_ref[g], preferred_element_type=F32)
        pool_ref[:, cols] = (mapped * pscale_ref[:, cols]).astype(BF16)

    gate0 = QKV_WIDTH + POOL_WIDTH
    for c in range(gate_ref.shape[1] // PROJ_COLS):
        gate_ref[:, c * PROJ_COLS:(c + 1) * PROJ_COLS] = project(u_ref, gate0 + c * PROJ_COLS)


def _in_proj(x2, gain, w_in, w_pool_group, pool_scale, rope_tab, batch, seq_len):
    tokens, d_model = x2.shape
    in_width = w_in.shape[1]
    gate_width = in_width - QKV_WIDTH - POOL_WIDTH
    rows = PROJ_ROWS
    tiles_per_seq = seq_len // rows
    const = dict(pipeline_mode=pl.Buffered(1))

    def stream_spec(dilation):
        return pl.BlockSpec((None, dilation, rows // dilation, GROUP_QKV_WIDTH),
                            lambda i: (i // tiles_per_seq, 0, i % tiles_per_seq, 0))

    return pl.pallas_call(
        functools.partial(_in_proj_kernel, tiles_per_seq=tiles_per_seq),
        name="in_proj",
        grid=(tokens // rows,),
        in_specs=[
            pl.BlockSpec((rows, d_model), lambda i: (i, 0)),
            pl.BlockSpec((1, d_model), lambda i: (0, 0), **const),
            pl.BlockSpec((d_model, in_width), lambda i: (0, 0), **const),
            pl.BlockSpec(w_pool_group.shape, lambda i: (0, 0, 0), **const),
            pl.BlockSpec((1, POOL_WIDTH), lambda i: (0, 0), **const),
            pl.BlockSpec((3, rows, LANES), lambda i: (0, i % tiles_per_seq, 0)),
        ],
        out_specs=[
            pl.BlockSpec((rows, GROUP_QKV_WIDTH), lambda i: (i, 0)),
            stream_spec(4),
            stream_spec(16),
            pl.BlockSpec((rows, POOL_WIDTH), lambda i: (i, 0)),
            pl.BlockSpec((rows, gate_width), lambda i: (i, 0)),
        ],
        out_shape=[
            jax.ShapeDtypeStruct((tokens, GROUP_QKV_WIDTH), BF16),
            jax.ShapeDtypeStruct((batch, 4, seq_len // 4, GROUP_QKV_WIDTH), BF16),
            jax.ShapeDtypeStruct((batch, 16, seq_len // 16, GROUP_QKV_WIDTH), BF16),
            jax.ShapeDtypeStruct((tokens, POOL_WIDTH), BF16),
            jax.ShapeDtypeStruct((tokens, gate_width), F32),
        ],
        scratch_shapes=[pltpu.VMEM((rows, d_model), BF16)] * 3
        + [pltpu.VMEM((d_model // LANES, rows, LANES), F32)] * 2
        + [pltpu.VMEM((3, rows, LANES), F32)] * 2
        + [pltpu.VMEM((POOL_HALO + rows, POOL_WIDTH), F32) for _ in POOL_WINDOWS],
        compiler_params=pltpu.CompilerParams(
            dimension_semantics=("arbitrary",), vmem_limit_bytes=V7X_VMEM_LIMIT_BYTES),
    )(x2, gain, w_in, w_pool_group, pool_scale, rope_tab)


def _stat_lane(head):
    return head // 2 + (HEAD_DIM if head % 2 == 0 else 0)


def _piece_shift(group, piece):
    return HEAD_PAIRS * (SPLIT_PIECES * group + piece)


def _attention_kernel(q_ref, k_ref, v_ref, bias_ref, o_ref, m_ref, l_ref, kcarry_ref, vaug_ref,
                      *, dilation, single_step):
    n_streams, rows, _ = q_ref.shape
    step = pl.program_id(2)
    first = step == 0

    lane = lax.broadcasted_iota(jnp.int32, (Q_BLOCK, LANES), 1)
    low_half = lane < HEAD_DIM
    low_rows = lax.broadcasted_iota(jnp.int32, (rows, LANES), 1) < HEAD_DIM
    one = jnp.ones((rows, LANES), BF16)
    nt_dims = (((1,), (1,)), ((), ()))
    zero_block = jnp.zeros((Q_BLOCK, ATT_OUT_WIDTH), BF16)

    def clear_carry():
        for st in range(n_streams):
            kcarry_ref[st, 0:Q_BLOCK, :] = zero_block
            vaug_ref[0, st, 0:Q_BLOCK, :] = zero_block
            vaug_ref[1, st, 0:Q_BLOCK, :] = zero_block

    if single_step:
        clear_carry()
    else:
        pl.when(first)(clear_carry)

    for st in range(n_streams):
        stream = pl.program_id(1) * n_streams + st
        kcarry_ref[st, Q_BLOCK:2 * Q_BLOCK, :] = k_ref[st, 0:Q_BLOCK, :]
        for pair in range(HEAD_PAIRS):
            cols = slice(pair * LANES, (pair + 1) * LANES)
            v_pair = v_ref[st, :, cols]
            vaug_ref[0, st, Q_BLOCK:Q_BLOCK + rows, cols] = jnp.where(low_rows, v_pair, one)
            vaug_ref[1, st, Q_BLOCK:Q_BLOCK + rows, cols] = jnp.where(low_rows, one, v_pair)

        for sub in range(rows // Q_BLOCK):
            band = slice(sub * Q_BLOCK, (sub + 2) * Q_BLOCK)
            q_rows = slice(sub * Q_BLOCK, (sub + 1) * Q_BLOCK)
            if sub == 0:
                bias_idx = 1 if single_step else jnp.where(first, 1, 0)
            else:
                bias_idx = 0
            first_row = step * rows + sub * Q_BLOCK
            if dilation == 1:
                tokens = pl.ds(pl.multiple_of(first_row, Q_BLOCK), Q_BLOCK)
            else:
                tokens = pl.ds(first_row * dilation + stream, Q_BLOCK, stride=dilation)
            m_tile = jnp.zeros((Q_BLOCK, LANES), F32)
            l_tile = jnp.ones((Q_BLOCK, LANES), F32)
            for pair in range(HEAD_PAIRS):
                cols = slice(pair * LANES, (pair + 1) * LANES)
                q_pair = q_ref[st, q_rows, cols]
                if sub == 0:
                    k_pair = kcarry_ref[st, :, cols]
                else:
                    k_pair = k_ref[st, (sub - 1) * Q_BLOCK:(sub + 1) * Q_BLOCK, cols]
                pvs = []
                for half in range(2):
                    keep = low_half if half == 0 else jnp.logical_not(low_half)
                    q_head = jnp.where(keep, q_pair, jnp.zeros_like(q_pair))
                    s = lax.dot_general(q_head, k_pair, nt_dims, preferred_element_type=F32)
                    s = s + bias_ref[bias_idx]
                    m = jnp.max(s, axis=-1, keepdims=True)
                    p = jnp.exp(s - m).astype(BF16)
                    pv = jnp.dot(p, vaug_ref[half, st, band, cols], preferred_element_type=F32)
                    pvs.append(pv)
                    at_stat = lane == _stat_lane(2 * pair + half)
                    m_tile = jnp.where(at_stat, m, m_tile)
                    l_tile = jnp.where(at_stat, pv, l_tile)
                o_ref[pair, tokens, :] = jnp.where(low_half, pvs[0], pvs[1])
            m_ref[tokens, :] = m_tile
            l_ref[tokens, :] = l_tile

        kcarry_ref[st, 0:Q_BLOCK, :] = k_ref[st, rows - Q_BLOCK:rows, :]
        for half in range(2):
            vaug_ref[half, st, 0:Q_BLOCK, :] = vaug_ref[half, st, rows:rows + Q_BLOCK, :]


def _attention_group(qkv, bias, batch, seq_len, group):
    _, dilation = ATT_GROUPS[group]
    stream_len = seq_len // dilation
    rows = min(ATT_ROWS, stream_len)
    n_streams = ATT_ROWS // rows
    single_step = rows == stream_len

    def qkv_spec(which):
        return pl.BlockSpec((None, n_streams, rows, ATT_OUT_WIDTH),
                            lambda b, r, m: (b, r, m, which))

    stat_spec = pl.BlockSpec((None, seq_len, LANES), lambda b, r, m: (b, 0, 0))
    stat_shape = jax.ShapeDtypeStruct((batch, seq_len, LANES), F32)
    return pl.pallas_call(
        functools.partial(_attention_kernel, dilation=dilation, single_step=single_step),
        name=f"attention_g{group}",
        grid=(batch, dilation // n_streams, stream_len // rows),
        in_specs=[
            qkv_spec(0), qkv_spec(1), qkv_spec(2),
            pl.BlockSpec(bias.shape, lambda b, r, m: (0, 0, 0), pipeline_mode=pl.Buffered(1)),
        ],
        out_specs=[
            pl.BlockSpec((None, HEAD_PAIRS, seq_len, LANES), lambda b, r, m: (b, 0, 0, 0)),
            stat_spec, stat_spec,
        ],
        out_shape=[
            jax.ShapeDtypeStruct((batch, HEAD_PAIRS, seq_len, LANES), F32),
            stat_shape, stat_shape,
        ],
        scratch_shapes=[
            pltpu.VMEM((n_streams, 2 * Q_BLOCK, ATT_OUT_WIDTH), BF16),
            pltpu.VMEM((2, n_streams, Q_BLOCK + rows, ATT_OUT_WIDTH), BF16),
        ],
        compiler_params=pltpu.CompilerParams(
            dimension_semantics=("arbitrary", "arbitrary", "arbitrary"),
            vmem_limit_bytes=V7X_VMEM_LIMIT_BYTES),
    )(qkv, qkv, qkv, bias)


def _merge_ffn_kernel(x_ref, o0_ref, o1_ref, o2_ref, m0_ref, m1_ref, m2_ref, l0_ref, l1_ref, l2_ref,
                      pool_ref, gate_ref, spread_ref,
                      wab_ref, wpb_ref, wout_ref, gffn_ref, wg_ref, wu_ref, wd_ref, gfin_ref,
                      out_ref, att_ref):
    d_model = x_ref.shape[1]
    o_refs = (o0_ref, o1_ref, o2_ref)
    m_refs = (m0_ref, m1_ref, m2_ref)
    l_refs = (l0_ref, l1_ref, l2_ref)
    lane = lax.broadcasted_iota(jnp.int32, (FFN_SUB_ROWS, LANES), 1)
    stat_lanes = (lane % HEAD_DIM) < HEAD_PAIRS

    for sub in range(x_ref.shape[0] // FFN_SUB_ROWS):
        rows = slice(sub * FFN_SUB_ROWS, (sub + 1) * FFN_SUB_ROWS)

        ms = [ref[rows, :] for ref in m_refs]
        ls = [ref[rows, :] for ref in l_refs]
        top = jnp.maximum(jnp.maximum(ms[0], ms[1]), ms[2])
        es = [jnp.exp(m - top) for m in ms]
        den = es[0] * ls[0] + es[1] * ls[1] + es[2] * ls[2]

        packed = jnp.zeros((FFN_SUB_ROWS, LANES), F32)
        for g in range(N_ATT_GROUPS):
            rest = jnp.where(stat_lanes, es[g] / den, 0.0)
            for piece in range(SPLIT_PIECES):
                part = rest.astype(BF16).astype(F32)
                rest = rest - part
                shift = _piece_shift(g, piece)
                packed = packed + (pltpu.roll(part, shift, axis=1) if shift else part)
        spread = jnp.dot(packed.astype(BF16), spread_ref[...], preferred_element_type=F32)

        for pair in range(HEAD_PAIRS):
            acc = jnp.zeros((FFN_SUB_ROWS, LANES), F32)
            for g in range(N_ATT_GROUPS):
                col = g * ATT_OUT_WIDTH + pair * LANES
                acc = acc + spread[:, col:col + LANES] * o_refs[g][pair, rows, :]
            att_ref[rows, pair * LANES:(pair + 1) * LANES] = acc.astype(BF16)

        y_att = jnp.dot(att_ref[rows, :], wab_ref[...], preferred_element_type=F32)
        y_pool = jnp.dot(pool_ref[rows, :], wpb_ref[...], preferred_element_type=F32)
        merged = (jax.nn.sigmoid(gate_ref[rows, 0:d_model]) * y_att
                  + jax.nn.sigmoid(gate_ref[rows, d_model:2 * d_model]) * y_pool)
        h = x_ref[rows, :] + jnp.dot(merged.astype(BF16), wout_ref[...],
                                     preferred_element_type=F32)

        f = _rms_norm(h, gffn_ref[...]).astype(BF16)
        ffn = jnp.zeros((FFN_SUB_ROWS, d_model), F32)
        for lo, hi in FFN_CHUNKS:
            gate = jnp.dot(f, wg_ref[:, lo:hi], preferred_element_type=F32)
            up = jnp.dot(f, wu_ref[:, lo:hi], preferred_element_type=F32)
            hidden = (jax.nn.silu(gate) * up).astype(BF16)
            ffn = ffn + jnp.dot(hidden, wd_ref[lo:hi, :], preferred_element_type=F32)
        out_ref[rows, :] = _rms_norm(h + ffn, gfin_ref[...])


def _merge_ffn(x2, outs, maxes, sums, pool_feat, gates, w_att_branch, w_pool_branch, w_out, norm_ffn,
               w_gate, w_up, w_down, norm_final, seq_len):
    tokens, d_model = x2.shape
    rows = FFN_ROWS
    tiles_per_seq = seq_len // rows

    def tile(width):
        return pl.BlockSpec((rows, width), lambda i: (i, 0))

    def whole(arr):
        return pl.BlockSpec(arr.shape, lambda i: (0, 0), pipeline_mode=pl.Buffered(1))

    o_spec = pl.BlockSpec((None, HEAD_PAIRS, rows, LANES),
                          lambda i: (i // tiles_per_seq, 0, i % tiles_per_seq, 0))
    stat_spec = pl.BlockSpec((None, rows, LANES),
                             lambda i: (i // tiles_per_seq, i % tiles_per_seq, 0))
    weights = (_spread_matrix(), w_att_branch, w_pool_branch, w_out, norm_ffn, w_gate, w_up,
               w_down, norm_final)
    return pl.pallas_call(
        _merge_ffn_kernel,
        name="merge_ffn",
        grid=(tokens // rows,),
        in_specs=[tile(d_model)]
        + [o_spec] * N_ATT_GROUPS
        + [stat_spec] * (2 * N_ATT_GROUPS)
        + [tile(POOL_WIDTH), tile(gates.shape[1])]
        + [whole(w) for w in weights],
        out_specs=tile(d_model),
        out_shape=jax.ShapeDtypeStruct((tokens, d_model), F32),
        scratch_shapes=[pltpu.VMEM((rows, ATT_OUT_WIDTH), BF16)],
        compiler_params=pltpu.CompilerParams(
            dimension_semantics=("arbitrary",), vmem_limit_bytes=V7X_VMEM_LIMIT_BYTES),
    )(x2, *outs, *maxes, *sums, pool_feat, gates, *weights)


def _rope_tables(seq_len):
    inv_freq = ROPE_THETA ** (-jnp.arange(0, ROT_DIM, 2, dtype=F32) / ROT_DIM)
    ang = jnp.arange(seq_len).astype(F32)[:, None] * inv_freq[None, :]
    cos, sin = jnp.cos(ang), jnp.sin(ang)
    zeros = jnp.zeros((seq_len, HEAD_DIM - ROT_DIM), F32)
    half0 = jnp.zeros((seq_len, ROT_HALF), F32)
    c_head = jnp.concatenate([cos, cos, zeros + 1.0], axis=1)
    lo_head = jnp.concatenate([-sin, half0, zeros], axis=1)
    hi_head = jnp.concatenate([half0, sin, zeros], axis=1)
    tab = jnp.stack([c_head, lo_head, hi_head])
    return jnp.concatenate([tab] * (LANES // HEAD_DIM), axis=2)


def _spread_matrix():
    row = jnp.arange(LANES)[:, None]
    col = jnp.arange(N_ATT_GROUPS * ATT_OUT_WIDTH)[None, :]
    group = col // ATT_OUT_WIDTH
    head = (col % ATT_OUT_WIDTH) // HEAD_DIM
    stat = head // 2 + jnp.where(head % 2 == 0, HEAD_DIM, 0)
    hit = jnp.zeros((LANES, N_ATT_GROUPS * ATT_OUT_WIDTH), jnp.bool_)
    for piece in range(SPLIT_PIECES):
        hit = hit | (row == stat + HEAD_PAIRS * (SPLIT_PIECES * group + piece))
    return hit.astype(BF16)


def _band_bias():
    qi = jnp.arange(Q_BLOCK)[:, None]
    kb = jnp.arange(2 * Q_BLOCK)[None, :]
    dist = qi + Q_BLOCK - kb
    valid = (dist >= 0) & (dist <= WINDOW_KEYS)
    both = jnp.stack([valid, valid & (kb >= Q_BLOCK)])
    return jnp.where(both, 0.0, -jnp.inf).astype(F32)


def kernel(x, norm_mix, w_in, w_pool_group, pool_scale, w_att_branch, w_pool_branch, w_out,
           norm_ffn, w_ffn_gate, w_ffn_up, w_ffn_down, norm_final):
    batch, seq_len, d_model = x.shape
    depth = w_in.shape[0]
    tokens = batch * seq_len
    assert depth == 1, "stacked layers need the un-normalised residual between layers"
    bias = _band_bias()

    h = x.reshape(tokens, d_model)
    qkv0, qkv1, qkv2, pool_feat, gates = _in_proj(
        h, norm_mix[0][None, :], w_in[0].astype(BF16), w_pool_group[0].astype(BF16),
        pool_scale[0][None, :], _rope_tables(seq_len), batch, seq_len)
    qkv_groups = (qkv0.reshape(batch, 1, seq_len, GROUP_QKV_WIDTH), qkv1, qkv2)
    att = [_attention_group(qkv_groups[g], bias, batch, seq_len, g) for g in range(N_ATT_GROUPS)]
    out = _merge_ffn(
        h, [a[0] for a in att], [a[1] for a in att], [a[2] for a in att], pool_feat, gates,
        w_att_branch[0].astype(BF16), w_pool_branch[0].astype(BF16), w_out[0].astype(BF16),
        norm_ffn[0][None, :], w_ffn_gate[0].astype(BF16), w_ffn_up[0].astype(BF16),
        w_ffn_down[0].astype(BF16), norm_final[None, :], seq_len)
    return out.reshape(batch, seq_len, d_model)
```

```python
import functools

import jax
import jax.numpy as jnp
from jax import lax
from jax.experimental import pallas as pl
from jax.experimental.pallas import tpu as pltpu

F32 = jnp.float32
BF16 = jnp.bfloat16

HEAD_DIM = 64
HEADS_PER_GROUP = 8
ATT_GROUPS = ((128, 1), (512, 4), (2048, 16))
N_ATT_GROUPS = len(ATT_GROUPS)
ATT_OUT_WIDTH = HEADS_PER_GROUP * HEAD_DIM
ATT_WIDTH = N_ATT_GROUPS * ATT_OUT_WIDTH
QKV_WIDTH = 3 * ATT_WIDTH
GROUP_QKV_WIDTH = 3 * ATT_OUT_WIDTH
ROT_DIM = HEAD_DIM // 4
ROT_HALF = ROT_DIM // 2
ROPE_THETA = 500000.0
POOL_WINDOWS = (2, 4, 8, 16)
POOL_GROUP_WIDTH = 128
POOL_WIDTH = len(POOL_WINDOWS) * POOL_GROUP_WIDTH
NORM_EPS = 1e-6
WINDOW_KEYS = 128

LANES = 128
V7X_VMEM_LIMIT_BYTES = 56 * 1024 * 1024

PROJ_ROWS = 512
PROJ_SUB_ROWS = 256
PROJ_COLS = 512
POOL_HALO = 16
ATT_ROWS = 512
Q_BLOCK = 128
HEAD_PAIRS = HEADS_PER_GROUP // 2
FFN_ROWS = 512
FFN_SUB_ROWS = 256
FFN_CHUNKS = ((0, 1024), (1024, 2048), (2048, 2816))
SPLIT_PIECES = 3


def _rms_norm(xf, gain):
    ms = jnp.mean(xf * xf, axis=-1, keepdims=True)
    return (xf * lax.rsqrt(ms + NORM_EPS)) * gain


def _in_proj_kernel(x_ref, gain_ref, w_ref, wpg_ref, pscale_ref, rope_ref,
                    qkv0_ref, qkv1_ref, qkv2_ref, pool_ref, gate_ref,
                    u_ref, u4_ref, u16_ref, uslab_ref, u4slab_ref, tab4_ref, tab16_ref,
                    z0_ref, z1_ref, z2_ref, z3_ref, *, tiles_per_seq):
    rows, d_model = x_ref.shape
    seq_tile = pl.program_id(0) % tiles_per_seq
    n_slabs = d_model // LANES
    sub_rows = PROJ_SUB_ROWS
    n_sub = rows // sub_rows
    run4 = sub_rows // 4
    run16 = sub_rows // 16
    levels = (z0_ref, z1_ref, z2_ref, z3_ref)

    @pl.when(seq_tile == 0)
    def _():
        for ref in levels:
            ref[0:POOL_HALO, :] = jnp.zeros((POOL_HALO, POOL_WIDTH), F32)

    for sub in range(n_sub):
        row0 = sub * sub_rows
        uf = _rms_norm(x_ref[row0:row0 + sub_rows, :], gain_ref[...])
        u_ref[sub] = uf.astype(BF16)
        for s in range(n_slabs):
            uslab_ref[s] = uf[:, s * LANES:(s + 1) * LANES]
        for r in range(4):
            dst = slice(r * run4, (r + 1) * run4)
            for s in range(n_slabs):
                piece = uslab_ref[s, pl.ds(r, run4, stride=4), :]
                u4slab_ref[s, dst, :] = piece
                u4_ref[sub, dst, s * LANES:(s + 1) * LANES] = piece.astype(BF16)
            for t in range(3):
                tab4_ref[sub, t, dst, :] = rope_ref[t, pl.ds(row0 + r, run4, stride=4), :]
        for run in range(16):
            src0 = (run // 4) * run4 + run % 4
            dst = slice(run * run16, (run + 1) * run16)
            for s in range(n_slabs):
                piece = u4slab_ref[s, pl.ds(src0, run16, stride=4), :]
                u16_ref[sub, dst, s * LANES:(s + 1) * LANES] = piece.astype(BF16)
            for t in range(3):
                tab16_ref[sub, t, dst, :] = tab4_ref[sub, t, pl.ds(src0, run16, stride=4), :]

    def project(lhs_ref, sub, col0):
        return jnp.dot(lhs_ref[sub], w_ref[:, col0:col0 + PROJ_COLS],
                       preferred_element_type=F32)

    for sub in range(n_sub):
        row0 = POOL_HALO + sub * sub_rows
        z0_ref[row0:row0 + sub_rows, :] = project(u_ref, sub, QKV_WIDTH)

    cur = slice(POOL_HALO, POOL_HALO + rows)
    sums = []
    prev = z0_ref[cur, :]
    for lvl, ref in enumerate(levels):
        shift = 1 << lvl
        lo = lvl * POOL_GROUP_WIDTH
        shifted = ref[POOL_HALO - shift:POOL_HALO - shift + rows, lo:]
        prev = prev[:, (POOL_GROUP_WIDTH if lvl else 0):] + shifted
        sums.append(prev[:, 0:POOL_GROUP_WIDTH])
        if lvl + 1 < len(levels):
            levels[lvl + 1][cur, lo:] = prev
    pos = seq_tile * rows + lax.broadcasted_iota(jnp.int32, (rows, 1), 0)
    pooled = []
    for g, w in enumerate(POOL_WINDOWS):
        cols = slice(g * POOL_GROUP_WIDTH, (g + 1) * POOL_GROUP_WIDTH)
        count = jnp.minimum(pos + 1, w).astype(F32)
        pooled.append((sums[g] / count - z0_ref[cur, cols]).astype(BF16))
    for lvl, ref in enumerate(levels):
        lo = lvl * POOL_GROUP_WIDTH
        ref[0:POOL_HALO, lo:] = ref[rows:rows + POOL_HALO, lo:]

    def rope(xs, table):
        up = pltpu.roll(xs, LANES - ROT_HALF, axis=1)
        down = pltpu.roll(xs, ROT_HALF, axis=1)
        return xs * table(0) + up * table(1) + down * table(2)

    def store_natural(sub, col, part):
        qkv0_ref[sub * sub_rows:(sub + 1) * sub_rows, col:col + LANES] = part

    def store_by4(sub, col, part):
        for r in range(4):
            qkv1_ref[r, sub * run4:(sub + 1) * run4, col:col + LANES] = (
                part[r * run4:(r + 1) * run4])

    def store_by16(sub, col, part):
        for run in range(16):
            stream = 4 * (run % 4) + run // 4
            qkv2_ref[stream, sub * run16:(sub + 1) * run16, col:col + LANES] = (
                part[run * run16:(run + 1) * run16])

    for sub in range(n_sub):
        natural = slice(sub * sub_rows, (sub + 1) * sub_rows)
        variants = (
            (u_ref, lambda t: rope_ref[t, natural, :], store_natural),
            (u4_ref, lambda t: tab4_ref[sub, t], store_by4),
            (u16_ref, lambda t: tab16_ref[sub, t], store_by16),
        )
        for group, (lhs_ref, table, store) in enumerate(variants):
            for which in range(3):
                acc = project(lhs_ref, sub, which * ATT_WIDTH + group * ATT_OUT_WIDTH)
                for j in range(PROJ_COLS // LANES):
                    part = acc[:, j * LANES:(j + 1) * LANES]
                    if which == 0:
                        part = rope(part, table) * (HEAD_DIM ** -0.5)
                    elif which == 1:
                        part = rope(part, table)
                    store(sub, which * ATT_OUT_WIDTH + j * LANES, part.astype(BF16))

    gate0 = QKV_WIDTH + POOL_WIDTH
    for sub in range(n_sub):
        natural = slice(sub * sub_rows, (sub + 1) * sub_rows)
        for c in range(gate_ref.shape[1] // PROJ_COLS):
            gate_ref[natural, c * PROJ_COLS:(c + 1) * PROJ_COLS] = (
                project(u_ref, sub, gate0 + c * PROJ_COLS))

    for g in range(len(POOL_WINDOWS)):
        cols = slice(g * POOL_GROUP_WIDTH, (g + 1) * POOL_GROUP_WIDTH)
        mapped = jnp.dot(pooled[g], wpg_ref[g], preferred_element_type=F32)
        pool_ref[:, cols] = (mapped * pscale_ref[:, cols]).astype(BF16)


def _in_proj(x2, gain, w_in, w_pool_group, pool_scale, rope_tab, batch, seq_len):
    tokens, d_model = x2.shape
    in_width = w_in.shape[1]
    gate_width = in_width - QKV_WIDTH - POOL_WIDTH
    rows = PROJ_ROWS
    n_sub = rows // PROJ_SUB_ROWS
    tiles_per_seq = seq_len // rows
    const = dict(pipeline_mode=pl.Buffered(1))

    def stream_spec(dilation):
        return pl.BlockSpec((None, dilation, rows // dilation, GROUP_QKV_WIDTH),
                            lambda i: (i // tiles_per_seq, 0, i % tiles_per_seq, 0))

    return pl.pallas_call(
        functools.partial(_in_proj_kernel, tiles_per_seq=tiles_per_seq),
        name="in_proj",
        grid=(tokens // rows,),
        in_specs=[
            pl.BlockSpec((rows, d_model), lambda i: (i, 0)),
            pl.BlockSpec((1, d_model), lambda i: (0, 0), **const),
            pl.BlockSpec((d_model, in_width), lambda i: (0, 0), **const),
            pl.BlockSpec(w_pool_group.shape, lambda i: (0, 0, 0), **const),
            pl.BlockSpec((1, POOL_WIDTH), lambda i: (0, 0), **const),
            pl.BlockSpec((3, rows, LANES), lambda i: (0, i % tiles_per_seq, 0)),
        ],
        out_specs=[
            pl.BlockSpec((rows, GROUP_QKV_WIDTH), lambda i: (i, 0)),
            stream_spec(4),
            stream_spec(16),
            pl.BlockSpec((rows, POOL_WIDTH), lambda i: (i, 0)),
            pl.BlockSpec((rows, gate_width), lambda i: (i, 0)),
        ],
        out_shape=[
            jax.ShapeDtypeStruct((tokens, GROUP_QKV_WIDTH), BF16),
            jax.ShapeDtypeStruct((batch, 4, seq_len // 4, GROUP_QKV_WIDTH), BF16),
            jax.ShapeDtypeStruct((batch, 16, seq_len // 16, GROUP_QKV_WIDTH), BF16),
            jax.ShapeDtypeStruct((tokens, POOL_WIDTH), BF16),
            jax.ShapeDtypeStruct((tokens, gate_width), F32),
        ],
        scratch_shapes=[pltpu.VMEM((n_sub, PROJ_SUB_ROWS, d_model), BF16)] * 3
        + [pltpu.VMEM((d_model // LANES, PROJ_SUB_ROWS, LANES), F32)] * 2
        + [pltpu.VMEM((n_sub, 3, PROJ_SUB_ROWS, LANES), F32)] * 2
        + [pltpu.VMEM((POOL_HALO + rows, POOL_WIDTH), F32) for _ in POOL_WINDOWS],
        compiler_params=pltpu.CompilerParams(
            dimension_semantics=("arbitrary",), vmem_limit_bytes=V7X_VMEM_LIMIT_BYTES),
    )(x2, gain, w_in, w_pool_group, pool_scale, rope_tab)


def _stat_lane(head):
    return head // 2 + (HEAD_DIM if head % 2 == 0 else 0)


def _piece_shift(group, piece):
    return HEAD_PAIRS * (SPLIT_PIECES * group + piece)


def _attention_kernel(q_ref, k_ref, v_ref, bias_ref, o_ref, m_ref, l_ref, kcarry_ref, vaug_ref,
                      *, dilation, single_step):
    n_streams, rows, _ = q_ref.shape
    step = pl.program_id(2)
    first = step == 0

    lane = lax.broadcasted_iota(jnp.int32, (Q_BLOCK, LANES), 1)
    low_half = lane < HEAD_DIM
    low_rows = lax.broadcasted_iota(jnp.int32, (rows, LANES), 1) < HEAD_DIM
    one = jnp.ones((rows, LANES), BF16)
    nt_dims = (((1,), (1,)), ((), ()))
    zero_block = jnp.zeros((Q_BLOCK, ATT_OUT_WIDTH), BF16)

    def clear_carry():
        for st in range(n_streams):
            kcarry_ref[st, 0:Q_BLOCK, :] = zero_block
            vaug_ref[0, st, 0:Q_BLOCK, :] = zero_block
            vaug_ref[1, st, 0:Q_BLOCK, :] = zero_block

    if single_step:
        clear_carry()
    else:
        pl.when(first)(clear_carry)

    for st in range(n_streams):
        stream = pl.program_id(1) * n_streams + st
        kcarry_ref[st, Q_BLOCK:2 * Q_BLOCK, :] = k_ref[st, 0:Q_BLOCK, :]
        for pair in range(HEAD_PAIRS):
            cols = slice(pair * LANES, (pair + 1) * LANES)
            v_pair = v_ref[st, :, cols]
            vaug_ref[0, st, Q_BLOCK:Q_BLOCK + rows, cols] = jnp.where(low_rows, v_pair, one)
            vaug_ref[1, st, Q_BLOCK:Q_BLOCK + rows, cols] = jnp.where(low_rows, one, v_pair)

        for sub in range(rows // Q_BLOCK):
            band = slice(sub * Q_BLOCK, (sub + 2) * Q_BLOCK)
            q_rows = slice(sub * Q_BLOCK, (sub + 1) * Q_BLOCK)
            if sub == 0:
                bias_idx = 1 if single_step else jnp.where(first, 1, 0)
            else:
                bias_idx = 0
            first_row = step * rows + sub * Q_BLOCK
            if dilation == 1:
                tokens = pl.ds(pl.multiple_of(first_row, Q_BLOCK), Q_BLOCK)
            else:
                tokens = pl.ds(first_row * dilation + stream, Q_BLOCK, stride=dilation)
            m_tile = jnp.zeros((Q_BLOCK, LANES), F32)
            l_tile = jnp.ones((Q_BLOCK, LANES), F32)
            for pair in range(HEAD_PAIRS):
                cols = slice(pair * LANES, (pair + 1) * LANES)
                q_pair = q_ref[st, q_rows, cols]
                if sub == 0:
                    k_pair = kcarry_ref[st, :, cols]
                else:
                    k_pair = k_ref[st, (sub - 1) * Q_BLOCK:(sub + 1) * Q_BLOCK, cols]
                pvs = []
                for half in range(2):
                    keep = low_half if half == 0 else jnp.logical_not(low_half)
                    q_head = jnp.where(keep, q_pair, jnp.zeros_like(q_pair))
                    s = lax.dot_general(q_head, k_pair, nt_dims, preferred_element_type=F32)
                    s = s + bias_ref[bias_idx]
                    m = jnp.max(s, axis=-1, keepdims=True)
                    p = jnp.exp(s - m).astype(BF16)
                    pv = jnp.dot(p, vaug_ref[half, st, band, cols], preferred_element_type=F32)
                    pvs.append(pv)
                    at_stat = lane == _stat_lane(2 * pair + half)
                    m_tile = jnp.where(at_stat, m, m_tile)
                    l_tile = jnp.where(at_stat, pv, l_tile)
                o_ref[pair, tokens, :] = jnp.where(low_half, pvs[0], pvs[1])
            m_ref[tokens, :] = m_tile
            l_ref[tokens, :] = l_tile

        kcarry_ref[st, 0:Q_BLOCK, :] = k_ref[st, rows - Q_BLOCK:rows, :]
        for half in range(2):
            vaug_ref[half, st, 0:Q_BLOCK, :] = vaug_ref[half, st, rows:rows + Q_BLOCK, :]


def _attention_group(qkv, bias, batch, seq_len, group):
    _, dilation = ATT_GROUPS[group]
    stream_len = seq_len // dilation
    rows = min(ATT_ROWS, stream_len)
    n_streams = ATT_ROWS // rows
    single_step = rows == stream_len

    def qkv_spec(which):
        return pl.BlockSpec((None, n_streams, rows, ATT_OUT_WIDTH),
                            lambda b, r, m: (b, r, m, which))

    stat_spec = pl.BlockSpec((None, seq_len, LANES), lambda b, r, m: (b, 0, 0))
    stat_shape = jax.ShapeDtypeStruct((batch, seq_len, LANES), F32)
    return pl.pallas_call(
        functools.partial(_attention_kernel, dilation=dilation, single_step=single_step),
        name=f"attention_g{group}",
        grid=(batch, dilation // n_streams, stream_len // rows),
        in_specs=[
            qkv_spec(0), qkv_spec(1), qkv_spec(2),
            pl.BlockSpec(bias.shape, lambda b, r, m: (0, 0, 0), pipeline_mode=pl.Buffered(1)),
        ],
        out_specs=[
            pl.BlockSpec((None, HEAD_PAIRS, seq_len, LANES), lambda b, r, m: (b, 0, 0, 0)),
            stat_spec, stat_spec,
        ],
        out_shape=[
            jax.ShapeDtypeStruct((batch, HEAD_PAIRS, seq_len, LANES), F32),
            stat_shape, stat_shape,
        ],
        scratch_shapes=[
            pltpu.VMEM((n_streams, 2 * Q_BLOCK, ATT_OUT_WIDTH), BF16),
            pltpu.VMEM((2, n_streams, Q_BLOCK + rows, ATT_OUT_WIDTH), BF16),
        ],
        compiler_params=pltpu.CompilerParams(
            dimension_semantics=("arbitrary", "arbitrary", "arbitrary"),
            vmem_limit_bytes=V7X_VMEM_LIMIT_BYTES),
    )(qkv, qkv, qkv, bias)


def _merge_ffn_kernel(x_ref, o0_ref, o1_ref, o2_ref, m0_ref, m1_ref, m2_ref, l0_ref, l1_ref, l2_ref,
                      pool_ref, gate_ref, spread_ref,
                      wab_ref, wpb_ref, wout_ref, gffn_ref, wg_ref, wu_ref, wd_ref, gfin_ref,
                      out_ref, att_ref):
    d_model = x_ref.shape[1]
    o_refs = (o0_ref, o1_ref, o2_ref)
    m_refs = (m0_ref, m1_ref, m2_ref)
    l_refs = (l0_ref, l1_ref, l2_ref)
    lane = lax.broadcasted_iota(jnp.int32, (FFN_SUB_ROWS, LANES), 1)
    stat_lanes = (lane % HEAD_DIM) < HEAD_PAIRS

    for sub in range(x_ref.shape[0] // FFN_SUB_ROWS):
        rows = slice(sub * FFN_SUB_ROWS, (sub + 1) * FFN_SUB_ROWS)

        ms = [ref[rows, :] for ref in m_refs]
        ls = [ref[rows, :] for ref in l_refs]
        top = jnp.maximum(jnp.maximum(ms[0], ms[1]), ms[2])
        es = [jnp.exp(m - top) for m in ms]
        den = es[0] * ls[0] + es[1] * ls[1] + es[2] * ls[2]

        packed = jnp.zeros((FFN_SUB_ROWS, LANES), F32)
        for g in range(N_ATT_GROUPS):
            rest = jnp.where(stat_lanes, es[g] / den, 0.0)
            for piece in range(SPLIT_PIECES):
                part = rest.astype(BF16).astype(F32)
                rest = rest - part
                shift = _piece_shift(g, piece)
                packed = packed + (pltpu.roll(part, shift, axis=1) if shift else part)
        spread = jnp.dot(packed.astype(BF16), spread_ref[...], preferred_element_type=F32)

        for pair in range(HEAD_PAIRS):
            acc = jnp.zeros((FFN_SUB_ROWS, LANES), F32)
            for g in range(N_ATT_GROUPS):
                col = g * ATT_OUT_WIDTH + pair * LANES
                acc = acc + spread[:, col:col + LANES] * o_refs[g][pair, rows, :]
            att_ref[rows, pair * LANES:(pair + 1) * LANES] = acc.astype(BF16)

        y_att = jnp.dot(att_ref[rows, :], wab_ref[...], preferred_element_type=F32)
        y_pool = jnp.dot(pool_ref[rows, :], wpb_ref[...], preferred_element_type=F32)
        merged = (jax.nn.sigmoid(gate_ref[rows, 0:d_model]) * y_att
                  + jax.nn.sigmoid(gate_ref[rows, d_model:2 * d_model]) * y_pool)
        h = x_ref[rows, :] + jnp.dot(merged.astype(BF16), wout_ref[...],
                                     preferred_element_type=F32)

        f = _rms_norm(h, gffn_ref[...]).astype(BF16)
        ffn = jnp.zeros((FFN_SUB_ROWS, d_model), F32)
        for lo, hi in FFN_CHUNKS:
            gate = jnp.dot(f, wg_ref[:, lo:hi], preferred_element_type=F32)
            up = jnp.dot(f, wu_ref[:, lo:hi], preferred_element_type=F32)
            hidden = (jax.nn.silu(gate) * up).astype(BF16)
            ffn = ffn + jnp.dot(hidden, wd_ref[lo:hi, :], preferred_element_type=F32)
        out_ref[rows, :] = _rms_norm(h + ffn, gfin_ref[...])


def _merge_ffn(x2, outs, maxes, sums, pool_feat, gates, w_att_branch, w_pool_branch, w_out, norm_ffn,
               w_gate, w_up, w_down, norm_final, seq_len):
    tokens, d_model = x2.shape
    rows = FFN_ROWS
    tiles_per_seq = seq_len // rows

    def tile(width):
        return pl.BlockSpec((rows, width), lambda i: (i, 0))

    def whole(arr):
        return pl.BlockSpec(arr.shape, lambda i: (0, 0), pipeline_mode=pl.Buffered(1))

    o_spec = pl.BlockSpec((None, HEAD_PAIRS, rows, LANES),
                          lambda i: (i // tiles_per_seq, 0, i % tiles_per_seq, 0))
    stat_spec = pl.BlockSpec((None, rows, LANES),
                             lambda i: (i // tiles_per_seq, i % tiles_per_seq, 0))
    weights = (_spread_matrix(), w_att_branch, w_pool_branch, w_out, norm_ffn, w_gate, w_up,
               w_down, norm_final)
    return pl.pallas_call(
        _merge_ffn_kernel,
        name="merge_ffn",
        grid=(tokens // rows,),
        in_specs=[tile(d_model)]
        + [o_spec] * N_ATT_GROUPS
        + [stat_spec] * (2 * N_ATT_GROUPS)
        + [tile(POOL_WIDTH), tile(gates.shape[1])]
        + [whole(w) for w in weights],
        out_specs=tile(d_model),
        out_shape=jax.ShapeDtypeStruct((tokens, d_model), F32),
        scratch_shapes=[pltpu.VMEM((rows, ATT_OUT_WIDTH), BF16)],
        compiler_params=pltpu.CompilerParams(
            dimension_semantics=("arbitrary",), vmem_limit_bytes=V7X_VMEM_LIMIT_BYTES),
    )(x2, *outs, *maxes, *sums, pool_feat, gates, *weights)


def _rope_tables(seq_len):
    inv_freq = ROPE_THETA ** (-jnp.arange(0, ROT_DIM, 2, dtype=F32) / ROT_DIM)
    ang = jnp.arange(seq_len).astype(F32)[:, None] * inv_freq[None, :]
    cos, sin = jnp.cos(ang), jnp.sin(ang)
    zeros = jnp.zeros((seq_len, HEAD_DIM - ROT_DIM), F32)
    half0 = jnp.zeros((seq_len, ROT_HALF), F32)
    c_head = jnp.concatenate([cos, cos, zeros + 1.0], axis=1)
    lo_head = jnp.concatenate([-sin, half0, zeros], axis=1)
    hi_head = jnp.concatenate([half0, sin, zeros], axis=1)
    tab = jnp.stack([c_head, lo_head, hi_head])
    return jnp.concatenate([tab] * (LANES // HEAD_DIM), axis=2)


def _spread_matrix():
    row = jnp.arange(LANES)[:, None]
    col = jnp.arange(N_ATT_GROUPS * ATT_OUT_WIDTH)[None, :]
    group = col // ATT_OUT_WIDTH
    head = (col % ATT_OUT_WIDTH) // HEAD_DIM
    stat = head // 2 + jnp.where(head % 2 == 0, HEAD_DIM, 0)
    hit = jnp.zeros((LANES, N_ATT_GROUPS * ATT_OUT_WIDTH), jnp.bool_)
    for piece in range(SPLIT_PIECES):
        hit = hit | (row == stat + HEAD_PAIRS * (SPLIT_PIECES * group + piece))
    return hit.astype(BF16)


def _band_bias():
    qi = jnp.arange(Q_BLOCK)[:, None]
    kb = jnp.arange(2 * Q_BLOCK)[None, :]
    dist = qi + Q_BLOCK - kb
    valid = (dist >= 0) & (dist <= WINDOW_KEYS)
    both = jnp.stack([valid, valid & (kb >= Q_BLOCK)])
    return jnp.where(both, 0.0, -jnp.inf).astype(F32)


def kernel(x, norm_mix, w_in, w_pool_group, pool_scale, w_att_branch, w_pool_branch, w_out,
           norm_ffn, w_ffn_gate, w_ffn_up, w_ffn_down, norm_final):
    batch, seq_len, d_model = x.shape
    depth = w_in.shape[0]
    tokens = batch * seq_len
    assert depth == 1, "stacked layers need the un-normalised residual between layers"
    bias = _band_bias()

    h = x.reshape(tokens, d_model)
    qkv0, qkv1, qkv2, pool_feat, gates = _in_proj(
        h, norm_mix[0][None, :], w_in[0].astype(BF16), w_pool_group[0].astype(BF16),
        pool_scale[0][None, :], _rope_tables(seq_len), batch, seq_len)
    qkv_groups = (qkv0.reshape(batch, 1, seq_len, GROUP_QKV_WIDTH), qkv1, qkv2)
    att = [_attention_group(qkv_groups[g], bias, batch, seq_len, g) for g in range(N_ATT_GROUPS)]
    out = _merge_ffn(
        h, [a[0] for a in att], [a[1] for a in att], [a[2] for a in att], pool_feat, gates,
        w_att_branch[0].astype(BF16), w_pool_branch[0].astype(BF16), w_out[0].astype(BF16),
        norm_ffn[0][None, :], w_ffn_gate[0].astype(BF16), w_ffn_up[0].astype(BF16),
        w_ffn_down[0].astype(BF16), norm_final[None, :], seq_len)
    return out.reshape(batch, seq_len, d_model)
```

```python
import functools

import jax
import jax.numpy as jnp
from jax import lax
from jax.experimental import pallas as pl
from jax.experimental.pallas import tpu as pltpu

F32 = jnp.float32
BF16 = jnp.bfloat16

HEAD_DIM = 64
HEADS_PER_GROUP = 8
ATT_GROUPS = ((128, 1), (512, 4), (2048, 16))
N_ATT_GROUPS = len(ATT_GROUPS)
ATT_OUT_WIDTH = HEADS_PER_GROUP * HEAD_DIM
ATT_WIDTH = N_ATT_GROUPS * ATT_OUT_WIDTH
QKV_WIDTH = 3 * ATT_WIDTH
GROUP_QKV_WIDTH = 3 * ATT_OUT_WIDTH
ROT_DIM = HEAD_DIM // 4
ROT_HALF = ROT_DIM // 2
ROPE_THETA = 500000.0
POOL_WINDOWS = (2, 4, 8, 16)
POOL_GROUP_WIDTH = 128
POOL_WIDTH = len(POOL_WINDOWS) * POOL_GROUP_WIDTH
NORM_EPS = 1e-6
WINDOW_KEYS = 128

LANES = 128
V7X_VMEM_LIMIT_BYTES = 56 * 1024 * 1024

PROJ_ROWS = 512
PROJ_SUB_ROWS = 256
PROJ_COLS = 512
POOL_HALO = 16
ATT_ROWS = 1024
Q_BLOCK = 128
HEAD_PAIRS = HEADS_PER_GROUP // 2
FFN_ROWS = 512
FFN_SUB_ROWS = 256
FFN_CHUNKS = ((0, 1024), (1024, 2048), (2048, 2816))
SPLIT_PIECES = 3


def _rms_norm(xf, gain):
    ms = jnp.mean(xf * xf, axis=-1, keepdims=True)
    return (xf * lax.rsqrt(ms + NORM_EPS)) * gain


def _in_proj_kernel(x_ref, gain_ref, w_ref, wpg_ref, pscale_ref, rope_ref,
                    qkv0_ref, qkv1_ref, qkv2_ref, pool_ref, gate_ref,
                    u_ref, u4_ref, u16_ref, uslab_ref, u4slab_ref, tab4_ref, tab16_ref,
                    z0_ref, z1_ref, z2_ref, z3_ref, *, tiles_per_seq):
    rows, d_model = x_ref.shape
    seq_tile = pl.program_id(0) % tiles_per_seq
    n_slabs = d_model // LANES
    sub_rows = PROJ_SUB_ROWS
    n_sub = rows // sub_rows
    run4 = sub_rows // 4
    run16 = sub_rows // 16
    levels = (z0_ref, z1_ref, z2_ref, z3_ref)

    @pl.when(seq_tile == 0)
    def _():
        for ref in levels:
            ref[0:POOL_HALO, :] = jnp.zeros((POOL_HALO, POOL_WIDTH), F32)

    for sub in range(n_sub):
        row0 = sub * sub_rows
        uf = _rms_norm(x_ref[row0:row0 + sub_rows, :], gain_ref[...])
        u_ref[sub] = uf.astype(BF16)
        for s in range(n_slabs):
            uslab_ref[s] = uf[:, s * LANES:(s + 1) * LANES]
        for r in range(4):
            dst = slice(r * run4, (r + 1) * run4)
            for s in range(n_slabs):
                piece = uslab_ref[s, pl.ds(r, run4, stride=4), :]
                u4slab_ref[s, dst, :] = piece
                u4_ref[sub, dst, s * LANES:(s + 1) * LANES] = piece.astype(BF16)
            for t in range(3):
                tab4_ref[sub, t, dst, :] = rope_ref[t, pl.ds(row0 + r, run4, stride=4), :]
        for run in range(16):
            src0 = (run // 4) * run4 + run % 4
            dst = slice(run * run16, (run + 1) * run16)
            for s in range(n_slabs):
                piece = u4slab_ref[s, pl.ds(src0, run16, stride=4), :]
                u16_ref[sub, dst, s * LANES:(s + 1) * LANES] = piece.astype(BF16)
            for t in range(3):
                tab16_ref[sub, t, dst, :] = tab4_ref[sub, t, pl.ds(src0, run16, stride=4), :]

    def project(lhs_ref, sub, col0):
        return jnp.dot(lhs_ref[sub], w_ref[:, col0:col0 + PROJ_COLS],
                       preferred_element_type=F32)

    for sub in range(n_sub):
        row0 = POOL_HALO + sub * sub_rows
        z0_ref[row0:row0 + sub_rows, :] = project(u_ref, sub, QKV_WIDTH)

    cur = slice(POOL_HALO, POOL_HALO + rows)
    sums = []
    prev = z0_ref[cur, :]
    for lvl, ref in enumerate(levels):
        shift = 1 << lvl
        lo = lvl * POOL_GROUP_WIDTH
        shifted = ref[POOL_HALO - shift:POOL_HALO - shift + rows, lo:]
        prev = prev[:, (POOL_GROUP_WIDTH if lvl else 0):] + shifted
        sums.append(prev[:, 0:POOL_GROUP_WIDTH])
        if lvl + 1 < len(levels):
            levels[lvl + 1][cur, lo:] = prev
    pos = seq_tile * rows + lax.broadcasted_iota(jnp.int32, (rows, 1), 0)
    pooled = []
    for g, w in enumerate(POOL_WINDOWS):
        cols = slice(g * POOL_GROUP_WIDTH, (g + 1) * POOL_GROUP_WIDTH)
        count = jnp.minimum(pos + 1, w).astype(F32)
        pooled.append((sums[g] / count - z0_ref[cur, cols]).astype(BF16))
    for lvl, ref in enumerate(levels):
        lo = lvl * POOL_GROUP_WIDTH
        ref[0:POOL_HALO, lo:] = ref[rows:rows + POOL_HALO, lo:]

    def rope(xs, table):
        up = pltpu.roll(xs, LANES - ROT_HALF, axis=1)
        down = pltpu.roll(xs, ROT_HALF, axis=1)
        return xs * table(0) + up * table(1) + down * table(2)

    def store_natural(sub, col, part):
        qkv0_ref[sub * sub_rows:(sub + 1) * sub_rows, col:col + LANES] = part

    def store_by4(sub, col, part):
        for r in range(4):
            qkv1_ref[r, sub * run4:(sub + 1) * run4, col:col + LANES] = (
                part[r * run4:(r + 1) * run4])

    def store_by16(sub, col, part):
        for run in range(16):
            stream = 4 * (run % 4) + run // 4
            qkv2_ref[stream, sub * run16:(sub + 1) * run16, col:col + LANES] = (
                part[run * run16:(run + 1) * run16])

    for sub in range(n_sub):
        natural = slice(sub * sub_rows, (sub + 1) * sub_rows)
        variants = (
            (u_ref, lambda t: rope_ref[t, natural, :], store_natural),
            (u4_ref, lambda t: tab4_ref[sub, t], store_by4),
            (u16_ref, lambda t: tab16_ref[sub, t], store_by16),
        )
        for group, (lhs_ref, table, store) in enumerate(variants):
            for which in range(3):
                acc = project(lhs_ref, sub, which * ATT_WIDTH + group * ATT_OUT_WIDTH)
                for j in range(PROJ_COLS // LANES):
                    part = acc[:, j * LANES:(j + 1) * LANES]
                    if which == 0:
                        part = rope(part, table) * (HEAD_DIM ** -0.5)
                    elif which == 1:
                        part = rope(part, table)
                    store(sub, which * ATT_OUT_WIDTH + j * LANES, part.astype(BF16))

    gate0 = QKV_WIDTH + POOL_WIDTH
    for sub in range(n_sub):
        natural = slice(sub * sub_rows, (sub + 1) * sub_rows)
        for c in range(gate_ref.shape[1] // PROJ_COLS):
            gate_ref[natural, c * PROJ_COLS:(c + 1) * PROJ_COLS] = (
                project(u_ref, sub, gate0 + c * PROJ_COLS))

    for g in range(len(POOL_WINDOWS)):
        cols = slice(g * POOL_GROUP_WIDTH, (g + 1) * POOL_GROUP_WIDTH)
        mapped = jnp.dot(pooled[g], wpg_ref[g], preferred_element_type=F32)
        pool_ref[:, cols] = (mapped * pscale_ref[:, cols]).astype(BF16)


def _in_proj(x2, gain, w_in, w_pool_group, pool_scale, rope_tab, batch, seq_len):
    tokens, d_model = x2.shape
    in_width = w_in.shape[1]
    gate_width = in_width - QKV_WIDTH - POOL_WIDTH
    rows = PROJ_ROWS
    n_sub = rows // PROJ_SUB_ROWS
    tiles_per_seq = seq_len // rows
    const = dict(pipeline_mode=pl.Buffered(1))

    def stream_spec(dilation):
        return pl.BlockSpec((None, dilation, rows // dilation, GROUP_QKV_WIDTH),
                            lambda i: (i // tiles_per_seq, 0, i % tiles_per_seq, 0))

    return pl.pallas_call(
        functools.partial(_in_proj_kernel, tiles_per_seq=tiles_per_seq),
        name="in_proj",
        grid=(tokens // rows,),
        in_specs=[
            pl.BlockSpec((rows, d_model), lambda i: (i, 0)),
            pl.BlockSpec((1, d_model), lambda i: (0, 0), **const),
            pl.BlockSpec((d_model, in_width), lambda i: (0, 0), **const),
            pl.BlockSpec(w_pool_group.shape, lambda i: (0, 0, 0), **const),
            pl.BlockSpec((1, POOL_WIDTH), lambda i: (0, 0), **const),
            pl.BlockSpec((3, rows, LANES), lambda i: (0, i % tiles_per_seq, 0)),
        ],
        out_specs=[
            pl.BlockSpec((rows, GROUP_QKV_WIDTH), lambda i: (i, 0)),
            stream_spec(4),
            stream_spec(16),
            pl.BlockSpec((rows, POOL_WIDTH), lambda i: (i, 0)),
            pl.BlockSpec((rows, gate_width), lambda i: (i, 0)),
        ],
        out_shape=[
            jax.ShapeDtypeStruct((tokens, GROUP_QKV_WIDTH), BF16),
            jax.ShapeDtypeStruct((batch, 4, seq_len // 4, GROUP_QKV_WIDTH), BF16),
            jax.ShapeDtypeStruct((batch, 16, seq_len // 16, GROUP_QKV_WIDTH), BF16),
            jax.ShapeDtypeStruct((tokens, POOL_WIDTH), BF16),
            jax.ShapeDtypeStruct((tokens, gate_width), F32),
        ],
        scratch_shapes=[pltpu.VMEM((n_sub, PROJ_SUB_ROWS, d_model), BF16)] * 3
        + [pltpu.VMEM((d_model // LANES, PROJ_SUB_ROWS, LANES), F32)] * 2
        + [pltpu.VMEM((n_sub, 3, PROJ_SUB_ROWS, LANES), F32)] * 2
        + [pltpu.VMEM((POOL_HALO + rows, POOL_WIDTH), F32) for _ in POOL_WINDOWS],
        compiler_params=pltpu.CompilerParams(
            dimension_semantics=("arbitrary",), vmem_limit_bytes=V7X_VMEM_LIMIT_BYTES),
    )(x2, gain, w_in, w_pool_group, pool_scale, rope_tab)


def _stat_lane(head):
    return head // 2 + (HEAD_DIM if head % 2 == 0 else 0)


def _piece_shift(group, piece):
    return HEAD_PAIRS * (SPLIT_PIECES * group + piece)


def _attention_kernel(q_ref, k_ref, v_ref, bias_ref, o_ref, m_ref, l_ref, kcarry_ref, vaug_ref,
                      *, dilation, single_step):
    n_streams, rows, _ = q_ref.shape
    step = pl.program_id(2)
    first = step == 0

    lane = lax.broadcasted_iota(jnp.int32, (Q_BLOCK, LANES), 1)
    low_half = lane < HEAD_DIM
    head_mask = (jnp.where(low_half, 1.0, 0.0).astype(BF16),
                 jnp.where(low_half, 0.0, 1.0).astype(BF16))
    low_rows = lax.broadcasted_iota(jnp.int32, (rows, LANES), 1) < HEAD_DIM
    one = jnp.ones((rows, LANES), BF16)
    nt_dims = (((1,), (1,)), ((), ()))
    zero_block = jnp.zeros((Q_BLOCK, ATT_OUT_WIDTH), BF16)

    def clear_carry():
        for st in range(n_streams):
            kcarry_ref[st, 0:Q_BLOCK, :] = zero_block
            vaug_ref[0, st, 0:Q_BLOCK, :] = zero_block
            vaug_ref[1, st, 0:Q_BLOCK, :] = zero_block

    if single_step:
        clear_carry()
    else:
        pl.when(first)(clear_carry)

    for st in range(n_streams):
        stream = pl.program_id(1) * n_streams + st
        kcarry_ref[st, Q_BLOCK:2 * Q_BLOCK, :] = k_ref[st, 0:Q_BLOCK, :]
        for pair in range(HEAD_PAIRS):
            cols = slice(pair * LANES, (pair + 1) * LANES)
            v_pair = v_ref[st, :, cols]
            vaug_ref[0, st, Q_BLOCK:Q_BLOCK + rows, cols] = jnp.where(low_rows, v_pair, one)
            vaug_ref[1, st, Q_BLOCK:Q_BLOCK + rows, cols] = jnp.where(low_rows, one, v_pair)

        for sub in range(rows // Q_BLOCK):
            band = slice(sub * Q_BLOCK, (sub + 2) * Q_BLOCK)
            q_rows = slice(sub * Q_BLOCK, (sub + 1) * Q_BLOCK)
            if sub == 0:
                bias_idx = 1 if single_step else jnp.where(first, 1, 0)
            else:
                bias_idx = 0
            first_row = step * rows + sub * Q_BLOCK
            if dilation == 1:
                tokens = pl.ds(pl.multiple_of(first_row, Q_BLOCK), Q_BLOCK)
            else:
                tokens = pl.ds(first_row * dilation + stream, Q_BLOCK, stride=dilation)
            m_tile = jnp.zeros((Q_BLOCK, LANES), F32)
            l_tile = jnp.ones((Q_BLOCK, LANES), F32)
            for pair in range(HEAD_PAIRS):
                cols = slice(pair * LANES, (pair + 1) * LANES)
                q_pair = q_ref[st, q_rows, cols]
                if sub == 0:
                    k_pair = kcarry_ref[st, :, cols]
                else:
                    k_pair = k_ref[st, (sub - 1) * Q_BLOCK:(sub + 1) * Q_BLOCK, cols]
                pvs = []
                for half in range(2):
                    q_head = q_pair * head_mask[half]
                    s = lax.dot_general(q_head, k_pair, nt_dims, preferred_element_type=F32)
                    s = s + bias_ref[bias_idx]
                    m = jnp.max(s, axis=-1, keepdims=True)
                    p = jnp.exp(s - m).astype(BF16)
                    pv = jnp.dot(p, vaug_ref[half, st, band, cols], preferred_element_type=F32)
                    pvs.append(pv)
                    at_stat = lane == _stat_lane(2 * pair + half)
                    m_tile = jnp.where(at_stat, m, m_tile)
                    l_tile = jnp.where(at_stat, pv, l_tile)
                o_ref[pair, tokens, :] = jnp.where(low_half, pvs[0], pvs[1])
            m_ref[tokens, :] = m_tile
            l_ref[tokens, :] = l_tile

        kcarry_ref[st, 0:Q_BLOCK, :] = k_ref[st, rows - Q_BLOCK:rows, :]
        for half in range(2):
            vaug_ref[half, st, 0:Q_BLOCK, :] = vaug_ref[half, st, rows:rows + Q_BLOCK, :]


def _attention_group(qkv, bias, batch, seq_len, group):
    _, dilation = ATT_GROUPS[group]
    stream_len = seq_len // dilation
    rows = min(ATT_ROWS, stream_len)
    n_streams = ATT_ROWS // rows
    single_step = rows == stream_len

    def qkv_spec(which):
        return pl.BlockSpec((None, n_streams, rows, ATT_OUT_WIDTH),
                            lambda b, r, m: (b, r, m, which))

    stat_spec = pl.BlockSpec((None, seq_len, LANES), lambda b, r, m: (b, 0, 0))
    stat_shape = jax.ShapeDtypeStruct((batch, seq_len, LANES), F32)
    return pl.pallas_call(
        functools.partial(_attention_kernel, dilation=dilation, single_step=single_step),
        name=f"attention_g{group}",
        grid=(batch, dilation // n_streams, stream_len // rows),
        in_specs=[
            qkv_spec(0), qkv_spec(1), qkv_spec(2),
            pl.BlockSpec(bias.shape, lambda b, r, m: (0, 0, 0), pipeline_mode=pl.Buffered(1)),
        ],
        out_specs=[
            pl.BlockSpec((None, HEAD_PAIRS, seq_len, LANES), lambda b, r, m: (b, 0, 0, 0)),
            stat_spec, stat_spec,
        ],
        out_shape=[
            jax.ShapeDtypeStruct((batch, HEAD_PAIRS, seq_len, LANES), F32),
            stat_shape, stat_shape,
        ],
        scratch_shapes=[
            pltpu.VMEM((n_streams, 2 * Q_BLOCK, ATT_OUT_WIDTH), BF16),
            pltpu.VMEM((2, n_streams, Q_BLOCK + rows, ATT_OUT_WIDTH), BF16),
        ],
        compiler_params=pltpu.CompilerParams(
            dimension_semantics=("arbitrary", "arbitrary", "arbitrary"),
            vmem_limit_bytes=V7X_VMEM_LIMIT_BYTES),
    )(qkv, qkv, qkv, bias)


def _merge_ffn_kernel(x_ref, o0_ref, o1_ref, o2_ref, m0_ref, m1_ref, m2_ref, l0_ref, l1_ref, l2_ref,
                      pool_ref, gate_ref, spread_ref,
                      wab_ref, wpb_ref, wout_ref, gffn_ref, wg_ref, wu_ref, wd_ref, gfin_ref,
                      out_ref, att_ref):
    d_model = x_ref.shape[1]
    o_refs = (o0_ref, o1_ref, o2_ref)
    m_refs = (m0_ref, m1_ref, m2_ref)
    l_refs = (l0_ref, l1_ref, l2_ref)
    lane = lax.broadcasted_iota(jnp.int32, (FFN_SUB_ROWS, LANES), 1)
    stat_lanes = (lane % HEAD_DIM) < HEAD_PAIRS

    for sub in range(x_ref.shape[0] // FFN_SUB_ROWS):
        rows = slice(sub * FFN_SUB_ROWS, (sub + 1) * FFN_SUB_ROWS)

        ms = [ref[rows, :] for ref in m_refs]
        ls = [ref[rows, :] for ref in l_refs]
        top = jnp.maximum(jnp.maximum(ms[0], ms[1]), ms[2])
        es = [jnp.exp(m - top) for m in ms]
        den = es[0] * ls[0] + es[1] * ls[1] + es[2] * ls[2]

        packed = jnp.zeros((FFN_SUB_ROWS, LANES), F32)
        for g in range(N_ATT_GROUPS):
            rest = jnp.where(stat_lanes, es[g] / den, 0.0)
            for piece in range(SPLIT_PIECES):
                part = rest.astype(BF16).astype(F32)
                rest = rest - part
                shift = _piece_shift(g, piece)
                packed = packed + (pltpu.roll(part, shift, axis=1) if shift else part)
        spread = jnp.dot(packed.astype(BF16), spread_ref[...], preferred_element_type=F32)

        for pair in range(HEAD_PAIRS):
            acc = jnp.zeros((FFN_SUB_ROWS, LANES), F32)
            for g in range(N_ATT_GROUPS):
                col = g * ATT_OUT_WIDTH + pair * LANES
                acc = acc + spread[:, col:col + LANES] * o_refs[g][pair, rows, :]
            att_ref[rows, pair * LANES:(pair + 1) * LANES] = acc.astype(BF16)

        y_att = jnp.dot(att_ref[rows, :], wab_ref[...], preferred_element_type=F32)
        y_pool = jnp.dot(pool_ref[rows, :], wpb_ref[...], preferred_element_type=F32)
        merged = (jax.nn.sigmoid(gate_ref[rows, 0:d_model]) * y_att
                  + jax.nn.sigmoid(gate_ref[rows, d_model:2 * d_model]) * y_pool)
        h = x_ref[rows, :] + jnp.dot(merged.astype(BF16), wout_ref[...],
                                     preferred_element_type=F32)

        f = _rms_norm(h, gffn_ref[...]).astype(BF16)
        ffn = jnp.zeros((FFN_SUB_ROWS, d_model), F32)
        for lo, hi in FFN_CHUNKS:
            gate = jnp.dot(f, wg_ref[:, lo:hi], preferred_element_type=F32)
            up = jnp.dot(f, wu_ref[:, lo:hi], preferred_element_type=F32)
            hidden = (jax.nn.silu(gate) * up).astype(BF16)
            ffn = ffn + jnp.dot(hidden, wd_ref[lo:hi, :], preferred_element_type=F32)
        out_ref[rows, :] = _rms_norm(h + ffn, gfin_ref[...])


def _merge_ffn(x2, outs, maxes, sums, pool_feat, gates, w_att_branch, w_pool_branch, w_out, norm_ffn,
               w_gate, w_up, w_down, norm_final, seq_len):
    tokens, d_model = x2.shape
    rows = FFN_ROWS
    tiles_per_seq = seq_len // rows

    def tile(width):
        return pl.BlockSpec((rows, width), lambda i: (i, 0))

    def whole(arr):
        return pl.BlockSpec(arr.shape, lambda i: (0, 0), pipeline_mode=pl.Buffered(1))

    o_spec = pl.BlockSpec((None, HEAD_PAIRS, rows, LANES),
                          lambda i: (i // tiles_per_seq, 0, i % tiles_per_seq, 0))
    stat_spec = pl.BlockSpec((None, rows, LANES),
                             lambda i: (i // tiles_per_seq, i % tiles_per_seq, 0))
    weights = (_spread_matrix(), w_att_branch, w_pool_branch, w_out, norm_ffn, w_gate, w_up,
               w_down, norm_final)
    return pl.pallas_call(
        _merge_ffn_kernel,
        name="merge_ffn",
        grid=(tokens // rows,),
        in_specs=[tile(d_model)]
        + [o_spec] * N_ATT_GROUPS
        + [stat_spec] * (2 * N_ATT_GROUPS)
        + [tile(POOL_WIDTH), tile(gates.shape[1])]
        + [whole(w) for w in weights],
        out_specs=tile(d_model),
        out_shape=jax.ShapeDtypeStruct((tokens, d_model), F32),
        scratch_shapes=[pltpu.VMEM((rows, ATT_OUT_WIDTH), BF16)],
        compiler_params=pltpu.CompilerParams(
            dimension_semantics=("arbitrary",), vmem_limit_bytes=V7X_VMEM_LIMIT_BYTES),
    )(x2, *outs, *maxes, *sums, pool_feat, gates, *weights)


def _rope_tables(seq_len):
    inv_freq = ROPE_THETA ** (-jnp.arange(0, ROT_DIM, 2, dtype=F32) / ROT_DIM)
    ang = jnp.arange(seq_len).astype(F32)[:, None] * inv_freq[None, :]
    cos, sin = jnp.cos(ang), jnp.sin(ang)
    zeros = jnp.zeros((seq_len, HEAD_DIM - ROT_DIM), F32)
    half0 = jnp.zeros((seq_len, ROT_HALF), F32)
    c_head = jnp.concatenate([cos, cos, zeros + 1.0], axis=1)
    lo_head = jnp.concatenate([-sin, half0, zeros], axis=1)
    hi_head = jnp.concatenate([half0, sin, zeros], axis=1)
    tab = jnp.stack([c_head, lo_head, hi_head])
    return jnp.concatenate([tab] * (LANES // HEAD_DIM), axis=2)


def _spread_matrix():
    row = jnp.arange(LANES)[:, None]
    col = jnp.arange(N_ATT_GROUPS * ATT_OUT_WIDTH)[None, :]
    group = col // ATT_OUT_WIDTH
    head = (col % ATT_OUT_WIDTH) // HEAD_DIM
    stat = head // 2 + jnp.where(head % 2 == 0, HEAD_DIM, 0)
    hit = jnp.zeros((LANES, N_ATT_GROUPS * ATT_OUT_WIDTH), jnp.bool_)
    for piece in range(SPLIT_PIECES):
        hit = hit | (row == stat + HEAD_PAIRS * (SPLIT_PIECES * group + piece))
    return hit.astype(BF16)


def _band_bias():
    qi = jnp.arange(Q_BLOCK)[:, None]
    kb = jnp.arange(2 * Q_BLOCK)[None, :]
    dist = qi + Q_BLOCK - kb
    valid = (dist >= 0) & (dist <= WINDOW_KEYS)
    both = jnp.stack([valid, valid & (kb >= Q_BLOCK)])
    return jnp.where(both, 0.0, -jnp.inf).astype(F32)


def kernel(x, norm_mix, w_in, w_pool_group, pool_scale, w_att_branch, w_pool_branch, w_out,
           norm_ffn, w_ffn_gate, w_ffn_up, w_ffn_down, norm_final):
    batch, seq_len, d_model = x.shape
    depth = w_in.shape[0]
    tokens = batch * seq_len
    assert depth == 1, "stacked layers need the un-normalised residual between layers"
    bias = _band_bias()

    h = x.reshape(tokens, d_model)
    qkv0, qkv1, qkv2, pool_feat, gates = _in_proj(
        h, norm_mix[0][None, :], w_in[0].astype(BF16), w_pool_group[0].astype(BF16),
        pool_scale[0][None, :], _rope_tables(seq_len), batch, seq_len)
    qkv_groups = (qkv0.reshape(batch, 1, seq_len, GROUP_QKV_WIDTH), qkv1, qkv2)
    att = [_attention_group(qkv_groups[g], bias, batch, seq_len, g) for g in range(N_ATT_GROUPS)]
    out = _merge_ffn(
        h, [a[0] for a in att], [a[1] for a in att], [a[2] for a in att], pool_feat, gates,
        w_att_branch[0].astype(BF16), w_pool_branch[0].astype(BF16), w_out[0].astype(BF16),
        norm_ffn[0][None, :], w_ffn_gate[0].astype(BF16), w_ffn_up[0].astype(BF16),
        w_ffn_down[0].astype(BF16), norm_final[None, :], seq_len)
    return out.reshape(batch, seq_len, d_model)
```

```python
import functools

import jax
import jax.numpy as jnp
from jax import lax
from jax.experimental import pallas as pl
from jax.experimental.pallas import tpu as pltpu

F32 = jnp.float32
BF16 = jnp.bfloat16

HEAD_DIM = 64
HEADS_PER_GROUP = 8
ATT_GROUPS = ((128, 1), (512, 4), (2048, 16))
N_ATT_GROUPS = len(ATT_GROUPS)
ATT_OUT_WIDTH = HEADS_PER_GROUP * HEAD_DIM
ATT_WIDTH = N_ATT_GROUPS * ATT_OUT_WIDTH
QKV_WIDTH = 3 * ATT_WIDTH
GROUP_QKV_WIDTH = 3 * ATT_OUT_WIDTH
ROT_DIM = HEAD_DIM // 4
ROT_HALF = ROT_DIM // 2
ROPE_THETA = 500000.0
POOL_WINDOWS = (2, 4, 8, 16)
POOL_GROUP_WIDTH = 128
POOL_WIDTH = len(POOL_WINDOWS) * POOL_GROUP_WIDTH
NORM_EPS = 1e-6
WINDOW_KEYS = 128

LANES = 128
V7X_VMEM_LIMIT_BYTES = 56 * 1024 * 1024

PROJ_ROWS = 512
PROJ_SUB_ROWS = 256
PROJ_COLS = 512
POOL_HALO = 16
ATT_ROWS = 1024
Q_BLOCK = 128
HEAD_PAIRS = HEADS_PER_GROUP // 2
FFN_ROWS = 512
FFN_SUB_ROWS = 256
FFN_CHUNKS = ((0, 1024), (1024, 2048), (2048, 2816))
SPLIT_PIECES = 3


def _rms_norm(xf, gain):
    ms = jnp.mean(xf * xf, axis=-1, keepdims=True)
    return (xf * lax.rsqrt(ms + NORM_EPS)) * gain


def _in_proj_kernel(x_ref, gain_ref, w_ref, wpg_ref, pscale_ref, rope_ref,
                    qkv0_ref, qkv1_ref, qkv2_ref, pool_ref, gate_ref,
                    u_ref, u4_ref, u16_ref, uslab_ref, u4slab_ref, tab4_ref, tab16_ref,
                    z0_ref, z1_ref, z2_ref, z3_ref, *, tiles_per_seq):
    rows, d_model = x_ref.shape
    seq_tile = pl.program_id(0) % tiles_per_seq
    n_slabs = d_model // LANES
    sub_rows = PROJ_SUB_ROWS
    n_sub = rows // sub_rows
    run4 = sub_rows // 4
    run16 = sub_rows // 16
    levels = (z0_ref, z1_ref, z2_ref, z3_ref)

    @pl.when(seq_tile == 0)
    def _():
        for ref in levels:
            ref[0:POOL_HALO, :] = jnp.zeros((POOL_HALO, POOL_WIDTH), F32)

    for sub in range(n_sub):
        row0 = sub * sub_rows
        uf = _rms_norm(x_ref[row0:row0 + sub_rows, :], gain_ref[...])
        u_ref[sub] = uf.astype(BF16)
        for s in range(n_slabs):
            uslab_ref[s] = uf[:, s * LANES:(s + 1) * LANES]
        for r in range(4):
            dst = slice(r * run4, (r + 1) * run4)
            for s in range(n_slabs):
                piece = uslab_ref[s, pl.ds(r, run4, stride=4), :]
                u4slab_ref[s, dst, :] = piece
                u4_ref[sub, dst, s * LANES:(s + 1) * LANES] = piece.astype(BF16)
            for t in range(3):
                tab4_ref[sub, t, dst, :] = rope_ref[t, pl.ds(row0 + r, run4, stride=4), :]
        for run in range(16):
            src0 = (run // 4) * run4 + run % 4
            dst = slice(run * run16, (run + 1) * run16)
            for s in range(n_slabs):
                piece = u4slab_ref[s, pl.ds(src0, run16, stride=4), :]
                u16_ref[sub, dst, s * LANES:(s + 1) * LANES] = piece.astype(BF16)
            for t in range(3):
                tab16_ref[sub, t, dst, :] = tab4_ref[sub, t, pl.ds(src0, run16, stride=4), :]

    def project(lhs_ref, sub, col0):
        return jnp.dot(lhs_ref[sub], w_ref[:, col0:col0 + PROJ_COLS],
                       preferred_element_type=F32)

    for sub in range(n_sub):
        row0 = POOL_HALO + sub * sub_rows
        z0_ref[row0:row0 + sub_rows, :] = project(u_ref, sub, QKV_WIDTH)

    cur = slice(POOL_HALO, POOL_HALO + rows)
    sums = []
    prev = z0_ref[cur, :]
    for lvl, ref in enumerate(levels):
        shift = 1 << lvl
        lo = lvl * POOL_GROUP_WIDTH
        shifted = ref[POOL_HALO - shift:POOL_HALO - shift + rows, lo:]
        prev = prev[:, (POOL_GROUP_WIDTH if lvl else 0):] + shifted
        sums.append(prev[:, 0:POOL_GROUP_WIDTH])
        if lvl + 1 < len(levels):
            levels[lvl + 1][cur, lo:] = prev
    pos = seq_tile * rows + lax.broadcasted_iota(jnp.int32, (rows, 1), 0)
    pooled = []
    for g, w in enumerate(POOL_WINDOWS):
        cols = slice(g * POOL_GROUP_WIDTH, (g + 1) * POOL_GROUP_WIDTH)
        count = jnp.minimum(pos + 1, w).astype(F32)
        pooled.append((sums[g] / count - z0_ref[cur, cols]).astype(BF16))
    for lvl, ref in enumerate(levels):
        lo = lvl * POOL_GROUP_WIDTH
        ref[0:POOL_HALO, lo:] = ref[rows:rows + POOL_HALO, lo:]

    def rope(xs, table):
        up = pltpu.roll(xs, LANES - ROT_HALF, axis=1)
        down = pltpu.roll(xs, ROT_HALF, axis=1)
        return xs * table(0) + up * table(1) + down * table(2)

    def store_natural(sub, col, part):
        qkv0_ref[sub * sub_rows:(sub + 1) * sub_rows, col:col + LANES] = part

    def store_by4(sub, col, part):
        for r in range(4):
            qkv1_ref[r, sub * run4:(sub + 1) * run4, col:col + LANES] = (
                part[r * run4:(r + 1) * run4])

    def store_by16(sub, col, part):
        for run in range(16):
            stream = 4 * (run % 4) + run // 4
            qkv2_ref[stream, sub * run16:(sub + 1) * run16, col:col + LANES] = (
                part[run * run16:(run + 1) * run16])

    for sub in range(n_sub):
        natural = slice(sub * sub_rows, (sub + 1) * sub_rows)
        variants = (
            (u_ref, lambda t: rope_ref[t, natural, :], store_natural),
            (u4_ref, lambda t: tab4_ref[sub, t], store_by4),
            (u16_ref, lambda t: tab16_ref[sub, t], store_by16),
        )
        for group, (lhs_ref, table, store) in enumerate(variants):
            for which in range(3):
                acc = project(lhs_ref, sub, which * ATT_WIDTH + group * ATT_OUT_WIDTH)
                for j in range(PROJ_COLS // LANES):
                    part = acc[:, j * LANES:(j + 1) * LANES]
                    if which == 0:
                        part = rope(part, table) * (HEAD_DIM ** -0.5)
                    elif which == 1:
                        part = rope(part, table)
                    store(sub, which * ATT_OUT_WIDTH + j * LANES, part.astype(BF16))

    gate0 = QKV_WIDTH + POOL_WIDTH
    for sub in range(n_sub):
        natural = slice(sub * sub_rows, (sub + 1) * sub_rows)
        for c in range(gate_ref.shape[1] // PROJ_COLS):
            gate_ref[natural, c * PROJ_COLS:(c + 1) * PROJ_COLS] = (
                project(u_ref, sub, gate0 + c * PROJ_COLS))

    for g in range(len(POOL_WINDOWS)):
        cols = slice(g * POOL_GROUP_WIDTH, (g + 1) * POOL_GROUP_WIDTH)
        mapped = jnp.dot(pooled[g], wpg_ref[g], preferred_element_type=F32)
        pool_ref[:, cols] = (mapped * pscale_ref[:, cols]).astype(BF16)


def _in_proj(x2, gain, w_in, w_pool_group, pool_scale, rope_tab, batch, seq_len):
    tokens, d_model = x2.shape
    in_width = w_in.shape[1]
    gate_width = in_width - QKV_WIDTH - POOL_WIDTH
    rows = PROJ_ROWS
    n_sub = rows // PROJ_SUB_ROWS
    tiles_per_seq = seq_len // rows
    const = dict(pipeline_mode=pl.Buffered(1))

    def stream_spec(dilation):
        return pl.BlockSpec((None, dilation, rows // dilation, GROUP_QKV_WIDTH),
                            lambda i: (i // tiles_per_seq, 0, i % tiles_per_seq, 0))

    return pl.pallas_call(
        functools.partial(_in_proj_kernel, tiles_per_seq=tiles_per_seq),
        name="in_proj",
        grid=(tokens // rows,),
        in_specs=[
            pl.BlockSpec((rows, d_model), lambda i: (i, 0)),
            pl.BlockSpec((1, d_model), lambda i: (0, 0), **const),
            pl.BlockSpec((d_model, in_width), lambda i: (0, 0), **const),
            pl.BlockSpec(w_pool_group.shape, lambda i: (0, 0, 0), **const),
            pl.BlockSpec((1, POOL_WIDTH), lambda i: (0, 0), **const),
            pl.BlockSpec((3, rows, LANES), lambda i: (0, i % tiles_per_seq, 0)),
        ],
        out_specs=[
            pl.BlockSpec((rows, GROUP_QKV_WIDTH), lambda i: (i, 0)),
            stream_spec(4),
            stream_spec(16),
            pl.BlockSpec((rows, POOL_WIDTH), lambda i: (i, 0)),
            pl.BlockSpec((rows, gate_width), lambda i: (i, 0)),
        ],
        out_shape=[
            jax.ShapeDtypeStruct((tokens, GROUP_QKV_WIDTH), BF16),
            jax.ShapeDtypeStruct((batch, 4, seq_len // 4, GROUP_QKV_WIDTH), BF16),
            jax.ShapeDtypeStruct((batch, 16, seq_len // 16, GROUP_QKV_WIDTH), BF16),
            jax.ShapeDtypeStruct((tokens, POOL_WIDTH), BF16),
            jax.ShapeDtypeStruct((tokens, gate_width), F32),
        ],
        scratch_shapes=[pltpu.VMEM((n_sub, PROJ_SUB_ROWS, d_model), BF16)] * 3
        + [pltpu.VMEM((d_model // LANES, PROJ_SUB_ROWS, LANES), F32)] * 2
        + [pltpu.VMEM((n_sub, 3, PROJ_SUB_ROWS, LANES), F32)] * 2
        + [pltpu.VMEM((POOL_HALO + rows, POOL_WIDTH), F32) for _ in POOL_WINDOWS],
        compiler_params=pltpu.CompilerParams(
            dimension_semantics=("arbitrary",), vmem_limit_bytes=V7X_VMEM_LIMIT_BYTES),
    )(x2, gain, w_in, w_pool_group, pool_scale, rope_tab)


def _stat_lane(head):
    return head // 2 + (HEAD_DIM if head % 2 == 0 else 0)


def _piece_shift(group, piece):
    return HEAD_PAIRS * (SPLIT_PIECES * group + piece)


def _attention_kernel(q_ref, k_ref, v_ref, bias_ref, o_ref, m_ref, l_ref, kcarry_ref, vaug_ref,
                      *, dilation, single_step):
    n_streams, rows, _ = q_ref.shape
    step = pl.program_id(2)
    first = step == 0

    lane = lax.broadcasted_iota(jnp.int32, (Q_BLOCK, LANES), 1)
    low_half = lane < HEAD_DIM
    head_mask = (jnp.where(low_half, 1.0, 0.0).astype(BF16),
                 jnp.where(low_half, 0.0, 1.0).astype(BF16))
    low_rows = lax.broadcasted_iota(jnp.int32, (rows, LANES), 1) < HEAD_DIM
    one = jnp.ones((rows, LANES), BF16)
    nt_dims = (((1,), (1,)), ((), ()))
    zero_block = jnp.zeros((Q_BLOCK, ATT_OUT_WIDTH), BF16)

    def clear_carry():
        for st in range(n_streams):
            kcarry_ref[st, 0:Q_BLOCK, :] = zero_block
            vaug_ref[0, st, 0:Q_BLOCK, :] = zero_block
            vaug_ref[1, st, 0:Q_BLOCK, :] = zero_block

    if single_step:
        clear_carry()
    else:
        pl.when(first)(clear_carry)

    for st in range(n_streams):
        stream = pl.program_id(1) * n_streams + st
        kcarry_ref[st, Q_BLOCK:2 * Q_BLOCK, :] = k_ref[st, 0:Q_BLOCK, :]
        for pair in range(HEAD_PAIRS):
            cols = slice(pair * LANES, (pair + 1) * LANES)
            v_pair = v_ref[st, :, cols]
            vaug_ref[0, st, Q_BLOCK:Q_BLOCK + rows, cols] = jnp.where(low_rows, v_pair, one)
            vaug_ref[1, st, Q_BLOCK:Q_BLOCK + rows, cols] = jnp.where(low_rows, one, v_pair)

        for sub in range(rows // Q_BLOCK):
            band = slice(sub * Q_BLOCK, (sub + 2) * Q_BLOCK)
            q_rows = slice(sub * Q_BLOCK, (sub + 1) * Q_BLOCK)
            if sub == 0:
                bias_idx = 1 if single_step else jnp.where(first, 1, 0)
            else:
                bias_idx = 0
            first_row = step * rows + sub * Q_BLOCK
            if dilation == 1:
                tokens = pl.ds(pl.multiple_of(first_row, Q_BLOCK), Q_BLOCK)
            else:
                tokens = pl.ds(first_row * dilation + stream, Q_BLOCK, stride=dilation)
            m_tile = jnp.zeros((Q_BLOCK, LANES), F32)
            l_tile = jnp.ones((Q_BLOCK, LANES), F32)
            for pair in range(HEAD_PAIRS):
                cols = slice(pair * LANES, (pair + 1) * LANES)
                q_pair = q_ref[st, q_rows, cols]
                if sub == 0:
                    k_pair = kcarry_ref[st, :, cols]
                else:
                    k_pair = k_ref[st, (sub - 1) * Q_BLOCK:(sub + 1) * Q_BLOCK, cols]
                q_both = jnp.concatenate([q_pair * head_mask[0], q_pair * head_mask[1]], axis=0)
                s_both = lax.dot_general(q_both, k_pair, nt_dims, preferred_element_type=F32)
                pvs = []
                for half in range(2):
                    s = s_both[half * Q_BLOCK:(half + 1) * Q_BLOCK] + bias_ref[bias_idx]
                    m = jnp.max(s, axis=-1, keepdims=True)
                    p = jnp.exp(s - m).astype(BF16)
                    pv = jnp.dot(p, vaug_ref[half, st, band, cols], preferred_element_type=F32)
                    pvs.append(pv)
                    at_stat = lane == _stat_lane(2 * pair + half)
                    m_tile = jnp.where(at_stat, m, m_tile)
                    l_tile = jnp.where(at_stat, pv, l_tile)
                o_ref[pair, tokens, :] = jnp.where(low_half, pvs[0], pvs[1])
            m_ref[tokens, :] = m_tile
            l_ref[tokens, :] = l_tile

        kcarry_ref[st, 0:Q_BLOCK, :] = k_ref[st, rows - Q_BLOCK:rows, :]
        for half in range(2):
            vaug_ref[half, st, 0:Q_BLOCK, :] = vaug_ref[half, st, rows:rows + Q_BLOCK, :]


def _attention_group(qkv, bias, batch, seq_len, group):
    _, dilation = ATT_GROUPS[group]
    stream_len = seq_len // dilation
    rows = min(ATT_ROWS, stream_len)
    n_streams = ATT_ROWS // rows
    single_step = rows == stream_len

    def qkv_spec(which):
        return pl.BlockSpec((None, n_streams, rows, ATT_OUT_WIDTH),
                            lambda b, r, m: (b, r, m, which))

    stat_spec = pl.BlockSpec((None, seq_len, LANES), lambda b, r, m: (b, 0, 0))
    stat_shape = jax.ShapeDtypeStruct((batch, seq_len, LANES), F32)
    return pl.pallas_call(
        functools.partial(_attention_kernel, dilation=dilation, single_step=single_step),
        name=f"attention_g{group}",
        grid=(batch, dilation // n_streams, stream_len // rows),
        in_specs=[
            qkv_spec(0), qkv_spec(1), qkv_spec(2),
            pl.BlockSpec(bias.shape, lambda b, r, m: (0, 0, 0), pipeline_mode=pl.Buffered(1)),
        ],
        out_specs=[
            pl.BlockSpec((None, HEAD_PAIRS, seq_len, LANES), lambda b, r, m: (b, 0, 0, 0)),
            stat_spec, stat_spec,
        ],
        out_shape=[
            jax.ShapeDtypeStruct((batch, HEAD_PAIRS, seq_len, LANES), F32),
            stat_shape, stat_shape,
        ],
        scratch_shapes=[
            pltpu.VMEM((n_streams, 2 * Q_BLOCK, ATT_OUT_WIDTH), BF16),
            pltpu.VMEM((2, n_streams, Q_BLOCK + rows, ATT_OUT_WIDTH), BF16),
        ],
        compiler_params=pltpu.CompilerParams(
            dimension_semantics=("arbitrary", "arbitrary", "arbitrary"),
            vmem_limit_bytes=V7X_VMEM_LIMIT_BYTES),
    )(qkv, qkv, qkv, bias)


def _merge_ffn_kernel(x_ref, o0_ref, o1_ref, o2_ref, m0_ref, m1_ref, m2_ref, l0_ref, l1_ref, l2_ref,
                      pool_ref, gate_ref, spread_ref,
                      wab_ref, wpb_ref, wout_ref, gffn_ref, wg_ref, wu_ref, wd_ref, gfin_ref,
                      out_ref, att_ref):
    d_model = x_ref.shape[1]
    o_refs = (o0_ref, o1_ref, o2_ref)
    m_refs = (m0_ref, m1_ref, m2_ref)
    l_refs = (l0_ref, l1_ref, l2_ref)
    lane = lax.broadcasted_iota(jnp.int32, (FFN_SUB_ROWS, LANES), 1)
    stat_lanes = (lane % HEAD_DIM) < HEAD_PAIRS

    for sub in range(x_ref.shape[0] // FFN_SUB_ROWS):
        rows = slice(sub * FFN_SUB_ROWS, (sub + 1) * FFN_SUB_ROWS)

        ms = [ref[rows, :] for ref in m_refs]
        ls = [ref[rows, :] for ref in l_refs]
        top = jnp.maximum(jnp.maximum(ms[0], ms[1]), ms[2])
        es = [jnp.exp(m - top) for m in ms]
        den = es[0] * ls[0] + es[1] * ls[1] + es[2] * ls[2]

        packed = jnp.zeros((FFN_SUB_ROWS, LANES), F32)
        for g in range(N_ATT_GROUPS):
            rest = jnp.where(stat_lanes, es[g] / den, 0.0)
            for piece in range(SPLIT_PIECES):
                part = rest.astype(BF16).astype(F32)
                rest = rest - part
                shift = _piece_shift(g, piece)
                packed = packed + (pltpu.roll(part, shift, axis=1) if shift else part)
        spread = jnp.dot(packed.astype(BF16), spread_ref[...], preferred_element_type=F32)

        for pair in range(HEAD_PAIRS):
            acc = jnp.zeros((FFN_SUB_ROWS, LANES), F32)
            for g in range(N_ATT_GROUPS):
                col = g * ATT_OUT_WIDTH + pair * LANES
                acc = acc + spread[:, col:col + LANES] * o_refs[g][pair, rows, :]
            att_ref[rows, pair * LANES:(pair + 1) * LANES] = acc.astype(BF16)

        y_att = jnp.dot(att_ref[rows, :], wab_ref[...], preferred_element_type=F32)
        y_pool = jnp.dot(pool_ref[rows, :], wpb_ref[...], preferred_element_type=F32)
        merged = (jax.nn.sigmoid(gate_ref[rows, 0:d_model]) * y_att
                  + jax.nn.sigmoid(gate_ref[rows, d_model:2 * d_model]) * y_pool)
        h = x_ref[rows, :] + jnp.dot(merged.astype(BF16), wout_ref[...],
                                     preferred_element_type=F32)

        f = _rms_norm(h, gffn_ref[...]).astype(BF16)
        ffn = jnp.zeros((FFN_SUB_ROWS, d_model), F32)
        for lo, hi in FFN_CHUNKS:
            gate = jnp.dot(f, wg_ref[:, lo:hi], preferred_element_type=F32)
            up = jnp.dot(f, wu_ref[:, lo:hi], preferred_element_type=F32)
            hidden = (jax.nn.silu(gate) * up).astype(BF16)
            ffn = ffn + jnp.dot(hidden, wd_ref[lo:hi, :], preferred_element_type=F32)
        out_ref[rows, :] = _rms_norm(h + ffn, gfin_ref[...])


def _merge_ffn(x2, outs, maxes, sums, pool_feat, gates, w_att_branch, w_pool_branch, w_out, norm_ffn,
               w_gate, w_up, w_down, norm_final, seq_len):
    tokens, d_model = x2.shape
    rows = FFN_ROWS
    tiles_per_seq = seq_len // rows

    def tile(width):
        return pl.BlockSpec((rows, width), lambda i: (i, 0))

    def whole(arr):
        return pl.BlockSpec(arr.shape, lambda i: (0, 0), pipeline_mode=pl.Buffered(1))

    o_spec = pl.BlockSpec((None, HEAD_PAIRS, rows, LANES),
                          lambda i: (i // tiles_per_seq, 0, i % tiles_per_seq, 0))
    stat_spec = pl.BlockSpec((None, rows, LANES),
                             lambda i: (i // tiles_per_seq, i % tiles_per_seq, 0))
    weights = (_spread_matrix(), w_att_branch, w_pool_branch, w_out, norm_ffn, w_gate, w_up,
               w_down, norm_final)
    return pl.pallas_call(
        _merge_ffn_kernel,
        name="merge_ffn",
        grid=(tokens // rows,),
        in_specs=[tile(d_model)]
        + [o_spec] * N_ATT_GROUPS
        + [stat_spec] * (2 * N_ATT_GROUPS)
        + [tile(POOL_WIDTH), tile(gates.shape[1])]
        + [whole(w) for w in weights],
        out_specs=tile(d_model),
        out_shape=jax.ShapeDtypeStruct((tokens, d_model), F32),
        scratch_shapes=[pltpu.VMEM((rows, ATT_OUT_WIDTH), BF16)],
        compiler_params=pltpu.CompilerParams(
            dimension_semantics=("arbitrary",), vmem_limit_bytes=V7X_VMEM_LIMIT_BYTES),
    )(x2, *outs, *maxes, *sums, pool_feat, gates, *weights)


def _rope_tables(seq_len):
    inv_freq = ROPE_THETA ** (-jnp.arange(0, ROT_DIM, 2, dtype=F32) / ROT_DIM)
    ang = jnp.arange(seq_len).astype(F32)[:, None] * inv_freq[None, :]
    cos, sin = jnp.cos(ang), jnp.sin(ang)
    zeros = jnp.zeros((seq_len, HEAD_DIM - ROT_DIM), F32)
    half0 = jnp.zeros((seq_len, ROT_HALF), F32)
    c_head = jnp.concatenate([cos, cos, zeros + 1.0], axis=1)
    lo_head = jnp.concatenate([-sin, half0, zeros], axis=1)
    hi_head = jnp.concatenate([half0, sin, zeros], axis=1)
    tab = jnp.stack([c_head, lo_head, hi_head])
    return jnp.concatenate([tab] * (LANES // HEAD_DIM), axis=2)


def _spread_matrix():
    row = jnp.arange(LANES)[:, None]
    col = jnp.arange(N_ATT_GROUPS * ATT_OUT_WIDTH)[None, :]
    group = col // ATT_OUT_WIDTH
    head = (col % ATT_OUT_WIDTH) // HEAD_DIM
    stat = head // 2 + jnp.where(head % 2 == 0, HEAD_DIM, 0)
    hit = jnp.zeros((LANES, N_ATT_GROUPS * ATT_OUT_WIDTH), jnp.bool_)
    for piece in range(SPLIT_PIECES):
        hit = hit | (row == stat + HEAD_PAIRS * (SPLIT_PIECES * group + piece))
    return hit.astype(BF16)


def _band_bias():
    qi = jnp.arange(Q_BLOCK)[:, None]
    kb = jnp.arange(2 * Q_BLOCK)[None, :]
    dist = qi + Q_BLOCK - kb
    valid = (dist >= 0) & (dist <= WINDOW_KEYS)
    both = jnp.stack([valid, valid & (kb >= Q_BLOCK)])
    return jnp.where(both, 0.0, -jnp.inf).astype(F32)


def kernel(x, norm_mix, w_in, w_pool_group, pool_scale, w_att_branch, w_pool_branch, w_out,
           norm_ffn, w_ffn_gate, w_ffn_up, w_ffn_down, norm_final):
    batch, seq_len, d_model = x.shape
    depth = w_in.shape[0]
    tokens = batch * seq_len
    assert depth == 1, "stacked layers need the un-normalised residual between layers"
    bias = _band_bias()

    h = x.reshape(tokens, d_model)
    qkv0, qkv1, qkv2, pool_feat, gates = _in_proj(
        h, norm_mix[0][None, :], w_in[0].astype(BF16), w_pool_group[0].astype(BF16),
        pool_scale[0][None, :], _rope_tables(seq_len), batch, seq_len)
    qkv_groups = (qkv0.reshape(batch, 1, seq_len, GROUP_QKV_WIDTH), qkv1, qkv2)
    att = [_attention_group(qkv_groups[g], bias, batch, seq_len, g) for g in range(N_ATT_GROUPS)]
    out = _merge_ffn(
        h, [a[0] for a in att], [a[1] for a in att], [a[2] for a in att], pool_feat, gates,
        w_att_branch[0].astype(BF16), w_pool_branch[0].astype(BF16), w_out[0].astype(BF16),
        norm_ffn[0][None, :], w_ffn_gate[0].astype(BF16), w_ffn_up[0].astype(BF16),
        w_ffn_down[0].astype(BF16), norm_final[None, :], seq_len)
    return out.reshape(batch, seq_len, d_model)
```

```python
import functools
import math

import jax
import jax.numpy as jnp
from jax import lax
from jax.experimental import pallas as pl
from jax.experimental.pallas import tpu as pltpu

F32 = jnp.float32
BF16 = jnp.bfloat16

HEAD_DIM = 64
HEADS_PER_GROUP = 8
ATT_GROUPS = ((128, 1), (512, 4), (2048, 16))
N_ATT_GROUPS = len(ATT_GROUPS)
ATT_OUT_WIDTH = HEADS_PER_GROUP * HEAD_DIM
ATT_WIDTH = N_ATT_GROUPS * ATT_OUT_WIDTH
QKV_WIDTH = 3 * ATT_WIDTH
GROUP_QKV_WIDTH = 3 * ATT_OUT_WIDTH
ROT_DIM = HEAD_DIM // 4
ROT_HALF = ROT_DIM // 2
ROPE_THETA = 500000.0
POOL_WINDOWS = (2, 4, 8, 16)
POOL_GROUP_WIDTH = 128
POOL_WIDTH = len(POOL_WINDOWS) * POOL_GROUP_WIDTH
NORM_EPS = 1e-6
WINDOW_KEYS = 128
MASKED = -3.0e38
Q_SCALE = HEAD_DIM ** -0.5 * math.log2(math.e)

LANES = 128
V7X_VMEM_LIMIT_BYTES = 56 * 1024 * 1024

PROJ_ROWS = 512
PROJ_SUB_ROWS = 256
PROJ_COLS = 512
POOL_HALO = 16
ATT_ROWS = 1024
Q_BLOCK = 128
HEAD_PAIRS = HEADS_PER_GROUP // 2
FFN_ROWS = 512
FFN_SUB_ROWS = 256
FFN_CHUNKS = ((0, 1024), (1024, 2048), (2048, 2816))
SPLIT_PIECES = 3


def _rms_norm(xf, gain):
    ms = jnp.mean(xf * xf, axis=-1, keepdims=True)
    return (xf * lax.rsqrt(ms + NORM_EPS)) * gain


def _in_proj_kernel(x_ref, gain_ref, w_ref, wpg_ref, pscale_ref, rope_ref,
                    qkv0_ref, qkv1_ref, qkv2_ref, pool_ref, gate_ref,
                    u_ref, u4_ref, u16_ref, uslab_ref, u4slab_ref, tab4_ref, tab16_ref,
                    z0_ref, z1_ref, z2_ref, z3_ref, *, tiles_per_seq):
    rows, d_model = x_ref.shape
    seq_tile = pl.program_id(0) % tiles_per_seq
    n_slabs = d_model // LANES
    sub_rows = PROJ_SUB_ROWS
    n_sub = rows // sub_rows
    run4 = sub_rows // 4
    run16 = sub_rows // 16
    levels = (z0_ref, z1_ref, z2_ref, z3_ref)

    @pl.when(seq_tile == 0)
    def _():
        for ref in levels:
            ref[0:POOL_HALO, :] = jnp.zeros((POOL_HALO, POOL_WIDTH), F32)

    for sub in range(n_sub):
        row0 = sub * sub_rows
        uf = _rms_norm(x_ref[row0:row0 + sub_rows, :], gain_ref[...])
        u_ref[sub] = uf.astype(BF16)
        for s in range(n_slabs):
            uslab_ref[s] = uf[:, s * LANES:(s + 1) * LANES]
        for r in range(4):
            dst = slice(r * run4, (r + 1) * run4)
            for s in range(n_slabs):
                piece = uslab_ref[s, pl.ds(r, run4, stride=4), :]
                u4slab_ref[s, dst, :] = piece
                u4_ref[sub, dst, s * LANES:(s + 1) * LANES] = piece.astype(BF16)
            for t in range(3):
                tab4_ref[sub, t, dst, :] = rope_ref[t, pl.ds(row0 + r, run4, stride=4), :]
        for run in range(16):
            src0 = (run // 4) * run4 + run % 4
            dst = slice(run * run16, (run + 1) * run16)
            for s in range(n_slabs):
                piece = u4slab_ref[s, pl.ds(src0, run16, stride=4), :]
                u16_ref[sub, dst, s * LANES:(s + 1) * LANES] = piece.astype(BF16)
            for t in range(3):
                tab16_ref[sub, t, dst, :] = tab4_ref[sub, t, pl.ds(src0, run16, stride=4), :]

    def project(lhs_ref, sub, col0):
        return jnp.dot(lhs_ref[sub], w_ref[:, col0:col0 + PROJ_COLS],
                       preferred_element_type=F32)

    for sub in range(n_sub):
        row0 = POOL_HALO + sub * sub_rows
        z0_ref[row0:row0 + sub_rows, :] = project(u_ref, sub, QKV_WIDTH)

    cur = slice(POOL_HALO, POOL_HALO + rows)
    sums = []
    prev = z0_ref[cur, :]
    for lvl, ref in enumerate(levels):
        shift = 1 << lvl
        lo = lvl * POOL_GROUP_WIDTH
        shifted = ref[POOL_HALO - shift:POOL_HALO - shift + rows, lo:]
        prev = prev[:, (POOL_GROUP_WIDTH if lvl else 0):] + shifted
        sums.append(prev[:, 0:POOL_GROUP_WIDTH])
        if lvl + 1 < len(levels):
            levels[lvl + 1][cur, lo:] = prev
    pos = seq_tile * rows + lax.broadcasted_iota(jnp.int32, (rows, 1), 0)
    pooled = []
    for g, w in enumerate(POOL_WINDOWS):
        cols = slice(g * POOL_GROUP_WIDTH, (g + 1) * POOL_GROUP_WIDTH)
        count = jnp.minimum(pos + 1, w).astype(F32)
        pooled.append((sums[g] / count - z0_ref[cur, cols]).astype(BF16))
    for lvl, ref in enumerate(levels):
        lo = lvl * POOL_GROUP_WIDTH
        ref[0:POOL_HALO, lo:] = ref[rows:rows + POOL_HALO, lo:]

    def rope(xs, table):
        up = pltpu.roll(xs, LANES - ROT_HALF, axis=1)
        down = pltpu.roll(xs, ROT_HALF, axis=1)
        return xs * table(0) + up * table(1) + down * table(2)

    def store_natural(sub, col, part):
        qkv0_ref[sub * sub_rows:(sub + 1) * sub_rows, col:col + LANES] = part

    def store_by4(sub, col, part):
        for r in range(4):
            qkv1_ref[r, sub * run4:(sub + 1) * run4, col:col + LANES] = (
                part[r * run4:(r + 1) * run4])

    def store_by16(sub, col, part):
        for run in range(16):
            stream = 4 * (run % 4) + run // 4
            qkv2_ref[stream, sub * run16:(sub + 1) * run16, col:col + LANES] = (
                part[run * run16:(run + 1) * run16])

    for sub in range(n_sub):
        natural = slice(sub * sub_rows, (sub + 1) * sub_rows)
        variants = (
            (u_ref, lambda t: rope_ref[t, natural, :], store_natural),
            (u4_ref, lambda t: tab4_ref[sub, t], store_by4),
            (u16_ref, lambda t: tab16_ref[sub, t], store_by16),
        )
        for group, (lhs_ref, table, store) in enumerate(variants):
            for which in range(3):
                acc = project(lhs_ref, sub, which * ATT_WIDTH + group * ATT_OUT_WIDTH)
                for j in range(PROJ_COLS // LANES):
                    part = acc[:, j * LANES:(j + 1) * LANES]
                    if which == 0:
                        part = rope(part, table) * Q_SCALE
                    elif which == 1:
                        part = rope(part, table)
                    store(sub, which * ATT_OUT_WIDTH + j * LANES, part.astype(BF16))

    gate0 = QKV_WIDTH + POOL_WIDTH
    for sub in range(n_sub):
        natural = slice(sub * sub_rows, (sub + 1) * sub_rows)
        for c in range(gate_ref.shape[1] // PROJ_COLS):
            gate_ref[natural, c * PROJ_COLS:(c + 1) * PROJ_COLS] = (
                project(u_ref, sub, gate0 + c * PROJ_COLS))

    for g in range(len(POOL_WINDOWS)):
        cols = slice(g * POOL_GROUP_WIDTH, (g + 1) * POOL_GROUP_WIDTH)
        mapped = jnp.dot(pooled[g], wpg_ref[g], preferred_element_type=F32)
        pool_ref[:, cols] = (mapped * pscale_ref[:, cols]).astype(BF16)


def _in_proj(x2, gain, w_in, w_pool_group, pool_scale, rope_tab, batch, seq_len):
    tokens, d_model = x2.shape
    in_width = w_in.shape[1]
    gate_width = in_width - QKV_WIDTH - POOL_WIDTH
    rows = PROJ_ROWS
    n_sub = rows // PROJ_SUB_ROWS
    tiles_per_seq = seq_len // rows
    const = dict(pipeline_mode=pl.Buffered(1))

    def stream_spec(dilation):
        return pl.BlockSpec((None, dilation, rows // dilation, GROUP_QKV_WIDTH),
                            lambda i: (i // tiles_per_seq, 0, i % tiles_per_seq, 0))

    return pl.pallas_call(
        functools.partial(_in_proj_kernel, tiles_per_seq=tiles_per_seq),
        name="in_proj",
        grid=(tokens // rows,),
        in_specs=[
            pl.BlockSpec((rows, d_model), lambda i: (i, 0)),
            pl.BlockSpec((1, d_model), lambda i: (0, 0), **const),
            pl.BlockSpec((d_model, in_width), lambda i: (0, 0), **const),
            pl.BlockSpec(w_pool_group.shape, lambda i: (0, 0, 0), **const),
            pl.BlockSpec((1, POOL_WIDTH), lambda i: (0, 0), **const),
            pl.BlockSpec((3, rows, LANES), lambda i: (0, i % tiles_per_seq, 0)),
        ],
        out_specs=[
            pl.BlockSpec((rows, GROUP_QKV_WIDTH), lambda i: (i, 0)),
            stream_spec(4),
            stream_spec(16),
            pl.BlockSpec((rows, POOL_WIDTH), lambda i: (i, 0)),
            pl.BlockSpec((rows, gate_width), lambda i: (i, 0)),
        ],
        out_shape=[
            jax.ShapeDtypeStruct((tokens, GROUP_QKV_WIDTH), BF16),
            jax.ShapeDtypeStruct((batch, 4, seq_len // 4, GROUP_QKV_WIDTH), BF16),
            jax.ShapeDtypeStruct((batch, 16, seq_len // 16, GROUP_QKV_WIDTH), BF16),
            jax.ShapeDtypeStruct((tokens, POOL_WIDTH), BF16),
            jax.ShapeDtypeStruct((tokens, gate_width), F32),
        ],
        scratch_shapes=[pltpu.VMEM((n_sub, PROJ_SUB_ROWS, d_model), BF16)] * 3
        + [pltpu.VMEM((d_model // LANES, PROJ_SUB_ROWS, LANES), F32)] * 2
        + [pltpu.VMEM((n_sub, 3, PROJ_SUB_ROWS, LANES), F32)] * 2
        + [pltpu.VMEM((POOL_HALO + rows, POOL_WIDTH), F32) for _ in POOL_WINDOWS],
        compiler_params=pltpu.CompilerParams(
            dimension_semantics=("arbitrary",), vmem_limit_bytes=V7X_VMEM_LIMIT_BYTES),
    )(x2, gain, w_in, w_pool_group, pool_scale, rope_tab)


def _stat_lane(head):
    return head // 2 + (HEAD_DIM if head % 2 == 0 else 0)


def _piece_shift(group, piece):
    return HEAD_PAIRS * (SPLIT_PIECES * group + piece)


def _attention_kernel(q_ref, k_ref, v_ref, bias_ref, eye_ref, o_ref, m_ref, l_ref,
                      kcarry_ref, vaug_ref,
                      *, dilation, single_step):
    n_streams, rows, _ = q_ref.shape
    step = pl.program_id(2)
    first = step == 0

    lane = lax.broadcasted_iota(jnp.int32, (Q_BLOCK, LANES), 1)
    low_half = lane < HEAD_DIM
    head_mask = (jnp.where(low_half, 1.0, 0.0).astype(BF16),
                 jnp.where(low_half, 0.0, 1.0).astype(BF16))
    low_rows = lax.broadcasted_iota(jnp.int32, (rows, LANES), 1) < HEAD_DIM
    one = jnp.ones((rows, LANES), BF16)
    nt_dims = (((1,), (1,)), ((), ()))
    zero_block = jnp.zeros((Q_BLOCK, ATT_OUT_WIDTH), BF16)

    def clear_carry():
        for st in range(n_streams):
            kcarry_ref[st, 0:Q_BLOCK, :] = zero_block
            vaug_ref[0, st, 0:Q_BLOCK, :] = zero_block
            vaug_ref[1, st, 0:Q_BLOCK, :] = zero_block

    if single_step:
        clear_carry()
    else:
        pl.when(first)(clear_carry)

    for st in range(n_streams):
        stream = pl.program_id(1) * n_streams + st
        kcarry_ref[st, Q_BLOCK:2 * Q_BLOCK, :] = k_ref[st, 0:Q_BLOCK, :]
        for pair in range(HEAD_PAIRS):
            cols = slice(pair * LANES, (pair + 1) * LANES)
            v_pair = v_ref[st, :, cols]
            vaug_ref[0, st, Q_BLOCK:Q_BLOCK + rows, cols] = jnp.where(low_rows, v_pair, one)
            vaug_ref[1, st, Q_BLOCK:Q_BLOCK + rows, cols] = jnp.where(low_rows, one, v_pair)

        for sub in range(rows // Q_BLOCK):
            band = slice(sub * Q_BLOCK, (sub + 2) * Q_BLOCK)
            q_rows = slice(sub * Q_BLOCK, (sub + 1) * Q_BLOCK)
            if sub == 0:
                bias_idx = 1 if single_step else jnp.where(first, 1, 0)
            else:
                bias_idx = 0
            first_row = step * rows + sub * Q_BLOCK
            if dilation == 1:
                tokens = pl.ds(pl.multiple_of(first_row, Q_BLOCK), Q_BLOCK)
            else:
                tokens = pl.ds(first_row * dilation + stream, Q_BLOCK, stride=dilation)
            m_tile = jnp.zeros((Q_BLOCK, LANES), F32)
            l_tile = jnp.ones((Q_BLOCK, LANES), F32)
            for pair in range(HEAD_PAIRS):
                cols = slice(pair * LANES, (pair + 1) * LANES)
                q_pair = q_ref[st, q_rows, cols]
                if sub == 0:
                    k_pair = kcarry_ref[st, :, cols]
                else:
                    k_pair = k_ref[st, (sub - 1) * Q_BLOCK:(sub + 1) * Q_BLOCK, cols]
                q_both = jnp.concatenate([q_pair * head_mask[0], q_pair * head_mask[1]], axis=0)
                q_both = jnp.concatenate([q_both, eye_ref[...]], axis=1)
                k_aug = jnp.concatenate([k_pair, bias_ref[bias_idx]], axis=1)
                s_both = lax.dot_general(q_both, k_aug, nt_dims, preferred_element_type=F32)
                pvs = []
                for half in range(2):
                    s = s_both[half * Q_BLOCK:(half + 1) * Q_BLOCK]
                    m = jnp.max(s, axis=-1, keepdims=True)
                    p = jnp.exp2(s - m).astype(BF16)
                    pv = jnp.dot(p, vaug_ref[half, st, band, cols], preferred_element_type=F32)
                    pvs.append(pv)
                    at_stat = lane == _stat_lane(2 * pair + half)
                    m_tile = jnp.where(at_stat, m, m_tile)
                    l_tile = jnp.where(at_stat, pv, l_tile)
                o_ref[pair, tokens, :] = jnp.where(low_half, pvs[0], pvs[1])
            m_ref[tokens, :] = m_tile
            l_ref[tokens, :] = l_tile

        kcarry_ref[st, 0:Q_BLOCK, :] = k_ref[st, rows - Q_BLOCK:rows, :]
        for half in range(2):
            vaug_ref[half, st, 0:Q_BLOCK, :] = vaug_ref[half, st, rows:rows + Q_BLOCK, :]


def _attention_group(qkv, bias, eye, batch, seq_len, group):
    _, dilation = ATT_GROUPS[group]
    stream_len = seq_len // dilation
    rows = min(ATT_ROWS, stream_len)
    n_streams = ATT_ROWS // rows
    single_step = rows == stream_len

    def qkv_spec(which):
        return pl.BlockSpec((None, n_streams, rows, ATT_OUT_WIDTH),
                            lambda b, r, m: (b, r, m, which))

    stat_spec = pl.BlockSpec((None, seq_len, LANES), lambda b, r, m: (b, 0, 0))
    stat_shape = jax.ShapeDtypeStruct((batch, seq_len, LANES), F32)
    return pl.pallas_call(
        functools.partial(_attention_kernel, dilation=dilation, single_step=single_step),
        name=f"attention_g{group}",
        grid=(batch, dilation // n_streams, stream_len // rows),
        in_specs=[
            qkv_spec(0), qkv_spec(1), qkv_spec(2),
            pl.BlockSpec(bias.shape, lambda b, r, m: (0, 0, 0), pipeline_mode=pl.Buffered(1)),
            pl.BlockSpec(eye.shape, lambda b, r, m: (0, 0), pipeline_mode=pl.Buffered(1)),
        ],
        out_specs=[
            pl.BlockSpec((None, HEAD_PAIRS, seq_len, LANES), lambda b, r, m: (b, 0, 0, 0)),
            stat_spec, stat_spec,
        ],
        out_shape=[
            jax.ShapeDtypeStruct((batch, HEAD_PAIRS, seq_len, LANES), F32),
            stat_shape, stat_shape,
        ],
        scratch_shapes=[
            pltpu.VMEM((n_streams, 2 * Q_BLOCK, ATT_OUT_WIDTH), BF16),
            pltpu.VMEM((2, n_streams, Q_BLOCK + rows, ATT_OUT_WIDTH), BF16),
        ],
        compiler_params=pltpu.CompilerParams(
            dimension_semantics=("arbitrary", "arbitrary", "arbitrary"),
            vmem_limit_bytes=V7X_VMEM_LIMIT_BYTES),
    )(qkv, qkv, qkv, bias, eye)


def _merge_ffn_kernel(x_ref, o0_ref, o1_ref, o2_ref, m0_ref, m1_ref, m2_ref, l0_ref, l1_ref, l2_ref,
                      pool_ref, gate_ref, spread_ref,
                      wab_ref, wpb_ref, wout_ref, gffn_ref, wg_ref, wu_ref, wd_ref, gfin_ref,
                      out_ref, att_ref):
    d_model = x_ref.shape[1]
    o_refs = (o0_ref, o1_ref, o2_ref)
    m_refs = (m0_ref, m1_ref, m2_ref)
    l_refs = (l0_ref, l1_ref, l2_ref)
    lane = lax.broadcasted_iota(jnp.int32, (FFN_SUB_ROWS, LANES), 1)
    stat_lanes = (lane % HEAD_DIM) < HEAD_PAIRS

    for sub in range(x_ref.shape[0] // FFN_SUB_ROWS):
        rows = slice(sub * FFN_SUB_ROWS, (sub + 1) * FFN_SUB_ROWS)

        ms = [ref[rows, :] for ref in m_refs]
        ls = [ref[rows, :] for ref in l_refs]
        top = jnp.maximum(jnp.maximum(ms[0], ms[1]), ms[2])
        es = [jnp.exp2(m - top) for m in ms]
        den = es[0] * ls[0] + es[1] * ls[1] + es[2] * ls[2]

        packed = jnp.zeros((FFN_SUB_ROWS, LANES), F32)
        for g in range(N_ATT_GROUPS):
            rest = jnp.where(stat_lanes, es[g] / den, 0.0)
            for piece in range(SPLIT_PIECES):
                part = rest.astype(BF16).astype(F32)
                rest = rest - part
                shift = _piece_shift(g, piece)
                packed = packed + (pltpu.roll(part, shift, axis=1) if shift else part)
        spread = jnp.dot(packed.astype(BF16), spread_ref[...], preferred_element_type=F32)

        for pair in range(HEAD_PAIRS):
            acc = jnp.zeros((FFN_SUB_ROWS, LANES), F32)
            for g in range(N_ATT_GROUPS):
                col = g * ATT_OUT_WIDTH + pair * LANES
                acc = acc + spread[:, col:col + LANES] * o_refs[g][pair, rows, :]
            att_ref[rows, pair * LANES:(pair + 1) * LANES] = acc.astype(BF16)

        y_att = jnp.dot(att_ref[rows, :], wab_ref[...], preferred_element_type=F32)
        y_pool = jnp.dot(pool_ref[rows, :], wpb_ref[...], preferred_element_type=F32)
        merged = (jax.nn.sigmoid(gate_ref[rows, 0:d_model]) * y_att
                  + jax.nn.sigmoid(gate_ref[rows, d_model:2 * d_model]) * y_pool)
        h = x_ref[rows, :] + jnp.dot(merged.astype(BF16), wout_ref[...],
                                     preferred_element_type=F32)

        f = _rms_norm(h, gffn_ref[...]).astype(BF16)
        ffn = jnp.zeros((FFN_SUB_ROWS, d_model), F32)
        for lo, hi in FFN_CHUNKS:
            gate = jnp.dot(f, wg_ref[:, lo:hi], preferred_element_type=F32)
            up = jnp.dot(f, wu_ref[:, lo:hi], preferred_element_type=F32)
            hidden = (jax.nn.silu(gate) * up).astype(BF16)
            ffn = ffn + jnp.dot(hidden, wd_ref[lo:hi, :], preferred_element_type=F32)
        out_ref[rows, :] = _rms_norm(h + ffn, gfin_ref[...])


def _merge_ffn(x2, outs, maxes, sums, pool_feat, gates, w_att_branch, w_pool_branch, w_out, norm_ffn,
               w_gate, w_up, w_down, norm_final, seq_len):
    tokens, d_model = x2.shape
    rows = FFN_ROWS
    tiles_per_seq = seq_len // rows

    def tile(width):
        return pl.BlockSpec((rows, width), lambda i: (i, 0))

    def whole(arr):
        return pl.BlockSpec(arr.shape, lambda i: (0, 0), pipeline_mode=pl.Buffered(1))

    o_spec = pl.BlockSpec((None, HEAD_PAIRS, rows, LANES),
                          lambda i: (i // tiles_per_seq, 0, i % tiles_per_seq, 0))
    stat_spec = pl.BlockSpec((None, rows, LANES),
                             lambda i: (i // tiles_per_seq, i % tiles_per_seq, 0))
    weights = (_spread_matrix(), w_att_branch, w_pool_branch, w_out, norm_ffn, w_gate, w_up,
               w_down, norm_final)
    return pl.pallas_call(
        _merge_ffn_kernel,
        name="merge_ffn",
        grid=(tokens // rows,),
        in_specs=[tile(d_model)]
        + [o_spec] * N_ATT_GROUPS
        + [stat_spec] * (2 * N_ATT_GROUPS)
        + [tile(POOL_WIDTH), tile(gates.shape[1])]
        + [whole(w) for w in weights],
        out_specs=tile(d_model),
        out_shape=jax.ShapeDtypeStruct((tokens, d_model), F32),
        scratch_shapes=[pltpu.VMEM((rows, ATT_OUT_WIDTH), BF16)],
        compiler_params=pltpu.CompilerParams(
            dimension_semantics=("arbitrary",), vmem_limit_bytes=V7X_VMEM_LIMIT_BYTES),
    )(x2, *outs, *maxes, *sums, pool_feat, gates, *weights)


def _rope_tables(seq_len):
    inv_freq = ROPE_THETA ** (-jnp.arange(0, ROT_DIM, 2, dtype=F32) / ROT_DIM)
    ang = jnp.arange(seq_len).astype(F32)[:, None] * inv_freq[None, :]
    cos, sin = jnp.cos(ang), jnp.sin(ang)
    zeros = jnp.zeros((seq_len, HEAD_DIM - ROT_DIM), F32)
    half0 = jnp.zeros((seq_len, ROT_HALF), F32)
    c_head = jnp.concatenate([cos, cos, zeros + 1.0], axis=1)
    lo_head = jnp.concatenate([-sin, half0, zeros], axis=1)
    hi_head = jnp.concatenate([half0, sin, zeros], axis=1)
    tab = jnp.stack([c_head, lo_head, hi_head])
    return jnp.concatenate([tab] * (LANES // HEAD_DIM), axis=2)


def _spread_matrix():
    row = jnp.arange(LANES)[:, None]
    col = jnp.arange(N_ATT_GROUPS * ATT_OUT_WIDTH)[None, :]
    group = col // ATT_OUT_WIDTH
    head = (col % ATT_OUT_WIDTH) // HEAD_DIM
    stat = head // 2 + jnp.where(head % 2 == 0, HEAD_DIM, 0)
    hit = jnp.zeros((LANES, N_ATT_GROUPS * ATT_OUT_WIDTH), jnp.bool_)
    for piece in range(SPLIT_PIECES):
        hit = hit | (row == stat + HEAD_PAIRS * (SPLIT_PIECES * group + piece))
    return hit.astype(BF16)


def _band_bias():
    kb = jnp.arange(2 * Q_BLOCK)[:, None]
    qi = jnp.arange(Q_BLOCK)[None, :]
    dist = qi + Q_BLOCK - kb
    valid = (dist >= 0) & (dist <= WINDOW_KEYS)
    both = jnp.stack([valid, valid & (kb >= Q_BLOCK)])
    return jnp.where(both, 0.0, MASKED).astype(BF16)


def _stacked_identity():
    row = jnp.arange(2 * Q_BLOCK)[:, None]
    col = jnp.arange(LANES)[None, :]
    return (row % Q_BLOCK == col).astype(BF16)


def kernel(x, norm_mix, w_in, w_pool_group, pool_scale, w_att_branch, w_pool_branch, w_out,
           norm_ffn, w_ffn_gate, w_ffn_up, w_ffn_down, norm_final):
    batch, seq_len, d_model = x.shape
    depth = w_in.shape[0]
    tokens = batch * seq_len
    assert depth == 1, "stacked layers need the un-normalised residual between layers"
    bias = _band_bias()

    h = x.reshape(tokens, d_model)
    qkv0, qkv1, qkv2, pool_feat, gates = _in_proj(
        h, norm_mix[0][None, :], w_in[0].astype(BF16), w_pool_group[0].astype(BF16),
        pool_scale[0][None, :], _rope_tables(seq_len), batch, seq_len)
    qkv_groups = (qkv0.reshape(batch, 1, seq_len, GROUP_QKV_WIDTH), qkv1, qkv2)
    eye = _stacked_identity()
    att = [_attention_group(qkv_groups[g], bias, eye, batch, seq_len, g)
           for g in range(N_ATT_GROUPS)]
    out = _merge_ffn(
        h, [a[0] for a in att], [a[1] for a in att], [a[2] for a in att], pool_feat, gates,
        w_att_branch[0].astype(BF16), w_pool_branch[0].astype(BF16), w_out[0].astype(BF16),
        norm_ffn[0][None, :], w_ffn_gate[0].astype(BF16), w_ffn_up[0].astype(BF16),
        w_ffn_down[0].astype(BF16), norm_final[None, :], seq_len)
    return out.reshape(batch, seq_len, d_model)
```

```python
import functools
import math

import jax
import jax.numpy as jnp
import numpy as np
from jax import lax
from jax.experimental import pallas as pl
from jax.experimental.pallas import tpu as pltpu

F32 = jnp.float32
BF16 = jnp.bfloat16

HEAD_DIM = 64
HEADS_PER_GROUP = 8
ATT_GROUPS = ((128, 1), (512, 4), (2048, 16))
N_ATT_GROUPS = len(ATT_GROUPS)
ATT_OUT_WIDTH = HEADS_PER_GROUP * HEAD_DIM
ATT_WIDTH = N_ATT_GROUPS * ATT_OUT_WIDTH
QKV_WIDTH = 3 * ATT_WIDTH
GROUP_QKV_WIDTH = 3 * ATT_OUT_WIDTH
ROT_DIM = HEAD_DIM // 4
ROT_HALF = ROT_DIM // 2
ROPE_THETA = 500000.0
POOL_WINDOWS = (2, 4, 8, 16)
POOL_GROUP_WIDTH = 128
POOL_WIDTH = len(POOL_WINDOWS) * POOL_GROUP_WIDTH
NORM_EPS = 1e-6
WINDOW_KEYS = 128
MASKED = -3.0e38
Q_SCALE = HEAD_DIM ** -0.5 * math.log2(math.e)

LANES = 128
V7X_VMEM_LIMIT_BYTES = 56 * 1024 * 1024

PROJ_ROWS = 512
PROJ_SUB_ROWS = 256
PROJ_COLS = 512
POOL_HALO = 16
ATT_ROWS = 1024
Q_BLOCK = 128
HEAD_PAIRS = HEADS_PER_GROUP // 2
FFN_ROWS = 512
FFN_SUB_ROWS = 256
FFN_CHUNKS = ((0, 1024), (1024, 2048), (2048, 2816))
SPLIT_PIECES = 3


def _rms_norm(xf, gain):
    ms = jnp.mean(xf * xf, axis=-1, keepdims=True)
    return (xf * lax.rsqrt(ms + NORM_EPS)) * gain


def _in_proj_kernel(x_ref, gain_ref, w_ref, wpg_ref, pscale_ref, rope_ref,
                    qkv0_ref, qkv1_ref, qkv2_ref, pool_ref, gate_ref,
                    u_ref, u4_ref, u16_ref, uslab_ref, u4slab_ref, tab4_ref, tab16_ref,
                    z0_ref, z1_ref, z2_ref, z3_ref, *, tiles_per_seq):
    rows, d_model = x_ref.shape
    seq_tile = pl.program_id(0) % tiles_per_seq
    n_slabs = d_model // LANES
    sub_rows = PROJ_SUB_ROWS
    n_sub = rows // sub_rows
    run4 = sub_rows // 4
    run16 = sub_rows // 16
    levels = (z0_ref, z1_ref, z2_ref, z3_ref)

    @pl.when(seq_tile == 0)
    def _():
        for ref in levels:
            ref[0:POOL_HALO, :] = jnp.zeros((POOL_HALO, POOL_WIDTH), F32)

    for sub in range(n_sub):
        row0 = sub * sub_rows
        uf = _rms_norm(x_ref[row0:row0 + sub_rows, :], gain_ref[...])
        u_ref[sub] = uf.astype(BF16)
        for s in range(n_slabs):
            uslab_ref[s] = uf[:, s * LANES:(s + 1) * LANES]
        for r in range(4):
            dst = slice(r * run4, (r + 1) * run4)
            for s in range(n_slabs):
                piece = uslab_ref[s, pl.ds(r, run4, stride=4), :]
                u4slab_ref[s, dst, :] = piece
                u4_ref[sub, dst, s * LANES:(s + 1) * LANES] = piece.astype(BF16)
            for t in range(3):
                tab4_ref[sub, t, dst, :] = rope_ref[t, pl.ds(row0 + r, run4, stride=4), :]
        for run in range(16):
            src0 = (run // 4) * run4 + run % 4
            dst = slice(run * run16, (run + 1) * run16)
            for s in range(n_slabs):
                piece = u4slab_ref[s, pl.ds(src0, run16, stride=4), :]
                u16_ref[sub, dst, s * LANES:(s + 1) * LANES] = piece.astype(BF16)
            for t in range(3):
                tab16_ref[sub, t, dst, :] = tab4_ref[sub, t, pl.ds(src0, run16, stride=4), :]

    def project(lhs_ref, sub, col0):
        return jnp.dot(lhs_ref[sub], w_ref[:, col0:col0 + PROJ_COLS],
                       preferred_element_type=F32)

    for sub in range(n_sub):
        row0 = POOL_HALO + sub * sub_rows
        z0_ref[row0:row0 + sub_rows, :] = project(u_ref, sub, QKV_WIDTH)

    cur = slice(POOL_HALO, POOL_HALO + rows)
    sums = []
    prev = z0_ref[cur, :]
    for lvl, ref in enumerate(levels):
        shift = 1 << lvl
        lo = lvl * POOL_GROUP_WIDTH
        shifted = ref[POOL_HALO - shift:POOL_HALO - shift + rows, lo:]
        prev = prev[:, (POOL_GROUP_WIDTH if lvl else 0):] + shifted
        sums.append(prev[:, 0:POOL_GROUP_WIDTH])
        if lvl + 1 < len(levels):
            levels[lvl + 1][cur, lo:] = prev
    pos = seq_tile * rows + lax.broadcasted_iota(jnp.int32, (rows, 1), 0)
    pooled = []
    for g, w in enumerate(POOL_WINDOWS):
        cols = slice(g * POOL_GROUP_WIDTH, (g + 1) * POOL_GROUP_WIDTH)
        count = jnp.minimum(pos + 1, w).astype(F32)
        pooled.append((sums[g] / count - z0_ref[cur, cols]).astype(BF16))
    for lvl, ref in enumerate(levels):
        lo = lvl * POOL_GROUP_WIDTH
        ref[0:POOL_HALO, lo:] = ref[rows:rows + POOL_HALO, lo:]

    def rope(xs, table):
        up = pltpu.roll(xs, LANES - ROT_HALF, axis=1)
        down = pltpu.roll(xs, ROT_HALF, axis=1)
        return xs * table(0) + up * table(1) + down * table(2)

    def store_natural(sub, col, part):
        qkv0_ref[sub * sub_rows:(sub + 1) * sub_rows, col:col + LANES] = part

    def store_by4(sub, col, part):
        for r in range(4):
            qkv1_ref[r, sub * run4:(sub + 1) * run4, col:col + LANES] = (
                part[r * run4:(r + 1) * run4])

    def store_by16(sub, col, part):
        for run in range(16):
            stream = 4 * (run % 4) + run // 4
            qkv2_ref[stream, sub * run16:(sub + 1) * run16, col:col + LANES] = (
                part[run * run16:(run + 1) * run16])

    for sub in range(n_sub):
        natural = slice(sub * sub_rows, (sub + 1) * sub_rows)
        variants = (
            (u_ref, lambda t: rope_ref[t, natural, :], store_natural),
            (u4_ref, lambda t: tab4_ref[sub, t], store_by4),
            (u16_ref, lambda t: tab16_ref[sub, t], store_by16),
        )
        for group, (lhs_ref, table, store) in enumerate(variants):
            for which in range(3):
                acc = project(lhs_ref, sub, which * ATT_WIDTH + group * ATT_OUT_WIDTH)
                for j in range(PROJ_COLS // LANES):
                    part = acc[:, j * LANES:(j + 1) * LANES]
                    if which == 0:
                        part = rope(part, table) * Q_SCALE
                    elif which == 1:
                        part = rope(part, table)
                    store(sub, which * ATT_OUT_WIDTH + j * LANES, part.astype(BF16))

    gate0 = QKV_WIDTH + POOL_WIDTH
    for sub in range(n_sub):
        natural = slice(sub * sub_rows, (sub + 1) * sub_rows)
        for c in range(gate_ref.shape[1] // PROJ_COLS):
            gate_ref[natural, c * PROJ_COLS:(c + 1) * PROJ_COLS] = (
                project(u_ref, sub, gate0 + c * PROJ_COLS))

    for g in range(len(POOL_WINDOWS)):
        cols = slice(g * POOL_GROUP_WIDTH, (g + 1) * POOL_GROUP_WIDTH)
        mapped = jnp.dot(pooled[g], wpg_ref[g], preferred_element_type=F32)
        pool_ref[:, cols] = (mapped * pscale_ref[:, cols]).astype(BF16)


def _in_proj(x2, gain, w_in, w_pool_group, pool_scale, rope_tab, batch, seq_len):
    tokens, d_model = x2.shape
    in_width = w_in.shape[1]
    gate_width = in_width - QKV_WIDTH - POOL_WIDTH
    rows = PROJ_ROWS
    n_sub = rows // PROJ_SUB_ROWS
    tiles_per_seq = seq_len // rows
    const = dict(pipeline_mode=pl.Buffered(1))

    def stream_spec(dilation):
        return pl.BlockSpec((None, dilation, rows // dilation, GROUP_QKV_WIDTH),
                            lambda i: (i // tiles_per_seq, 0, i % tiles_per_seq, 0))

    return pl.pallas_call(
        functools.partial(_in_proj_kernel, tiles_per_seq=tiles_per_seq),
        name="in_proj",
        grid=(tokens // rows,),
        in_specs=[
            pl.BlockSpec((rows, d_model), lambda i: (i, 0)),
            pl.BlockSpec((1, d_model), lambda i: (0, 0), **const),
            pl.BlockSpec((d_model, in_width), lambda i: (0, 0), **const),
            pl.BlockSpec(w_pool_group.shape, lambda i: (0, 0, 0), **const),
            pl.BlockSpec((1, POOL_WIDTH), lambda i: (0, 0), **const),
            pl.BlockSpec((3, rows, LANES), lambda i: (0, i % tiles_per_seq, 0)),
        ],
        out_specs=[
            pl.BlockSpec((rows, GROUP_QKV_WIDTH), lambda i: (i, 0)),
            stream_spec(4),
            stream_spec(16),
            pl.BlockSpec((rows, POOL_WIDTH), lambda i: (i, 0)),
            pl.BlockSpec((rows, gate_width), lambda i: (i, 0)),
        ],
        out_shape=[
            jax.ShapeDtypeStruct((tokens, GROUP_QKV_WIDTH), BF16),
            jax.ShapeDtypeStruct((batch, 4, seq_len // 4, GROUP_QKV_WIDTH), BF16),
            jax.ShapeDtypeStruct((batch, 16, seq_len // 16, GROUP_QKV_WIDTH), BF16),
            jax.ShapeDtypeStruct((tokens, POOL_WIDTH), BF16),
            jax.ShapeDtypeStruct((tokens, gate_width), F32),
        ],
        scratch_shapes=[pltpu.VMEM((n_sub, PROJ_SUB_ROWS, d_model), BF16)] * 3
        + [pltpu.VMEM((d_model // LANES, PROJ_SUB_ROWS, LANES), F32)] * 2
        + [pltpu.VMEM((n_sub, 3, PROJ_SUB_ROWS, LANES), F32)] * 2
        + [pltpu.VMEM((POOL_HALO + rows, POOL_WIDTH), F32) for _ in POOL_WINDOWS],
        compiler_params=pltpu.CompilerParams(
            dimension_semantics=("arbitrary",), vmem_limit_bytes=V7X_VMEM_LIMIT_BYTES),
    )(x2, gain, w_in, w_pool_group, pool_scale, rope_tab)


def _stat_lane(head):
    return head // 2 + (HEAD_DIM if head % 2 == 0 else 0)


def _piece_shift(group, piece):
    return HEAD_PAIRS * (SPLIT_PIECES * group + piece)


def _attention_kernel(q_ref, k_ref, v_ref, bias_ref, eye_ref, o_ref, m_ref, l_ref,
                      kcarry_ref, vaug_ref,
                      *, dilation, single_step):
    n_streams, rows, _ = q_ref.shape
    step = pl.program_id(2)
    first = step == 0

    lane = lax.broadcasted_iota(jnp.int32, (Q_BLOCK, LANES), 1)
    low_half = lane < HEAD_DIM
    head_mask = (jnp.where(low_half, 1.0, 0.0).astype(BF16),
                 jnp.where(low_half, 0.0, 1.0).astype(BF16))
    low_rows = lax.broadcasted_iota(jnp.int32, (rows, LANES), 1) < HEAD_DIM
    one = jnp.ones((rows, LANES), BF16)
    nt_dims = (((1,), (1,)), ((), ()))
    zero_block = jnp.zeros((Q_BLOCK, ATT_OUT_WIDTH), BF16)

    def clear_carry():
        for st in range(n_streams):
            kcarry_ref[st, 0:Q_BLOCK, :] = zero_block
            vaug_ref[0, st, 0:Q_BLOCK, :] = zero_block
            vaug_ref[1, st, 0:Q_BLOCK, :] = zero_block

    if single_step:
        clear_carry()
    else:
        pl.when(first)(clear_carry)

    for st in range(n_streams):
        stream = pl.program_id(1) * n_streams + st
        kcarry_ref[st, Q_BLOCK:2 * Q_BLOCK, :] = k_ref[st, 0:Q_BLOCK, :]
        for pair in range(HEAD_PAIRS):
            cols = slice(pair * LANES, (pair + 1) * LANES)
            v_pair = v_ref[st, :, cols]
            vaug_ref[0, st, Q_BLOCK:Q_BLOCK + rows, cols] = jnp.where(low_rows, v_pair, one)
            vaug_ref[1, st, Q_BLOCK:Q_BLOCK + rows, cols] = jnp.where(low_rows, one, v_pair)

        for sub in range(rows // Q_BLOCK):
            band = slice(sub * Q_BLOCK, (sub + 2) * Q_BLOCK)
            q_rows = slice(sub * Q_BLOCK, (sub + 1) * Q_BLOCK)
            if sub == 0:
                bias_idx = 1 if single_step else jnp.where(first, 1, 0)
            else:
                bias_idx = 0
            if dilation == 1:
                tokens = pl.ds(sub * Q_BLOCK, Q_BLOCK)
            else:
                first_row = step * rows + sub * Q_BLOCK
                tokens = pl.ds(first_row * dilation + stream, Q_BLOCK, stride=dilation)
            m_tile = jnp.zeros((Q_BLOCK, LANES), F32)
            l_tile = jnp.ones((Q_BLOCK, LANES), F32)
            for pair in range(HEAD_PAIRS):
                cols = slice(pair * LANES, (pair + 1) * LANES)
                q_pair = q_ref[st, q_rows, cols]
                if sub == 0:
                    k_pair = kcarry_ref[st, :, cols]
                else:
                    k_pair = k_ref[st, (sub - 1) * Q_BLOCK:(sub + 1) * Q_BLOCK, cols]
                q_both = jnp.concatenate([q_pair * head_mask[0], q_pair * head_mask[1]], axis=0)
                q_both = jnp.concatenate([q_both, eye_ref[...]], axis=1)
                k_aug = jnp.concatenate([k_pair, bias_ref[bias_idx]], axis=1)
                s_both = lax.dot_general(q_both, k_aug, nt_dims, preferred_element_type=F32)
                pvs = []
                for half in range(2):
                    s = s_both[half * Q_BLOCK:(half + 1) * Q_BLOCK]
                    m = jnp.max(s, axis=-1, keepdims=True)
                    p = jnp.exp2(s - m).astype(BF16)
                    pv = jnp.dot(p, vaug_ref[half, st, band, cols], preferred_element_type=F32)
                    pvs.append(pv)
                    at_stat = lane == _stat_lane(2 * pair + half)
                    m_tile = jnp.where(at_stat, m, m_tile)
                    l_tile = jnp.where(at_stat, pv, l_tile)
                o_ref[pair, tokens, :] = jnp.where(low_half, pvs[0], pvs[1])
            m_ref[tokens, :] = m_tile
            l_ref[tokens, :] = l_tile

        kcarry_ref[st, 0:Q_BLOCK, :] = k_ref[st, rows - Q_BLOCK:rows, :]
        for half in range(2):
            vaug_ref[half, st, 0:Q_BLOCK, :] = vaug_ref[half, st, rows:rows + Q_BLOCK, :]


def _attention_group(qkv, bias, eye, batch, seq_len, group):
    _, dilation = ATT_GROUPS[group]
    stream_len = seq_len // dilation
    rows = min(ATT_ROWS, stream_len)
    n_streams = ATT_ROWS // rows
    single_step = rows == stream_len

    def qkv_spec(which):
        return pl.BlockSpec((None, n_streams, rows, ATT_OUT_WIDTH),
                            lambda b, r, m: (b, r, m, which))

    if dilation == 1:
        o_spec = pl.BlockSpec((None, HEAD_PAIRS, rows, LANES), lambda b, r, m: (b, 0, m, 0))
        stat_spec = pl.BlockSpec((None, rows, LANES), lambda b, r, m: (b, m, 0))
    else:
        o_spec = pl.BlockSpec((None, HEAD_PAIRS, seq_len, LANES), lambda b, r, m: (b, 0, 0, 0))
        stat_spec = pl.BlockSpec((None, seq_len, LANES), lambda b, r, m: (b, 0, 0))
    stat_shape = jax.ShapeDtypeStruct((batch, seq_len, LANES), F32)
    return pl.pallas_call(
        functools.partial(_attention_kernel, dilation=dilation, single_step=single_step),
        name=f"attention_g{group}",
        grid=(batch, dilation // n_streams, stream_len // rows),
        in_specs=[
            qkv_spec(0), qkv_spec(1), qkv_spec(2),
            pl.BlockSpec(bias.shape, lambda b, r, m: (0, 0, 0), pipeline_mode=pl.Buffered(1)),
            pl.BlockSpec(eye.shape, lambda b, r, m: (0, 0), pipeline_mode=pl.Buffered(1)),
        ],
        out_specs=[o_spec, stat_spec, stat_spec],
        out_shape=[
            jax.ShapeDtypeStruct((batch, HEAD_PAIRS, seq_len, LANES), F32),
            stat_shape, stat_shape,
        ],
        scratch_shapes=[
            pltpu.VMEM((n_streams, 2 * Q_BLOCK, ATT_OUT_WIDTH), BF16),
            pltpu.VMEM((2, n_streams, Q_BLOCK + rows, ATT_OUT_WIDTH), BF16),
        ],
        compiler_params=pltpu.CompilerParams(
            dimension_semantics=("arbitrary", "arbitrary", "arbitrary"),
            vmem_limit_bytes=V7X_VMEM_LIMIT_BYTES),
    )(qkv, qkv, qkv, bias, eye)


def _merge_ffn_kernel(x_ref, o0_ref, o1_ref, o2_ref, m0_ref, m1_ref, m2_ref, l0_ref, l1_ref, l2_ref,
                      pool_ref, gate_ref, spread_ref,
                      wab_ref, wpb_ref, wout_ref, gffn_ref, wg_ref, wu_ref, wd_ref, gfin_ref,
                      out_ref, att_ref):
    d_model = x_ref.shape[1]
    o_refs = (o0_ref, o1_ref, o2_ref)
    m_refs = (m0_ref, m1_ref, m2_ref)
    l_refs = (l0_ref, l1_ref, l2_ref)
    lane = lax.broadcasted_iota(jnp.int32, (FFN_SUB_ROWS, LANES), 1)
    stat_lanes = (lane % HEAD_DIM) < HEAD_PAIRS

    for sub in range(x_ref.shape[0] // FFN_SUB_ROWS):
        rows = slice(sub * FFN_SUB_ROWS, (sub + 1) * FFN_SUB_ROWS)

        ms = [ref[rows, :] for ref in m_refs]
        ls = [ref[rows, :] for ref in l_refs]
        top = jnp.maximum(jnp.maximum(ms[0], ms[1]), ms[2])
        es = [jnp.exp2(m - top) for m in ms]
        den = es[0] * ls[0] + es[1] * ls[1] + es[2] * ls[2]

        packed = jnp.zeros((FFN_SUB_ROWS, LANES), F32)
        for g in range(N_ATT_GROUPS):
            rest = jnp.where(stat_lanes, es[g] / den, 0.0)
            for piece in range(SPLIT_PIECES):
                part = rest.astype(BF16).astype(F32)
                rest = rest - part
                shift = _piece_shift(g, piece)
                packed = packed + (pltpu.roll(part, shift, axis=1) if shift else part)
        spread = jnp.dot(packed.astype(BF16), spread_ref[...], preferred_element_type=F32)

        for pair in range(HEAD_PAIRS):
            acc = jnp.zeros((FFN_SUB_ROWS, LANES), F32)
            for g in range(N_ATT_GROUPS):
                col = g * ATT_OUT_WIDTH + pair * LANES
                acc = acc + spread[:, col:col + LANES] * o_refs[g][pair, rows, :]
            att_ref[rows, pair * LANES:(pair + 1) * LANES] = acc.astype(BF16)

        y_att = jnp.dot(att_ref[rows, :], wab_ref[...], preferred_element_type=F32)
        y_pool = jnp.dot(pool_ref[rows, :], wpb_ref[...], preferred_element_type=F32)
        merged = (jax.nn.sigmoid(gate_ref[rows, 0:d_model]) * y_att
                  + jax.nn.sigmoid(gate_ref[rows, d_model:2 * d_model]) * y_pool)
        h = x_ref[rows, :] + jnp.dot(merged.astype(BF16), wout_ref[...],
                                     preferred_element_type=F32)

        f = _rms_norm(h, gffn_ref[...]).astype(BF16)
        ffn = jnp.zeros((FFN_SUB_ROWS, d_model), F32)
        for lo, hi in FFN_CHUNKS:
            gate = jnp.dot(f, wg_ref[:, lo:hi], preferred_element_type=F32)
            up = jnp.dot(f, wu_ref[:, lo:hi], preferred_element_type=F32)
            hidden = (jax.nn.silu(gate) * up).astype(BF16)
            ffn = ffn + jnp.dot(hidden, wd_ref[lo:hi, :], preferred_element_type=F32)
        out_ref[rows, :] = _rms_norm(h + ffn, gfin_ref[...])


def _merge_ffn(x2, outs, maxes, sums, pool_feat, gates, w_att_branch, w_pool_branch, w_out, norm_ffn,
               w_gate, w_up, w_down, norm_final, seq_len):
    tokens, d_model = x2.shape
    rows = FFN_ROWS
    tiles_per_seq = seq_len // rows

    def tile(width):
        return pl.BlockSpec((rows, width), lambda i: (i, 0))

    def whole(arr):
        return pl.BlockSpec(arr.shape, lambda i: (0, 0), pipeline_mode=pl.Buffered(1))

    o_spec = pl.BlockSpec((None, HEAD_PAIRS, rows, LANES),
                          lambda i: (i // tiles_per_seq, 0, i % tiles_per_seq, 0))
    stat_spec = pl.BlockSpec((None, rows, LANES),
                             lambda i: (i // tiles_per_seq, i % tiles_per_seq, 0))
    weights = (_spread_matrix(), w_att_branch, w_pool_branch, w_out, norm_ffn, w_gate, w_up,
               w_down, norm_final)
    return pl.pallas_call(
        _merge_ffn_kernel,
        name="merge_ffn",
        grid=(tokens // rows,),
        in_specs=[tile(d_model)]
        + [o_spec] * N_ATT_GROUPS
        + [stat_spec] * (2 * N_ATT_GROUPS)
        + [tile(POOL_WIDTH), tile(gates.shape[1])]
        + [whole(w) for w in weights],
        out_specs=tile(d_model),
        out_shape=jax.ShapeDtypeStruct((tokens, d_model), F32),
        scratch_shapes=[pltpu.VMEM((rows, ATT_OUT_WIDTH), BF16)],
        compiler_params=pltpu.CompilerParams(
            dimension_semantics=("arbitrary",), vmem_limit_bytes=V7X_VMEM_LIMIT_BYTES),
    )(x2, *outs, *maxes, *sums, pool_feat, gates, *weights)


def _rope_tables(seq_len):
    inv_freq = ROPE_THETA ** (-np.arange(0, ROT_DIM, 2, dtype=np.float64) / ROT_DIM)
    ang = np.arange(seq_len, dtype=np.float64)[:, None] * inv_freq[None, :]
    cos, sin = np.cos(ang), np.sin(ang)
    zeros = np.zeros((seq_len, HEAD_DIM - ROT_DIM))
    half0 = np.zeros((seq_len, ROT_HALF))
    c_head = np.concatenate([cos, cos, zeros + 1.0], axis=1)
    lo_head = np.concatenate([-sin, half0, zeros], axis=1)
    hi_head = np.concatenate([half0, sin, zeros], axis=1)
    tab = np.stack([c_head, lo_head, hi_head])
    return jnp.asarray(np.concatenate([tab] * (LANES // HEAD_DIM), axis=2), dtype=F32)


def _spread_matrix():
    row = jnp.arange(LANES)[:, None]
    col = jnp.arange(N_ATT_GROUPS * ATT_OUT_WIDTH)[None, :]
    group = col // ATT_OUT_WIDTH
    head = (col % ATT_OUT_WIDTH) // HEAD_DIM
    stat = head // 2 + jnp.where(head % 2 == 0, HEAD_DIM, 0)
    hit = jnp.zeros((LANES, N_ATT_GROUPS * ATT_OUT_WIDTH), jnp.bool_)
    for piece in range(SPLIT_PIECES):
        hit = hit | (row == stat + HEAD_PAIRS * (SPLIT_PIECES * group + piece))
    return hit.astype(BF16)


def _band_bias():
    kb = jnp.arange(2 * Q_BLOCK)[:, None]
    qi = jnp.arange(Q_BLOCK)[None, :]
    dist = qi + Q_BLOCK - kb
    valid = (dist >= 0) & (dist <= WINDOW_KEYS)
    both = jnp.stack([valid, valid & (kb >= Q_BLOCK)])
    return jnp.where(both, 0.0, MASKED).astype(BF16)


def _stacked_identity():
    row = jnp.arange(2 * Q_BLOCK)[:, None]
    col = jnp.arange(LANES)[None, :]
    return (row % Q_BLOCK == col).astype(BF16)


def kernel(x, norm_mix, w_in, w_pool_group, pool_scale, w_att_branch, w_pool_branch, w_out,
           norm_ffn, w_ffn_gate, w_ffn_up, w_ffn_down, norm_final):
    batch, seq_len, d_model = x.shape
    depth = w_in.shape[0]
    tokens = batch * seq_len
    assert depth == 1, "stacked layers need the un-normalised residual between layers"
    bias = _band_bias()

    h = x.reshape(tokens, d_model)
    qkv0, qkv1, qkv2, pool_feat, gates = _in_proj(
        h, norm_mix[0][None, :], w_in[0].astype(BF16), w_pool_group[0].astype(BF16),
        pool_scale[0][None, :], _rope_tables(seq_len), batch, seq_len)
    qkv_groups = (qkv0.reshape(batch, 1, seq_len, GROUP_QKV_WIDTH), qkv1, qkv2)
    eye = _stacked_identity()
    att = [_attention_group(qkv_groups[g], bias, eye, batch, seq_len, g)
           for g in range(N_ATT_GROUPS)]
    out = _merge_ffn(
        h, [a[0] for a in att], [a[1] for a in att], [a[2] for a in att], pool_feat, gates,
        w_att_branch[0].astype(BF16), w_pool_branch[0].astype(BF16), w_out[0].astype(BF16),
        norm_ffn[0][None, :], w_ffn_gate[0].astype(BF16), w_ffn_up[0].astype(BF16),
        w_ffn_down[0].astype(BF16), norm_final[None, :], seq_len)
    return out.reshape(batch, seq_len, d_model)
```

```python
import functools
import math

import jax
import jax.numpy as jnp
import numpy as np
from jax import lax
from jax.experimental import pallas as pl
from jax.experimental.pallas import tpu as pltpu

F32 = jnp.float32
BF16 = jnp.bfloat16

HEAD_DIM = 64
HEADS_PER_GROUP = 8
ATT_GROUPS = ((128, 1), (512, 4), (2048, 16))
N_ATT_GROUPS = len(ATT_GROUPS)
ATT_OUT_WIDTH = HEADS_PER_GROUP * HEAD_DIM
ATT_WIDTH = N_ATT_GROUPS * ATT_OUT_WIDTH
QKV_WIDTH = 3 * ATT_WIDTH
GROUP_QKV_WIDTH = 3 * ATT_OUT_WIDTH
ROT_DIM = HEAD_DIM // 4
ROT_HALF = ROT_DIM // 2
ROPE_THETA = 500000.0
POOL_WINDOWS = (2, 4, 8, 16)
POOL_GROUP_WIDTH = 128
POOL_WIDTH = len(POOL_WINDOWS) * POOL_GROUP_WIDTH
NORM_EPS = 1e-6
WINDOW_KEYS = 128
MASKED = -3.0e38
Q_SCALE = HEAD_DIM ** -0.5 * math.log2(math.e)

LANES = 128
V7X_VMEM_LIMIT_BYTES = 56 * 1024 * 1024

PROJ_ROWS = 512
PROJ_SUB_ROWS = 256
PROJ_COLS = 512
POOL_HALO = 16
ATT_ROWS = 1024
Q_BLOCK = 128
HEAD_PAIRS = HEADS_PER_GROUP // 2
FFN_ROWS = 512
FFN_SUB_ROWS = 256
FFN_CHUNKS = ((0, 1024), (1024, 2048), (2048, 2816))
SPLIT_PIECES = 3


def _rms_norm(xf, gain):
    ms = jnp.mean(xf * xf, axis=-1, keepdims=True)
    return (xf * lax.rsqrt(ms + NORM_EPS)) * gain


def _in_proj_kernel(x_ref, gain_ref, w_ref, wpg_ref, pscale_ref, rope_ref,
                    qkv0_ref, qkv1_ref, qkv2_ref, pool_ref, gate_ref,
                    u_ref, u4_ref, u16_ref, uslab_ref, u4slab_ref, tab4_ref, tab16_ref,
                    z0_ref, z1_ref, z2_ref, z3_ref, *, tiles_per_seq):
    rows, d_model = x_ref.shape
    seq_tile = pl.program_id(0) % tiles_per_seq
    n_slabs = d_model // LANES
    sub_rows = PROJ_SUB_ROWS
    n_sub = rows // sub_rows
    run4 = sub_rows // 4
    run16 = sub_rows // 16
    levels = (z0_ref, z1_ref, z2_ref, z3_ref)

    @pl.when(seq_tile == 0)
    def _():
        for ref in levels:
            ref[0:POOL_HALO, :] = jnp.zeros((POOL_HALO, POOL_WIDTH), F32)

    for sub in range(n_sub):
        row0 = sub * sub_rows
        uf = _rms_norm(x_ref[row0:row0 + sub_rows, :], gain_ref[...])
        u_ref[sub] = uf.astype(BF16)
        for s in range(n_slabs):
            uslab_ref[s] = uf[:, s * LANES:(s + 1) * LANES]
        for r in range(4):
            dst = slice(r * run4, (r + 1) * run4)
            for s in range(n_slabs):
                piece = uslab_ref[s, pl.ds(r, run4, stride=4), :]
                u4slab_ref[s, dst, :] = piece
                u4_ref[sub, dst, s * LANES:(s + 1) * LANES] = piece.astype(BF16)
            for t in range(3):
                tab4_ref[sub, t, dst, :] = rope_ref[t, pl.ds(row0 + r, run4, stride=4), :]
        for run in range(16):
            src0 = (run // 4) * run4 + run % 4
            dst = slice(run * run16, (run + 1) * run16)
            for s in range(n_slabs):
                piece = u4slab_ref[s, pl.ds(src0, run16, stride=4), :]
                u16_ref[sub, dst, s * LANES:(s + 1) * LANES] = piece.astype(BF16)
            for t in range(3):
                tab16_ref[sub, t, dst, :] = tab4_ref[sub, t, pl.ds(src0, run16, stride=4), :]

    def project(lhs_ref, sub, col0):
        return jnp.dot(lhs_ref[sub], w_ref[:, col0:col0 + PROJ_COLS],
                       preferred_element_type=F32)

    for sub in range(n_sub):
        row0 = POOL_HALO + sub * sub_rows
        z0_ref[row0:row0 + sub_rows, :] = project(u_ref, sub, QKV_WIDTH)

    cur = slice(POOL_HALO, POOL_HALO + rows)
    sums = []
    prev = z0_ref[cur, :]
    for lvl, ref in enumerate(levels):
        shift = 1 << lvl
        lo = lvl * POOL_GROUP_WIDTH
        shifted = ref[POOL_HALO - shift:POOL_HALO - shift + rows, lo:]
        prev = prev[:, (POOL_GROUP_WIDTH if lvl else 0):] + shifted
        sums.append(prev[:, 0:POOL_GROUP_WIDTH])
        if lvl + 1 < len(levels):
            levels[lvl + 1][cur, lo:] = prev
    pos = seq_tile * rows + lax.broadcasted_iota(jnp.int32, (rows, 1), 0)
    pooled = []
    for g, w in enumerate(POOL_WINDOWS):
        cols = slice(g * POOL_GROUP_WIDTH, (g + 1) * POOL_GROUP_WIDTH)
        count = jnp.minimum(pos + 1, w).astype(F32)
        pooled.append((sums[g] / count - z0_ref[cur, cols]).astype(BF16))
    for lvl, ref in enumerate(levels):
        lo = lvl * POOL_GROUP_WIDTH
        ref[0:POOL_HALO, lo:] = ref[rows:rows + POOL_HALO, lo:]

    def rope(xs, table):
        up = pltpu.roll(xs, LANES - ROT_HALF, axis=1)
        down = pltpu.roll(xs, ROT_HALF, axis=1)
        return xs * table(0) + up * table(1) + down * table(2)

    def store_natural(sub, col, part):
        qkv0_ref[sub * sub_rows:(sub + 1) * sub_rows, col:col + LANES] = part

    def store_by4(sub, col, part):
        for r in range(4):
            qkv1_ref[r, sub * run4:(sub + 1) * run4, col:col + LANES] = (
                part[r * run4:(r + 1) * run4])

    def store_by16(sub, col, part):
        for run in range(16):
            stream = 4 * (run % 4) + run // 4
            qkv2_ref[stream, sub * run16:(sub + 1) * run16, col:col + LANES] = (
                part[run * run16:(run + 1) * run16])

    for sub in range(n_sub):
        natural = slice(sub * sub_rows, (sub + 1) * sub_rows)
        variants = (
            (u_ref, lambda t: rope_ref[t, natural, :], store_natural),
            (u4_ref, lambda t: tab4_ref[sub, t], store_by4),
            (u16_ref, lambda t: tab16_ref[sub, t], store_by16),
        )
        for group, (lhs_ref, table, store) in enumerate(variants):
            for which in range(3):
                acc = project(lhs_ref, sub, which * ATT_WIDTH + group * ATT_OUT_WIDTH)
                for j in range(PROJ_COLS // LANES):
                    part = acc[:, j * LANES:(j + 1) * LANES]
                    if which == 0:
                        part = rope(part, table) * Q_SCALE
                    elif which == 1:
                        part = rope(part, table)
                    store(sub, which * ATT_OUT_WIDTH + j * LANES, part.astype(BF16))

    gate0 = QKV_WIDTH + POOL_WIDTH
    for sub in range(n_sub):
        natural = slice(sub * sub_rows, (sub + 1) * sub_rows)
        for c in range(gate_ref.shape[1] // PROJ_COLS):
            gate_ref[natural, c * PROJ_COLS:(c + 1) * PROJ_COLS] = (
                project(u_ref, sub, gate0 + c * PROJ_COLS))

    for g in range(len(POOL_WINDOWS)):
        cols = slice(g * POOL_GROUP_WIDTH, (g + 1) * POOL_GROUP_WIDTH)
        mapped = jnp.dot(pooled[g], wpg_ref[g], preferred_element_type=F32)
        pool_ref[:, cols] = (mapped * pscale_ref[:, cols]).astype(BF16)


def _in_proj(x2, gain, w_in, w_pool_group, pool_scale, rope_tab, batch, seq_len):
    tokens, d_model = x2.shape
    in_width = w_in.shape[1]
    gate_width = in_width - QKV_WIDTH - POOL_WIDTH
    rows = PROJ_ROWS
    n_sub = rows // PROJ_SUB_ROWS
    tiles_per_seq = seq_len // rows
    const = dict(pipeline_mode=pl.Buffered(1))

    def stream_spec(dilation):
        return pl.BlockSpec((None, dilation, rows // dilation, GROUP_QKV_WIDTH),
                            lambda i: (i // tiles_per_seq, 0, i % tiles_per_seq, 0))

    return pl.pallas_call(
        functools.partial(_in_proj_kernel, tiles_per_seq=tiles_per_seq),
        name="in_proj",
        grid=(tokens // rows,),
        in_specs=[
            pl.BlockSpec((rows, d_model), lambda i: (i, 0)),
            pl.BlockSpec((1, d_model), lambda i: (0, 0), **const),
            pl.BlockSpec((d_model, in_width), lambda i: (0, 0), **const),
            pl.BlockSpec(w_pool_group.shape, lambda i: (0, 0, 0), **const),
            pl.BlockSpec((1, POOL_WIDTH), lambda i: (0, 0), **const),
            pl.BlockSpec((3, rows, LANES), lambda i: (0, i % tiles_per_seq, 0)),
        ],
        out_specs=[
            pl.BlockSpec((rows, GROUP_QKV_WIDTH), lambda i: (i, 0)),
            stream_spec(4),
            stream_spec(16),
            pl.BlockSpec((rows, POOL_WIDTH), lambda i: (i, 0)),
            pl.BlockSpec((rows, gate_width), lambda i: (i, 0)),
        ],
        out_shape=[
            jax.ShapeDtypeStruct((tokens, GROUP_QKV_WIDTH), BF16),
            jax.ShapeDtypeStruct((batch, 4, seq_len // 4, GROUP_QKV_WIDTH), BF16),
            jax.ShapeDtypeStruct((batch, 16, seq_len // 16, GROUP_QKV_WIDTH), BF16),
            jax.ShapeDtypeStruct((tokens, POOL_WIDTH), BF16),
            jax.ShapeDtypeStruct((tokens, gate_width), F32),
        ],
        scratch_shapes=[pltpu.VMEM((n_sub, PROJ_SUB_ROWS, d_model), BF16)] * 3
        + [pltpu.VMEM((d_model // LANES, PROJ_SUB_ROWS, LANES), F32)] * 2
        + [pltpu.VMEM((n_sub, 3, PROJ_SUB_ROWS, LANES), F32)] * 2
        + [pltpu.VMEM((POOL_HALO + rows, POOL_WIDTH), F32) for _ in POOL_WINDOWS],
        compiler_params=pltpu.CompilerParams(
            dimension_semantics=("arbitrary",), vmem_limit_bytes=V7X_VMEM_LIMIT_BYTES),
    )(x2, gain, w_in, w_pool_group, pool_scale, rope_tab)


def _stat_lane(head):
    return head // 2 + (HEAD_DIM if head % 2 == 0 else 0)


def _piece_shift(group, piece):
    return HEAD_PAIRS * (SPLIT_PIECES * group + piece)


def _attention_kernel(q_ref, k_ref, v_ref, bias_ref, eye_ref, o_ref, m_ref, l_ref,
                      kcarry_ref, vaug_ref,
                      *, single_step):
    dilation, rows, _ = q_ref.shape
    first = pl.program_id(1) == 0

    lane = lax.broadcasted_iota(jnp.int32, (Q_BLOCK, LANES), 1)
    low_half = lane < HEAD_DIM
    head_mask = (jnp.where(low_half, 1.0, 0.0).astype(BF16),
                 jnp.where(low_half, 0.0, 1.0).astype(BF16))
    low_rows = lax.broadcasted_iota(jnp.int32, (rows, LANES), 1) < HEAD_DIM
    one = jnp.ones((rows, LANES), BF16)
    nt_dims = (((1,), (1,)), ((), ()))
    zero_block = jnp.zeros((Q_BLOCK, ATT_OUT_WIDTH), BF16)

    def clear_carry():
        for st in range(dilation):
            kcarry_ref[st, 0:Q_BLOCK, :] = zero_block
            vaug_ref[0, st, 0:Q_BLOCK, :] = zero_block
            vaug_ref[1, st, 0:Q_BLOCK, :] = zero_block

    if single_step:
        clear_carry()
    else:
        pl.when(first)(clear_carry)

    for st in range(dilation):
        kcarry_ref[st, Q_BLOCK:2 * Q_BLOCK, :] = k_ref[st, 0:Q_BLOCK, :]
        for pair in range(HEAD_PAIRS):
            cols = slice(pair * LANES, (pair + 1) * LANES)
            v_pair = v_ref[st, :, cols]
            vaug_ref[0, st, Q_BLOCK:Q_BLOCK + rows, cols] = jnp.where(low_rows, v_pair, one)
            vaug_ref[1, st, Q_BLOCK:Q_BLOCK + rows, cols] = jnp.where(low_rows, one, v_pair)

        for sub in range(rows // Q_BLOCK):
            band = slice(sub * Q_BLOCK, (sub + 2) * Q_BLOCK)
            q_rows = slice(sub * Q_BLOCK, (sub + 1) * Q_BLOCK)
            if sub == 0:
                bias_idx = 1 if single_step else jnp.where(first, 1, 0)
            else:
                bias_idx = 0
            if dilation == 1:
                tokens = pl.ds(sub * Q_BLOCK, Q_BLOCK)
            else:
                tokens = pl.ds(sub * Q_BLOCK * dilation + st, Q_BLOCK, stride=dilation)
            m_tile = jnp.zeros((Q_BLOCK, LANES), F32)
            l_tile = jnp.ones((Q_BLOCK, LANES), F32)
            for pair in range(HEAD_PAIRS):
                cols = slice(pair * LANES, (pair + 1) * LANES)
                q_pair = q_ref[st, q_rows, cols]
                if sub == 0:
                    k_pair = kcarry_ref[st, :, cols]
                else:
                    k_pair = k_ref[st, (sub - 1) * Q_BLOCK:(sub + 1) * Q_BLOCK, cols]
                q_both = jnp.concatenate([q_pair * head_mask[0], q_pair * head_mask[1]], axis=0)
                q_both = jnp.concatenate([q_both, eye_ref[...]], axis=1)
                k_aug = jnp.concatenate([k_pair, bias_ref[bias_idx]], axis=1)
                s_both = lax.dot_general(q_both, k_aug, nt_dims, preferred_element_type=F32)
                pvs = []
                for half in range(2):
                    s = s_both[half * Q_BLOCK:(half + 1) * Q_BLOCK]
                    m = jnp.max(s, axis=-1, keepdims=True)
                    p = jnp.exp2(s - m).astype(BF16)
                    pv = jnp.dot(p, vaug_ref[half, st, band, cols], preferred_element_type=F32)
                    pvs.append(pv)
                    at_stat = lane == _stat_lane(2 * pair + half)
                    m_tile = jnp.where(at_stat, m, m_tile)
                    l_tile = jnp.where(at_stat, pv, l_tile)
                o_ref[pair, tokens, :] = jnp.where(low_half, pvs[0], pvs[1])
            m_ref[tokens, :] = m_tile
            l_ref[tokens, :] = l_tile

        kcarry_ref[st, 0:Q_BLOCK, :] = k_ref[st, rows - Q_BLOCK:rows, :]
        for half in range(2):
            vaug_ref[half, st, 0:Q_BLOCK, :] = vaug_ref[half, st, rows:rows + Q_BLOCK, :]


def _attention_group(qkv, bias, eye, batch, seq_len, group):
    _, dilation = ATT_GROUPS[group]
    stream_len = seq_len // dilation
    rows = max(Q_BLOCK, ATT_ROWS // dilation)
    block_tokens = rows * dilation
    single_step = rows == stream_len

    def qkv_spec(which):
        return pl.BlockSpec((None, dilation, rows, ATT_OUT_WIDTH), lambda b, m: (b, 0, m, which))

    stat_spec = pl.BlockSpec((None, block_tokens, LANES), lambda b, m: (b, m, 0))
    stat_shape = jax.ShapeDtypeStruct((batch, seq_len, LANES), F32)
    return pl.pallas_call(
        functools.partial(_attention_kernel, single_step=single_step),
        name=f"attention_g{group}",
        grid=(batch, stream_len // rows),
        in_specs=[
            qkv_spec(0), qkv_spec(1), qkv_spec(2),
            pl.BlockSpec(bias.shape, lambda b, m: (0, 0, 0), pipeline_mode=pl.Buffered(1)),
            pl.BlockSpec(eye.shape, lambda b, m: (0, 0), pipeline_mode=pl.Buffered(1)),
        ],
        out_specs=[
            pl.BlockSpec((None, HEAD_PAIRS, block_tokens, LANES), lambda b, m: (b, 0, m, 0)),
            stat_spec, stat_spec,
        ],
        out_shape=[
            jax.ShapeDtypeStruct((batch, HEAD_PAIRS, seq_len, LANES), F32),
            stat_shape, stat_shape,
        ],
        scratch_shapes=[
            pltpu.VMEM((dilation, 2 * Q_BLOCK, ATT_OUT_WIDTH), BF16),
            pltpu.VMEM((2, dilation, Q_BLOCK + rows, ATT_OUT_WIDTH), BF16),
        ],
        compiler_params=pltpu.CompilerParams(
            dimension_semantics=("arbitrary", "arbitrary"),
            vmem_limit_bytes=V7X_VMEM_LIMIT_BYTES),
    )(qkv, qkv, qkv, bias, eye)


def _merge_ffn_kernel(x_ref, o0_ref, o1_ref, o2_ref, m0_ref, m1_ref, m2_ref, l0_ref, l1_ref, l2_ref,
                      pool_ref, gate_ref, spread_ref,
                      wab_ref, wpb_ref, wout_ref, gffn_ref, wg_ref, wu_ref, wd_ref, gfin_ref,
                      out_ref, att_ref):
    d_model = x_ref.shape[1]
    o_refs = (o0_ref, o1_ref, o2_ref)
    m_refs = (m0_ref, m1_ref, m2_ref)
    l_refs = (l0_ref, l1_ref, l2_ref)
    lane = lax.broadcasted_iota(jnp.int32, (FFN_SUB_ROWS, LANES), 1)
    stat_lanes = (lane % HEAD_DIM) < HEAD_PAIRS

    for sub in range(x_ref.shape[0] // FFN_SUB_ROWS):
        rows = slice(sub * FFN_SUB_ROWS, (sub + 1) * FFN_SUB_ROWS)

        ms = [ref[rows, :] for ref in m_refs]
        ls = [ref[rows, :] for ref in l_refs]
        top = jnp.maximum(jnp.maximum(ms[0], ms[1]), ms[2])
        es = [jnp.exp2(m - top) for m in ms]
        den = es[0] * ls[0] + es[1] * ls[1] + es[2] * ls[2]

        packed = jnp.zeros((FFN_SUB_ROWS, LANES), F32)
        for g in range(N_ATT_GROUPS):
            rest = jnp.where(stat_lanes, es[g] / den, 0.0)
            for piece in range(SPLIT_PIECES):
                part = rest.astype(BF16).astype(F32)
                rest = rest - part
                shift = _piece_shift(g, piece)
                packed = packed + (pltpu.roll(part, shift, axis=1) if shift else part)
        spread = jnp.dot(packed.astype(BF16), spread_ref[...], preferred_element_type=F32)

        for pair in range(HEAD_PAIRS):
            acc = jnp.zeros((FFN_SUB_ROWS, LANES), F32)
            for g in range(N_ATT_GROUPS):
                col = g * ATT_OUT_WIDTH + pair * LANES
                acc = acc + spread[:, col:col + LANES] * o_refs[g][pair, rows, :]
            att_ref[rows, pair * LANES:(pair + 1) * LANES] = acc.astype(BF16)

        y_att = jnp.dot(att_ref[rows, :], wab_ref[...], preferred_element_type=F32)
        y_pool = jnp.dot(pool_ref[rows, :], wpb_ref[...], preferred_element_type=F32)
        merged = (jax.nn.sigmoid(gate_ref[rows, 0:d_model]) * y_att
                  + jax.nn.sigmoid(gate_ref[rows, d_model:2 * d_model]) * y_pool)
        h = x_ref[rows, :] + jnp.dot(merged.astype(BF16), wout_ref[...],
                                     preferred_element_type=F32)

        f = _rms_norm(h, gffn_ref[...]).astype(BF16)
        ffn = jnp.zeros((FFN_SUB_ROWS, d_model), F32)
        for lo, hi in FFN_CHUNKS:
            gate = jnp.dot(f, wg_ref[:, lo:hi], preferred_element_type=F32)
            up = jnp.dot(f, wu_ref[:, lo:hi], preferred_element_type=F32)
            hidden = (jax.nn.silu(gate) * up).astype(BF16)
            ffn = ffn + jnp.dot(hidden, wd_ref[lo:hi, :], preferred_element_type=F32)
        out_ref[rows, :] = _rms_norm(h + ffn, gfin_ref[...])


def _merge_ffn(x2, outs, maxes, sums, pool_feat, gates, w_att_branch, w_pool_branch, w_out, norm_ffn,
               w_gate, w_up, w_down, norm_final, seq_len):
    tokens, d_model = x2.shape
    rows = FFN_ROWS
    tiles_per_seq = seq_len // rows

    def tile(width):
        return pl.BlockSpec((rows, width), lambda i: (i, 0))

    def whole(arr):
        return pl.BlockSpec(arr.shape, lambda i: (0, 0), pipeline_mode=pl.Buffered(1))

    o_spec = pl.BlockSpec((None, HEAD_PAIRS, rows, LANES),
                          lambda i: (i // tiles_per_seq, 0, i % tiles_per_seq, 0))
    stat_spec = pl.BlockSpec((None, rows, LANES),
                             lambda i: (i // tiles_per_seq, i % tiles_per_seq, 0))
    weights = (_spread_matrix(), w_att_branch, w_pool_branch, w_out, norm_ffn, w_gate, w_up,
               w_down, norm_final)
    return pl.pallas_call(
        _merge_ffn_kernel,
        name="merge_ffn",
        grid=(tokens // rows,),
        in_specs=[tile(d_model)]
        + [o_spec] * N_ATT_GROUPS
        + [stat_spec] * (2 * N_ATT_GROUPS)
        + [tile(POOL_WIDTH), tile(gates.shape[1])]
        + [whole(w) for w in weights],
        out_specs=tile(d_model),
        out_shape=jax.ShapeDtypeStruct((tokens, d_model), F32),
        scratch_shapes=[pltpu.VMEM((rows, ATT_OUT_WIDTH), BF16)],
        compiler_params=pltpu.CompilerParams(
            dimension_semantics=("arbitrary",), vmem_limit_bytes=V7X_VMEM_LIMIT_BYTES),
    )(x2, *outs, *maxes, *sums, pool_feat, gates, *weights)


def _rope_tables(seq_len):
    inv_freq = ROPE_THETA ** (-np.arange(0, ROT_DIM, 2, dtype=np.float64) / ROT_DIM)
    ang = np.arange(seq_len, dtype=np.float64)[:, None] * inv_freq[None, :]
    cos, sin = np.cos(ang), np.sin(ang)
    zeros = np.zeros((seq_len, HEAD_DIM - ROT_DIM))
    half0 = np.zeros((seq_len, ROT_HALF))
    c_head = np.concatenate([cos, cos, zeros + 1.0], axis=1)
    lo_head = np.concatenate([-sin, half0, zeros], axis=1)
    hi_head = np.concatenate([half0, sin, zeros], axis=1)
    tab = np.stack([c_head, lo_head, hi_head])
    return jnp.asarray(np.concatenate([tab] * (LANES // HEAD_DIM), axis=2), dtype=F32)


def _spread_matrix():
    row = jnp.arange(LANES)[:, None]
    col = jnp.arange(N_ATT_GROUPS * ATT_OUT_WIDTH)[None, :]
    group = col // ATT_OUT_WIDTH
    head = (col % ATT_OUT_WIDTH) // HEAD_DIM
    stat = head // 2 + jnp.where(head % 2 == 0, HEAD_DIM, 0)
    hit = jnp.zeros((LANES, N_ATT_GROUPS * ATT_OUT_WIDTH), jnp.bool_)
    for piece in range(SPLIT_PIECES):
        hit = hit | (row == stat + HEAD_PAIRS * (SPLIT_PIECES * group + piece))
    return hit.astype(BF16)


def _band_bias():
    kb = jnp.arange(2 * Q_BLOCK)[:, None]
    qi = jnp.arange(Q_BLOCK)[None, :]
    dist = qi + Q_BLOCK - kb
    valid = (dist >= 0) & (dist <= WINDOW_KEYS)
    both = jnp.stack([valid, valid & (kb >= Q_BLOCK)])
    return jnp.where(both, 0.0, MASKED).astype(BF16)


def _stacked_identity():
    row = jnp.arange(2 * Q_BLOCK)[:, None]
    col = jnp.arange(LANES)[None, :]
    return (row % Q_BLOCK == col).astype(BF16)


def kernel(x, norm_mix, w_in, w_pool_group, pool_scale, w_att_branch, w_pool_branch, w_out,
           norm_ffn, w_ffn_gate, w_ffn_up, w_ffn_down, norm_final):
    batch, seq_len, d_model = x.shape
    depth = w_in.shape[0]
    tokens = batch * seq_len
    assert depth == 1, "stacked layers need the un-normalised residual between layers"
    bias = _band_bias()

    h = x.reshape(tokens, d_model)
    qkv0, qkv1, qkv2, pool_feat, gates = _in_proj(
        h, norm_mix[0][None, :], w_in[0].astype(BF16), w_pool_group[0].astype(BF16),
        pool_scale[0][None, :], _rope_tables(seq_len), batch, seq_len)
    qkv_groups = (qkv0.reshape(batch, 1, seq_len, GROUP_QKV_WIDTH), qkv1, qkv2)
    eye = _stacked_identity()
    att = [_attention_group(qkv_groups[g], bias, eye, batch, seq_len, g)
           for g in range(N_ATT_GROUPS)]
    out = _merge_ffn(
        h, [a[0] for a in att], [a[1] for a in att], [a[2] for a in att], pool_feat, gates,
        w_att_branch[0].astype(BF16), w_pool_branch[0].astype(BF16), w_out[0].astype(BF16),
        norm_ffn[0][None, :], w_ffn_gate[0].astype(BF16), w_ffn_up[0].astype(BF16),
        w_ffn_down[0].astype(BF16), norm_final[None, :], seq_len)
    return out.reshape(batch, seq_len, d_model)
```

```python
import functools
import math

import jax
import jax.numpy as jnp
import numpy as np
from jax import lax
from jax.experimental import pallas as pl
from jax.experimental.pallas import tpu as pltpu

F32 = jnp.float32
BF16 = jnp.bfloat16

HEAD_DIM = 64
HEADS_PER_GROUP = 8
ATT_GROUPS = ((128, 1), (512, 4), (2048, 16))
N_ATT_GROUPS = len(ATT_GROUPS)
ATT_OUT_WIDTH = HEADS_PER_GROUP * HEAD_DIM
ATT_WIDTH = N_ATT_GROUPS * ATT_OUT_WIDTH
QKV_WIDTH = 3 * ATT_WIDTH
GROUP_QKV_WIDTH = 3 * ATT_OUT_WIDTH
ROT_DIM = HEAD_DIM // 4
ROT_HALF = ROT_DIM // 2
ROPE_THETA = 500000.0
POOL_WINDOWS = (2, 4, 8, 16)
POOL_GROUP_WIDTH = 128
POOL_WIDTH = len(POOL_WINDOWS) * POOL_GROUP_WIDTH
NORM_EPS = 1e-6
WINDOW_KEYS = 128
MASKED = -3.0e38
Q_SCALE = HEAD_DIM ** -0.5 * math.log2(math.e)

LANES = 128
V7X_VMEM_LIMIT_BYTES = 56 * 1024 * 1024

PROJ_ROWS = 512
PROJ_SUB_ROWS = 256
PROJ_COLS = 512
POOL_HALO = 16
ATT_ROWS = 2048
Q_BLOCK = 128
HEAD_PAIRS = HEADS_PER_GROUP // 2
FFN_ROWS = 512
FFN_SUB_ROWS = 256
FFN_CHUNKS = ((0, 1024), (1024, 2048), (2048, 2816))
SPLIT_PIECES = 3


def _rms_norm(xf, gain):
    ms = jnp.mean(xf * xf, axis=-1, keepdims=True)
    return (xf * lax.rsqrt(ms + NORM_EPS)) * gain


def _in_proj_kernel(x_ref, gain_ref, w_ref, wpg_ref, pscale_ref, rope_ref,
                    qkv0_ref, qkv1_ref, qkv2_ref, pool_ref, gate_ref,
                    u_ref, u4_ref, u16_ref, uslab_ref, u4slab_ref, tab4_ref, tab16_ref,
                    z0_ref, z1_ref, z2_ref, z3_ref, *, tiles_per_seq):
    rows, d_model = x_ref.shape
    seq_tile = pl.program_id(0) % tiles_per_seq
    n_slabs = d_model // LANES
    sub_rows = PROJ_SUB_ROWS
    n_sub = rows // sub_rows
    run4 = sub_rows // 4
    run16 = sub_rows // 16
    levels = (z0_ref, z1_ref, z2_ref, z3_ref)

    @pl.when(seq_tile == 0)
    def _():
        for ref in levels:
            ref[0:POOL_HALO, :] = jnp.zeros((POOL_HALO, POOL_WIDTH), F32)

    for sub in range(n_sub):
        row0 = sub * sub_rows
        uf = _rms_norm(x_ref[row0:row0 + sub_rows, :], gain_ref[...])
        u_ref[sub] = uf.astype(BF16)
        for s in range(n_slabs):
            uslab_ref[s] = uf[:, s * LANES:(s + 1) * LANES]
        for r in range(4):
            dst = slice(r * run4, (r + 1) * run4)
            for s in range(n_slabs):
                piece = uslab_ref[s, pl.ds(r, run4, stride=4), :]
                u4slab_ref[s, dst, :] = piece
                u4_ref[sub, dst, s * LANES:(s + 1) * LANES] = piece.astype(BF16)
            for t in range(3):
                tab4_ref[sub, t, dst, :] = rope_ref[t, pl.ds(row0 + r, run4, stride=4), :]
        for run in range(16):
            src0 = (run // 4) * run4 + run % 4
            dst = slice(run * run16, (run + 1) * run16)
            for s in range(n_slabs):
                piece = u4slab_ref[s, pl.ds(src0, run16, stride=4), :]
                u16_ref[sub, dst, s * LANES:(s + 1) * LANES] = piece.astype(BF16)
            for t in range(3):
                tab16_ref[sub, t, dst, :] = tab4_ref[sub, t, pl.ds(src0, run16, stride=4), :]

    def project(lhs_ref, sub, col0):
        return jnp.dot(lhs_ref[sub], w_ref[:, col0:col0 + PROJ_COLS],
                       preferred_element_type=F32)

    for sub in range(n_sub):
        row0 = POOL_HALO + sub * sub_rows
        z0_ref[row0:row0 + sub_rows, :] = project(u_ref, sub, QKV_WIDTH)

    gate0 = QKV_WIDTH + POOL_WIDTH
    for sub in range(n_sub):
        natural = slice(sub * sub_rows, (sub + 1) * sub_rows)
        for c in range(gate_ref.shape[1] // PROJ_COLS):
            gate_ref[natural, c * PROJ_COLS:(c + 1) * PROJ_COLS] = (
                project(u_ref, sub, gate0 + c * PROJ_COLS))

    cur = slice(POOL_HALO, POOL_HALO + rows)
    sums = []
    prev = z0_ref[cur, :]
    for lvl, ref in enumerate(levels):
        shift = 1 << lvl
        lo = lvl * POOL_GROUP_WIDTH
        shifted = ref[POOL_HALO - shift:POOL_HALO - shift + rows, lo:]
        prev = prev[:, (POOL_GROUP_WIDTH if lvl else 0):] + shifted
        sums.append(prev[:, 0:POOL_GROUP_WIDTH])
        if lvl + 1 < len(levels):
            levels[lvl + 1][cur, lo:] = prev
    pos = seq_tile * rows + lax.broadcasted_iota(jnp.int32, (rows, 1), 0)
    pooled = []
    for g, w in enumerate(POOL_WINDOWS):
        cols = slice(g * POOL_GROUP_WIDTH, (g + 1) * POOL_GROUP_WIDTH)
        count = jnp.minimum(pos + 1, w).astype(F32)
        pooled.append((sums[g] / count - z0_ref[cur, cols]).astype(BF16))
    for lvl, ref in enumerate(levels):
        lo = lvl * POOL_GROUP_WIDTH
        ref[0:POOL_HALO, lo:] = ref[rows:rows + POOL_HALO, lo:]

    def rope(xs, table):
        up = pltpu.roll(xs, LANES - ROT_HALF, axis=1)
        down = pltpu.roll(xs, ROT_HALF, axis=1)
        return xs * table(0) + up * table(1) + down * table(2)

    def store_natural(sub, col, part):
        qkv0_ref[sub * sub_rows:(sub + 1) * sub_rows, col:col + LANES] = part

    def store_by4(sub, col, part):
        for r in range(4):
            qkv1_ref[r, sub * run4:(sub + 1) * run4, col:col + LANES] = (
                part[r * run4:(r + 1) * run4])

    def store_by16(sub, col, part):
        for run in range(16):
            stream = 4 * (run % 4) + run // 4
            qkv2_ref[stream, sub * run16:(sub + 1) * run16, col:col + LANES] = (
                part[run * run16:(run + 1) * run16])

    for sub in range(n_sub):
        natural = slice(sub * sub_rows, (sub + 1) * sub_rows)
        variants = (
            (u_ref, lambda t: rope_ref[t, natural, :], store_natural),
            (u4_ref, lambda t: tab4_ref[sub, t], store_by4),
            (u16_ref, lambda t: tab16_ref[sub, t], store_by16),
        )
        for group, (lhs_ref, table, store) in enumerate(variants):
            for which in range(3):
                acc = project(lhs_ref, sub, which * ATT_WIDTH + group * ATT_OUT_WIDTH)
                for j in range(PROJ_COLS // LANES):
                    part = acc[:, j * LANES:(j + 1) * LANES]
                    if which == 0:
                        part = rope(part, table) * Q_SCALE
                    elif which == 1:
                        part = rope(part, table)
                    store(sub, which * ATT_OUT_WIDTH + j * LANES, part.astype(BF16))

    for g in range(len(POOL_WINDOWS)):
        cols = slice(g * POOL_GROUP_WIDTH, (g + 1) * POOL_GROUP_WIDTH)
        mapped = jnp.dot(pooled[g], wpg_ref[g], preferred_element_type=F32)
        pool_ref[:, cols] = (mapped * pscale_ref[:, cols]).astype(BF16)


def _in_proj(x2, gain, w_in, w_pool_group, pool_scale, rope_tab, batch, seq_len):
    tokens, d_model = x2.shape
    in_width = w_in.shape[1]
    gate_width = in_width - QKV_WIDTH - POOL_WIDTH
    rows = PROJ_ROWS
    n_sub = rows // PROJ_SUB_ROWS
    tiles_per_seq = seq_len // rows
    const = dict(pipeline_mode=pl.Buffered(1))

    def stream_spec(dilation):
        return pl.BlockSpec((None, dilation, rows // dilation, GROUP_QKV_WIDTH),
                            lambda i: (i // tiles_per_seq, 0, i % tiles_per_seq, 0))

    return pl.pallas_call(
        functools.partial(_in_proj_kernel, tiles_per_seq=tiles_per_seq),
        name="in_proj",
        grid=(tokens // rows,),
        in_specs=[
            pl.BlockSpec((rows, d_model), lambda i: (i, 0)),
            pl.BlockSpec((1, d_model), lambda i: (0, 0), **const),
            pl.BlockSpec((d_model, in_width), lambda i: (0, 0), **const),
            pl.BlockSpec(w_pool_group.shape, lambda i: (0, 0, 0), **const),
            pl.BlockSpec((1, POOL_WIDTH), lambda i: (0, 0), **const),
            pl.BlockSpec((3, rows, LANES), lambda i: (0, i % tiles_per_seq, 0)),
        ],
        out_specs=[
            pl.BlockSpec((rows, GROUP_QKV_WIDTH), lambda i: (i, 0)),
            stream_spec(4),
            stream_spec(16),
            pl.BlockSpec((rows, POOL_WIDTH), lambda i: (i, 0)),
            pl.BlockSpec((rows, gate_width), lambda i: (i, 0)),
        ],
        out_shape=[
            jax.ShapeDtypeStruct((tokens, GROUP_QKV_WIDTH), BF16),
            jax.ShapeDtypeStruct((batch, 4, seq_len // 4, GROUP_QKV_WIDTH), BF16),
            jax.ShapeDtypeStruct((batch, 16, seq_len // 16, GROUP_QKV_WIDTH), BF16),
            jax.ShapeDtypeStruct((tokens, POOL_WIDTH), BF16),
            jax.ShapeDtypeStruct((tokens, gate_width), F32),
        ],
        scratch_shapes=[pltpu.VMEM((n_sub, PROJ_SUB_ROWS, d_model), BF16)] * 3
        + [pltpu.VMEM((d_model // LANES, PROJ_SUB_ROWS, LANES), F32)] * 2
        + [pltpu.VMEM((n_sub, 3, PROJ_SUB_ROWS, LANES), F32)] * 2
        + [pltpu.VMEM((POOL_HALO + rows, POOL_WIDTH), F32) for _ in POOL_WINDOWS],
        compiler_params=pltpu.CompilerParams(
            dimension_semantics=("arbitrary",), vmem_limit_bytes=V7X_VMEM_LIMIT_BYTES),
    )(x2, gain, w_in, w_pool_group, pool_scale, rope_tab)


def _stat_lane(head):
    return head // 2 + (HEAD_DIM if head % 2 == 0 else 0)


def _piece_shift(group, piece):
    return HEAD_PAIRS * (SPLIT_PIECES * group + piece)


def _attention_kernel(q_ref, k_ref, v_ref, bias_ref, eye_ref, o_ref, m_ref, l_ref,
                      kcarry_ref, vaug_ref,
                      *, single_step):
    dilation, rows, _ = q_ref.shape
    first = pl.program_id(1) == 0

    lane = lax.broadcasted_iota(jnp.int32, (Q_BLOCK, LANES), 1)
    low_half = lane < HEAD_DIM
    head_mask = (jnp.where(low_half, 1.0, 0.0).astype(BF16),
                 jnp.where(low_half, 0.0, 1.0).astype(BF16))
    low_rows = lax.broadcasted_iota(jnp.int32, (rows, LANES), 1) < HEAD_DIM
    one = jnp.ones((rows, LANES), BF16)
    nt_dims = (((1,), (1,)), ((), ()))
    zero_block = jnp.zeros((Q_BLOCK, ATT_OUT_WIDTH), BF16)

    def clear_carry():
        for st in range(dilation):
            kcarry_ref[st, 0:Q_BLOCK, :] = zero_block
            vaug_ref[0, st, 0:Q_BLOCK, :] = zero_block
            vaug_ref[1, st, 0:Q_BLOCK, :] = zero_block

    if single_step:
        clear_carry()
    else:
        pl.when(first)(clear_carry)

    for st in range(dilation):
        kcarry_ref[st, Q_BLOCK:2 * Q_BLOCK, :] = k_ref[st, 0:Q_BLOCK, :]
        for pair in range(HEAD_PAIRS):
            cols = slice(pair * LANES, (pair + 1) * LANES)
            v_pair = v_ref[st, :, cols]
            vaug_ref[0, st, Q_BLOCK:Q_BLOCK + rows, cols] = jnp.where(low_rows, v_pair, one)
            vaug_ref[1, st, Q_BLOCK:Q_BLOCK + rows, cols] = jnp.where(low_rows, one, v_pair)

        for sub in range(rows // Q_BLOCK):
            band = slice(sub * Q_BLOCK, (sub + 2) * Q_BLOCK)
            q_rows = slice(sub * Q_BLOCK, (sub + 1) * Q_BLOCK)
            if sub == 0:
                bias_idx = 1 if single_step else jnp.where(first, 1, 0)
            else:
                bias_idx = 0
            if dilation == 1:
                tokens = pl.ds(sub * Q_BLOCK, Q_BLOCK)
            else:
                tokens = pl.ds(sub * Q_BLOCK * dilation + st, Q_BLOCK, stride=dilation)
            m_tile = jnp.zeros((Q_BLOCK, LANES), F32)
            l_tile = jnp.ones((Q_BLOCK, LANES), F32)
            for pair in range(HEAD_PAIRS):
                cols = slice(pair * LANES, (pair + 1) * LANES)
                q_pair = q_ref[st, q_rows, cols]
                if sub == 0:
                    k_pair = kcarry_ref[st, :, cols]
                else:
                    k_pair = k_ref[st, (sub - 1) * Q_BLOCK:(sub + 1) * Q_BLOCK, cols]
                q_both = jnp.concatenate([q_pair * head_mask[0], q_pair * head_mask[1]], axis=0)
                q_both = jnp.concatenate([q_both, eye_ref[...]], axis=1)
                k_aug = jnp.concatenate([k_pair, bias_ref[bias_idx]], axis=1)
                s_both = lax.dot_general(q_both, k_aug, nt_dims, preferred_element_type=F32)
                pvs = []
                for half in range(2):
                    s = s_both[half * Q_BLOCK:(half + 1) * Q_BLOCK]
                    m = jnp.max(s, axis=-1, keepdims=True)
                    p = jnp.exp2(s - m).astype(BF16)
                    pv = jnp.dot(p, vaug_ref[half, st, band, cols], preferred_element_type=F32)
                    pvs.append(pv)
                    at_stat = lane == _stat_lane(2 * pair + half)
                    m_tile = jnp.where(at_stat, m, m_tile)
                    l_tile = jnp.where(at_stat, pv, l_tile)
                o_ref[pair, tokens, :] = jnp.where(low_half, pvs[0], pvs[1])
            m_ref[tokens, :] = m_tile
            l_ref[tokens, :] = l_tile

        kcarry_ref[st, 0:Q_BLOCK, :] = k_ref[st, rows - Q_BLOCK:rows, :]
        for half in range(2):
            vaug_ref[half, st, 0:Q_BLOCK, :] = vaug_ref[half, st, rows:rows + Q_BLOCK, :]


def _attention_group(qkv, bias, eye, batch, seq_len, group):
    _, dilation = ATT_GROUPS[group]
    stream_len = seq_len // dilation
    rows = max(Q_BLOCK, ATT_ROWS // dilation)
    block_tokens = rows * dilation
    single_step = rows == stream_len

    def qkv_spec(which):
        return pl.BlockSpec((None, dilation, rows, ATT_OUT_WIDTH), lambda b, m: (b, 0, m, which))

    stat_spec = pl.BlockSpec((None, block_tokens, LANES), lambda b, m: (b, m, 0))
    stat_shape = jax.ShapeDtypeStruct((batch, seq_len, LANES), F32)
    return pl.pallas_call(
        functools.partial(_attention_kernel, single_step=single_step),
        name=f"attention_g{group}",
        grid=(batch, stream_len // rows),
        in_specs=[
            qkv_spec(0), qkv_spec(1), qkv_spec(2),
            pl.BlockSpec(bias.shape, lambda b, m: (0, 0, 0), pipeline_mode=pl.Buffered(1)),
            pl.BlockSpec(eye.shape, lambda b, m: (0, 0), pipeline_mode=pl.Buffered(1)),
        ],
        out_specs=[
            pl.BlockSpec((None, HEAD_PAIRS, block_tokens, LANES), lambda b, m: (b, 0, m, 0)),
            stat_spec, stat_spec,
        ],
        out_shape=[
            jax.ShapeDtypeStruct((batch, HEAD_PAIRS, seq_len, LANES), F32),
            stat_shape, stat_shape,
        ],
        scratch_shapes=[
            pltpu.VMEM((dilation, 2 * Q_BLOCK, ATT_OUT_WIDTH), BF16),
            pltpu.VMEM((2, dilation, Q_BLOCK + rows, ATT_OUT_WIDTH), BF16),
        ],
        compiler_params=pltpu.CompilerParams(
            dimension_semantics=("arbitrary", "arbitrary"),
            vmem_limit_bytes=V7X_VMEM_LIMIT_BYTES),
    )(qkv, qkv, qkv, bias, eye)


def _merge_ffn_kernel(x_ref, o0_ref, o1_ref, o2_ref, m0_ref, m1_ref, m2_ref, l0_ref, l1_ref, l2_ref,
                      pool_ref, gate_ref, spread_ref,
                      wab_ref, wpb_ref, wout_ref, gffn_ref, wg_ref, wu_ref, wd_ref, gfin_ref,
                      out_ref, att_ref):
    d_model = x_ref.shape[1]
    o_refs = (o0_ref, o1_ref, o2_ref)
    m_refs = (m0_ref, m1_ref, m2_ref)
    l_refs = (l0_ref, l1_ref, l2_ref)
    lane = lax.broadcasted_iota(jnp.int32, (FFN_SUB_ROWS, LANES), 1)
    stat_lanes = (lane % HEAD_DIM) < HEAD_PAIRS

    for sub in range(x_ref.shape[0] // FFN_SUB_ROWS):
        rows = slice(sub * FFN_SUB_ROWS, (sub + 1) * FFN_SUB_ROWS)

        ms = [ref[rows, :] for ref in m_refs]
        ls = [ref[rows, :] for ref in l_refs]
        top = jnp.maximum(jnp.maximum(ms[0], ms[1]), ms[2])
        es = [jnp.exp2(m - top) for m in ms]
        den = es[0] * ls[0] + es[1] * ls[1] + es[2] * ls[2]

        packed = jnp.zeros((FFN_SUB_ROWS, LANES), F32)
        for g in range(N_ATT_GROUPS):
            rest = jnp.where(stat_lanes, es[g] / den, 0.0)
            for piece in range(SPLIT_PIECES):
                part = rest.astype(BF16).astype(F32)
                rest = rest - part
                shift = _piece_shift(g, piece)
                packed = packed + (pltpu.roll(part, shift, axis=1) if shift else part)
        spread = jnp.dot(packed.astype(BF16), spread_ref[...], preferred_element_type=F32)

        for pair in range(HEAD_PAIRS):
            acc = jnp.zeros((FFN_SUB_ROWS, LANES), F32)
            for g in range(N_ATT_GROUPS):
                col = g * ATT_OUT_WIDTH + pair * LANES
                acc = acc + spread[:, col:col + LANES] * o_refs[g][pair, rows, :]
            att_ref[rows, pair * LANES:(pair + 1) * LANES] = acc.astype(BF16)

        y_att = jnp.dot(att_ref[rows, :], wab_ref[...], preferred_element_type=F32)
        y_pool = jnp.dot(pool_ref[rows, :], wpb_ref[...], preferred_element_type=F32)
        merged = (jax.nn.sigmoid(gate_ref[rows, 0:d_model]) * y_att
                  + jax.nn.sigmoid(gate_ref[rows, d_model:2 * d_model]) * y_pool)
        h = x_ref[rows, :] + jnp.dot(merged.astype(BF16), wout_ref[...],
                                     preferred_element_type=F32)

        f = _rms_norm(h, gffn_ref[...]).astype(BF16)
        ffn = jnp.zeros((FFN_SUB_ROWS, d_model), F32)
        for lo, hi in FFN_CHUNKS:
            gate = jnp.dot(f, wg_ref[:, lo:hi], preferred_element_type=F32)
            up = jnp.dot(f, wu_ref[:, lo:hi], preferred_element_type=F32)
            hidden = (jax.nn.silu(gate) * up).astype(BF16)
            ffn = ffn + jnp.dot(hidden, wd_ref[lo:hi, :], preferred_element_type=F32)
        out_ref[rows, :] = _rms_norm(h + ffn, gfin_ref[...])


def _merge_ffn(x2, outs, maxes, sums, pool_feat, gates, w_att_branch, w_pool_branch, w_out, norm_ffn,
               w_gate, w_up, w_down, norm_final, seq_len):
    tokens, d_model = x2.shape
    rows = FFN_ROWS
    tiles_per_seq = seq_len // rows

    def tile(width):
        return pl.BlockSpec((rows, width), lambda i: (i, 0))

    def whole(arr):
        return pl.BlockSpec(arr.shape, lambda i: (0, 0), pipeline_mode=pl.Buffered(1))

    o_spec = pl.BlockSpec((None, HEAD_PAIRS, rows, LANES),
                          lambda i: (i // tiles_per_seq, 0, i % tiles_per_seq, 0))
    stat_spec = pl.BlockSpec((None, rows, LANES),
                             lambda i: (i // tiles_per_seq, i % tiles_per_seq, 0))
    weights = (_spread_matrix(), w_att_branch, w_pool_branch, w_out, norm_ffn, w_gate, w_up,
               w_down, norm_final)
    return pl.pallas_call(
        _merge_ffn_kernel,
        name="merge_ffn",
        grid=(tokens // rows,),
        in_specs=[tile(d_model)]
        + [o_spec] * N_ATT_GROUPS
        + [stat_spec] * (2 * N_ATT_GROUPS)
        + [tile(POOL_WIDTH), tile(gates.shape[1])]
        + [whole(w) for w in weights],
        out_specs=tile(d_model),
        out_shape=jax.ShapeDtypeStruct((tokens, d_model), F32),
        scratch_shapes=[pltpu.VMEM((rows, ATT_OUT_WIDTH), BF16)],
        compiler_params=pltpu.CompilerParams(
            dimension_semantics=("arbitrary",), vmem_limit_bytes=V7X_VMEM_LIMIT_BYTES),
    )(x2, *outs, *maxes, *sums, pool_feat, gates, *weights)


def _rope_tables(seq_len):
    inv_freq = ROPE_THETA ** (-np.arange(0, ROT_DIM, 2, dtype=np.float64) / ROT_DIM)
    ang = np.arange(seq_len, dtype=np.float64)[:, None] * inv_freq[None, :]
    cos, sin = np.cos(ang), np.sin(ang)
    zeros = np.zeros((seq_len, HEAD_DIM - ROT_DIM))
    half0 = np.zeros((seq_len, ROT_HALF))
    c_head = np.concatenate([cos, cos, zeros + 1.0], axis=1)
    lo_head = np.concatenate([-sin, half0, zeros], axis=1)
    hi_head = np.concatenate([half0, sin, zeros], axis=1)
    tab = np.stack([c_head, lo_head, hi_head])
    return jnp.asarray(np.concatenate([tab] * (LANES // HEAD_DIM), axis=2), dtype=F32)


def _spread_matrix():
    row = jnp.arange(LANES)[:, None]
    col = jnp.arange(N_ATT_GROUPS * ATT_OUT_WIDTH)[None, :]
    group = col // ATT_OUT_WIDTH
    head = (col % ATT_OUT_WIDTH) // HEAD_DIM
    stat = head // 2 + jnp.where(head % 2 == 0, HEAD_DIM, 0)
    hit = jnp.zeros((LANES, N_ATT_GROUPS * ATT_OUT_WIDTH), jnp.bool_)
    for piece in range(SPLIT_PIECES):
        hit = hit | (row == stat + HEAD_PAIRS * (SPLIT_PIECES * group + piece))
    return hit.astype(BF16)


def _band_bias():
    kb = jnp.arange(2 * Q_BLOCK)[:, None]
    qi = jnp.arange(Q_BLOCK)[None, :]
    dist = qi + Q_BLOCK - kb
    valid = (dist >= 0) & (dist <= WINDOW_KEYS)
    both = jnp.stack([valid, valid & (kb >= Q_BLOCK)])
    return jnp.where(both, 0.0, MASKED).astype(BF16)


def _stacked_identity():
    row = jnp.arange(2 * Q_BLOCK)[:, None]
    col = jnp.arange(LANES)[None, :]
    return (row % Q_BLOCK == col).astype(BF16)


def kernel(x, norm_mix, w_in, w_pool_group, pool_scale, w_att_branch, w_pool_branch, w_out,
           norm_ffn, w_ffn_gate, w_ffn_up, w_ffn_down, norm_final):
    batch, seq_len, d_model = x.shape
    depth = w_in.shape[0]
    tokens = batch * seq_len
    assert depth == 1, "stacked layers need the un-normalised residual between layers"
    bias = _band_bias()

    h = x.reshape(tokens, d_model)
    qkv0, qkv1, qkv2, pool_feat, gates = _in_proj(
        h, norm_mix[0][None, :], w_in[0].astype(BF16), w_pool_group[0].astype(BF16),
        pool_scale[0][None, :], _rope_tables(seq_len), batch, seq_len)
    qkv_groups = (qkv0.reshape(batch, 1, seq_len, GROUP_QKV_WIDTH), qkv1, qkv2)
    eye = _stacked_identity()
    att = [_attention_group(qkv_groups[g], bias, eye, batch, seq_len, g)
           for g in range(N_ATT_GROUPS)]
    out = _merge_ffn(
        h, [a[0] for a in att], [a[1] for a in att], [a[2] for a in att], pool_feat, gates,
        w_att_branch[0].astype(BF16), w_pool_branch[0].astype(BF16), w_out[0].astype(BF16),
        norm_ffn[0][None, :], w_ffn_gate[0].astype(BF16), w_ffn_up[0].astype(BF16),
        w_ffn_down[0].astype(BF16), norm_final[None, :], seq_len)
    return out.reshape(batch, seq_len, d_model)
```

```python
import functools
import math

import jax
import jax.numpy as jnp
import numpy as np
from jax import lax
from jax.experimental import pallas as pl
from jax.experimental.pallas import tpu as pltpu

F32 = jnp.float32
BF16 = jnp.bfloat16

HEAD_DIM = 64
HEADS_PER_GROUP = 8
ATT_GROUPS = ((128, 1), (512, 4), (2048, 16))
N_ATT_GROUPS = len(ATT_GROUPS)
ATT_OUT_WIDTH = HEADS_PER_GROUP * HEAD_DIM
ATT_WIDTH = N_ATT_GROUPS * ATT_OUT_WIDTH
QKV_WIDTH = 3 * ATT_WIDTH
GROUP_QKV_WIDTH = 3 * ATT_OUT_WIDTH
ROT_DIM = HEAD_DIM // 4
ROT_HALF = ROT_DIM // 2
ROPE_THETA = 500000.0
POOL_WINDOWS = (2, 4, 8, 16)
POOL_GROUP_WIDTH = 128
POOL_WIDTH = len(POOL_WINDOWS) * POOL_GROUP_WIDTH
NORM_EPS = 1e-6
WINDOW_KEYS = 128
MASKED = -3.0e38
Q_SCALE = HEAD_DIM ** -0.5 * math.log2(math.e)

LANES = 128
V7X_VMEM_BYTES = 64 * 1024 * 1024
V7X_VMEM_LIMIT_BYTES = V7X_VMEM_BYTES - 4 * 1024 * 1024

PROJ_ROWS = 512
PROJ_SUB_ROWS = 256
PROJ_COLS = 512
POOL_HALO = 16
ATT_ROWS = 2048
Q_BLOCK = 128
HEAD_PAIRS = HEADS_PER_GROUP // 2
FFN_ROWS = 512
FFN_SUB_ROWS = 256
FFN_CHUNKS = ((0, 1024), (1024, 2048), (2048, 2816))
SPLIT_PIECES = 3


def _run_staggered(chains):
    live = list(enumerate(chains))
    tick = 0
    while live:
        for item in list(live):
            index, gen = item
            if tick >= index and next(gen, StopIteration) is StopIteration:
                live.remove(item)
        tick += 1


def _rms_norm(xf, gain):
    ms = jnp.mean(xf * xf, axis=-1, keepdims=True)
    return (xf * lax.rsqrt(ms + NORM_EPS)) * gain


def _in_proj_kernel(x_ref, gain_ref, w_ref, wpg_ref, pscale_ref, rope_ref,
                    qkv0_ref, qkv1_ref, qkv2_ref, pool_ref, gate_ref,
                    u_ref, u4_ref, u16_ref, uslab_ref, u4slab_ref, tab4_ref, tab16_ref,
                    z0_ref, z1_ref, z2_ref, z3_ref, *, tiles_per_seq):
    rows, d_model = x_ref.shape
    seq_tile = pl.program_id(0) % tiles_per_seq
    n_slabs = d_model // LANES
    sub_rows = PROJ_SUB_ROWS
    n_sub = rows // sub_rows
    run4 = sub_rows // 4
    run16 = sub_rows // 16
    levels = (z0_ref, z1_ref, z2_ref, z3_ref)

    @pl.when(seq_tile == 0)
    def _():
        for ref in levels:
            ref[0:POOL_HALO, :] = jnp.zeros((POOL_HALO, POOL_WIDTH), F32)

    def project(lhs_ref, sub, col0):
        return jnp.dot(lhs_ref[sub], w_ref[:, col0:col0 + PROJ_COLS],
                       preferred_element_type=F32)

    gate0 = QKV_WIDTH + POOL_WIDTH
    for sub in range(n_sub):
        row0 = sub * sub_rows
        natural = slice(row0, row0 + sub_rows)
        uf = _rms_norm(x_ref[natural, :], gain_ref[...])
        u_ref[sub] = uf.astype(BF16)

        z0_ref[POOL_HALO + row0:POOL_HALO + row0 + sub_rows, :] = project(u_ref, sub, QKV_WIDTH)
        for c in range(gate_ref.shape[1] // PROJ_COLS):
            gate_ref[natural, c * PROJ_COLS:(c + 1) * PROJ_COLS] = (
                project(u_ref, sub, gate0 + c * PROJ_COLS))

        for s in range(n_slabs):
            uslab_ref[sub, s] = uf[:, s * LANES:(s + 1) * LANES]
        for r in range(4):
            dst = slice(r * run4, (r + 1) * run4)
            for s in range(n_slabs):
                piece = uslab_ref[sub, s, pl.ds(r, run4, stride=4), :]
                u4slab_ref[sub, s, dst, :] = piece
                u4_ref[sub, dst, s * LANES:(s + 1) * LANES] = piece.astype(BF16)
            for t in range(3):
                tab4_ref[sub, t, dst, :] = rope_ref[t, pl.ds(row0 + r, run4, stride=4), :]
        for run in range(16):
            src0 = (run // 4) * run4 + run % 4
            dst = slice(run * run16, (run + 1) * run16)
            for s in range(n_slabs):
                piece = u4slab_ref[sub, s, pl.ds(src0, run16, stride=4), :]
                u16_ref[sub, dst, s * LANES:(s + 1) * LANES] = piece.astype(BF16)
            for t in range(3):
                tab16_ref[sub, t, dst, :] = tab4_ref[sub, t, pl.ds(src0, run16, stride=4), :]

    cur = slice(POOL_HALO, POOL_HALO + rows)
    sums = []
    prev = z0_ref[cur, :]
    for lvl, ref in enumerate(levels):
        shift = 1 << lvl
        lo = lvl * POOL_GROUP_WIDTH
        shifted = ref[POOL_HALO - shift:POOL_HALO - shift + rows, lo:]
        prev = prev[:, (POOL_GROUP_WIDTH if lvl else 0):] + shifted
        sums.append(prev[:, 0:POOL_GROUP_WIDTH])
        if lvl + 1 < len(levels):
            levels[lvl + 1][cur, lo:] = prev
    pos = seq_tile * rows + lax.broadcasted_iota(jnp.int32, (rows, 1), 0)
    pooled = []
    for g, w in enumerate(POOL_WINDOWS):
        cols = slice(g * POOL_GROUP_WIDTH, (g + 1) * POOL_GROUP_WIDTH)
        count = jnp.minimum(pos + 1, w).astype(F32)
        pooled.append((sums[g] / count - z0_ref[cur, cols]).astype(BF16))
    for lvl, ref in enumerate(levels):
        lo = lvl * POOL_GROUP_WIDTH
        ref[0:POOL_HALO, lo:] = ref[rows:rows + POOL_HALO, lo:]

    def rope(xs, table):
        up = pltpu.roll(xs, LANES - ROT_HALF, axis=1)
        down = pltpu.roll(xs, ROT_HALF, axis=1)
        return xs * table(0) + up * table(1) + down * table(2)

    def store_natural(sub, col, part):
        qkv0_ref[sub * sub_rows:(sub + 1) * sub_rows, col:col + LANES] = part

    def store_by4(sub, col, part):
        for r in range(4):
            qkv1_ref[r, sub * run4:(sub + 1) * run4, col:col + LANES] = (
                part[r * run4:(r + 1) * run4])

    def store_by16(sub, col, part):
        for run in range(16):
            stream = 4 * (run % 4) + run // 4
            qkv2_ref[stream, sub * run16:(sub + 1) * run16, col:col + LANES] = (
                part[run * run16:(run + 1) * run16])

    for sub in range(n_sub):
        natural = slice(sub * sub_rows, (sub + 1) * sub_rows)
        variants = (
            (u_ref, lambda t: rope_ref[t, natural, :], store_natural),
            (u4_ref, lambda t: tab4_ref[sub, t], store_by4),
            (u16_ref, lambda t: tab16_ref[sub, t], store_by16),
        )
        for group, (lhs_ref, table, store) in enumerate(variants):
            for which in range(3):
                acc = project(lhs_ref, sub, which * ATT_WIDTH + group * ATT_OUT_WIDTH)
                for j in range(PROJ_COLS // LANES):
                    part = acc[:, j * LANES:(j + 1) * LANES]
                    if which == 0:
                        part = rope(part, table) * Q_SCALE
                    elif which == 1:
                        part = rope(part, table)
                    store(sub, which * ATT_OUT_WIDTH + j * LANES, part.astype(BF16))

    for g in range(len(POOL_WINDOWS)):
        cols = slice(g * POOL_GROUP_WIDTH, (g + 1) * POOL_GROUP_WIDTH)
        mapped = jnp.dot(pooled[g], wpg_ref[g], preferred_element_type=F32)
        pool_ref[:, cols] = (mapped * pscale_ref[:, cols]).astype(BF16)


def _in_proj(x2, gain, w_in, w_pool_group, pool_scale, rope_tab, batch, seq_len):
    tokens, d_model = x2.shape
    in_width = w_in.shape[1]
    gate_width = in_width - QKV_WIDTH - POOL_WIDTH
    rows = PROJ_ROWS
    n_sub = rows // PROJ_SUB_ROWS
    tiles_per_seq = seq_len // rows
    const = dict(pipeline_mode=pl.Buffered(1))

    def stream_spec(dilation):
        return pl.BlockSpec((None, dilation, rows // dilation, GROUP_QKV_WIDTH),
                            lambda i: (i // tiles_per_seq, 0, i % tiles_per_seq, 0))

    return pl.pallas_call(
        functools.partial(_in_proj_kernel, tiles_per_seq=tiles_per_seq),
        name="in_proj",
        grid=(tokens // rows,),
        in_specs=[
            pl.BlockSpec((rows, d_model), lambda i: (i, 0)),
            pl.BlockSpec((1, d_model), lambda i: (0, 0), **const),
            pl.BlockSpec((d_model, in_width), lambda i: (0, 0), **const),
            pl.BlockSpec(w_pool_group.shape, lambda i: (0, 0, 0), **const),
            pl.BlockSpec((1, POOL_WIDTH), lambda i: (0, 0), **const),
            pl.BlockSpec((3, rows, LANES), lambda i: (0, i % tiles_per_seq, 0)),
        ],
        out_specs=[
            pl.BlockSpec((rows, GROUP_QKV_WIDTH), lambda i: (i, 0)),
            stream_spec(4),
            stream_spec(16),
            pl.BlockSpec((rows, POOL_WIDTH), lambda i: (i, 0)),
            pl.BlockSpec((rows, gate_width), lambda i: (i, 0)),
        ],
        out_shape=[
            jax.ShapeDtypeStruct((tokens, GROUP_QKV_WIDTH), BF16),
            jax.ShapeDtypeStruct((batch, 4, seq_len // 4, GROUP_QKV_WIDTH), BF16),
            jax.ShapeDtypeStruct((batch, 16, seq_len // 16, GROUP_QKV_WIDTH), BF16),
            jax.ShapeDtypeStruct((tokens, POOL_WIDTH), BF16),
            jax.ShapeDtypeStruct((tokens, gate_width), F32),
        ],
        scratch_shapes=[pltpu.VMEM((n_sub, PROJ_SUB_ROWS, d_model), BF16)] * 3
        + [pltpu.VMEM((n_sub, d_model // LANES, PROJ_SUB_ROWS, LANES), F32)] * 2
        + [pltpu.VMEM((n_sub, 3, PROJ_SUB_ROWS, LANES), F32)] * 2
        + [pltpu.VMEM((POOL_HALO + rows, POOL_WIDTH), F32) for _ in POOL_WINDOWS],
        compiler_params=pltpu.CompilerParams(
            dimension_semantics=("arbitrary",), vmem_limit_bytes=V7X_VMEM_LIMIT_BYTES),
    )(x2, gain, w_in, w_pool_group, pool_scale, rope_tab)


def _stat_lane(head):
    return head // 2 + (HEAD_DIM if head % 2 == 0 else 0)


def _piece_shift(group, piece):
    return HEAD_PAIRS * (SPLIT_PIECES * group + piece)


def _attention_kernel(q_ref, k_ref, v_ref, bias_ref, eye_ref, o_ref, m_ref, l_ref,
                      kcarry_ref, vaug_ref,
                      *, single_step):
    dilation, rows, _ = q_ref.shape
    first = pl.program_id(1) == 0

    lane = lax.broadcasted_iota(jnp.int32, (Q_BLOCK, LANES), 1)
    low_half = lane < HEAD_DIM
    head_mask = (jnp.where(low_half, 1.0, 0.0).astype(BF16),
                 jnp.where(low_half, 0.0, 1.0).astype(BF16))
    low_rows = lax.broadcasted_iota(jnp.int32, (rows, LANES), 1) < HEAD_DIM
    one = jnp.ones((rows, LANES), BF16)
    nt_dims = (((1,), (1,)), ((), ()))
    zero_block = jnp.zeros((Q_BLOCK, ATT_OUT_WIDTH), BF16)

    def clear_carry():
        for st in range(dilation):
            kcarry_ref[st, 0:Q_BLOCK, :] = zero_block
            vaug_ref[0, st, 0:Q_BLOCK, :] = zero_block
            vaug_ref[1, st, 0:Q_BLOCK, :] = zero_block

    if single_step:
        clear_carry()
    else:
        pl.when(first)(clear_carry)

    for st in range(dilation):
        kcarry_ref[st, Q_BLOCK:2 * Q_BLOCK, :] = k_ref[st, 0:Q_BLOCK, :]
        for pair in range(HEAD_PAIRS):
            cols = slice(pair * LANES, (pair + 1) * LANES)
            v_pair = v_ref[st, :, cols]
            vaug_ref[0, st, Q_BLOCK:Q_BLOCK + rows, cols] = jnp.where(low_rows, v_pair, one)
            vaug_ref[1, st, Q_BLOCK:Q_BLOCK + rows, cols] = jnp.where(low_rows, one, v_pair)

    def scores(st, sub, pair):
        cols = slice(pair * LANES, (pair + 1) * LANES)
        q_pair = q_ref[st, sub * Q_BLOCK:(sub + 1) * Q_BLOCK, cols]
        if sub == 0:
            k_pair = kcarry_ref[st, :, cols]
            bias_idx = 1 if single_step else jnp.where(first, 1, 0)
        else:
            k_pair = k_ref[st, (sub - 1) * Q_BLOCK:(sub + 1) * Q_BLOCK, cols]
            bias_idx = 0
        q_both = jnp.concatenate([q_pair * head_mask[0], q_pair * head_mask[1]], axis=0)
        q_both = jnp.concatenate([q_both, eye_ref[...]], axis=1)
        k_aug = jnp.concatenate([k_pair, bias_ref[bias_idx]], axis=1)
        return lax.dot_general(q_both, k_aug, nt_dims, preferred_element_type=F32)

    def finish(st, sub, pair, s_both, stats):
        cols = slice(pair * LANES, (pair + 1) * LANES)
        band = slice(sub * Q_BLOCK, (sub + 2) * Q_BLOCK)
        if dilation == 1:
            tokens = pl.ds(sub * Q_BLOCK, Q_BLOCK)
        else:
            tokens = pl.ds(sub * Q_BLOCK * dilation + st, Q_BLOCK, stride=dilation)
        m_tile, l_tile = stats
        pvs = []
        for half in range(2):
            s = s_both[half * Q_BLOCK:(half + 1) * Q_BLOCK]
            m = jnp.max(s, axis=-1, keepdims=True)
            p = jnp.exp2(s - m).astype(BF16)
            pv = jnp.dot(p, vaug_ref[half, st, band, cols], preferred_element_type=F32)
            pvs.append(pv)
            at_stat = lane == _stat_lane(2 * pair + half)
            m_tile = jnp.where(at_stat, m, m_tile)
            l_tile = jnp.where(at_stat, pv, l_tile)
        o_ref[pair, tokens, :] = jnp.where(low_half, pvs[0], pvs[1])
        if pair == HEAD_PAIRS - 1:
            m_ref[tokens, :] = m_tile
            l_ref[tokens, :] = l_tile
        return m_tile, l_tile

    units = [(st, sub, pair) for st in range(dilation) for sub in range(rows // Q_BLOCK)
             for pair in range(HEAD_PAIRS)]
    ahead = scores(*units[0])
    stats = None
    for index, unit in enumerate(units):
        s_both = ahead
        if index + 1 < len(units):
            ahead = scores(*units[index + 1])
        if unit[2] == 0:
            stats = (jnp.zeros((Q_BLOCK, LANES), F32), jnp.ones((Q_BLOCK, LANES), F32))
        stats = finish(*unit, s_both, stats)

    for st in range(dilation):
        kcarry_ref[st, 0:Q_BLOCK, :] = k_ref[st, rows - Q_BLOCK:rows, :]
        for half in range(2):
            vaug_ref[half, st, 0:Q_BLOCK, :] = vaug_ref[half, st, rows:rows + Q_BLOCK, :]


def _attention_group(qkv, bias, eye, batch, seq_len, group):
    _, dilation = ATT_GROUPS[group]
    stream_len = seq_len // dilation
    rows = max(Q_BLOCK, ATT_ROWS // dilation)
    block_tokens = rows * dilation
    single_step = rows == stream_len

    def qkv_spec(which):
        return pl.BlockSpec((None, dilation, rows, ATT_OUT_WIDTH), lambda b, m: (b, 0, m, which))

    stat_spec = pl.BlockSpec((None, block_tokens, LANES), lambda b, m: (b, m, 0))
    stat_shape = jax.ShapeDtypeStruct((batch, seq_len, LANES), F32)
    return pl.pallas_call(
        functools.partial(_attention_kernel, single_step=single_step),
        name=f"attention_g{group}",
        grid=(batch, stream_len // rows),
        in_specs=[
            qkv_spec(0), qkv_spec(1), qkv_spec(2),
            pl.BlockSpec(bias.shape, lambda b, m: (0, 0, 0), pipeline_mode=pl.Buffered(1)),
            pl.BlockSpec(eye.shape, lambda b, m: (0, 0), pipeline_mode=pl.Buffered(1)),
        ],
        out_specs=[
            pl.BlockSpec((None, HEAD_PAIRS, block_tokens, LANES), lambda b, m: (b, 0, m, 0)),
            stat_spec, stat_spec,
        ],
        out_shape=[
            jax.ShapeDtypeStruct((batch, HEAD_PAIRS, seq_len, LANES), F32),
            stat_shape, stat_shape,
        ],
        scratch_shapes=[
            pltpu.VMEM((dilation, 2 * Q_BLOCK, ATT_OUT_WIDTH), BF16),
            pltpu.VMEM((2, dilation, Q_BLOCK + rows, ATT_OUT_WIDTH), BF16),
        ],
        compiler_params=pltpu.CompilerParams(
            dimension_semantics=("arbitrary", "arbitrary"),
            vmem_limit_bytes=V7X_VMEM_LIMIT_BYTES),
    )(qkv, qkv, qkv, bias, eye)


def _merge_ffn_kernel(x_ref, o0_ref, o1_ref, o2_ref, m0_ref, m1_ref, m2_ref, l0_ref, l1_ref, l2_ref,
                      pool_ref, gate_ref, spread_ref,
                      wab_ref, wpb_ref, wout_ref, gffn_ref, wg_ref, wu_ref, wd_ref, gfin_ref,
                      out_ref, att_ref):
    d_model = x_ref.shape[1]
    o_refs = (o0_ref, o1_ref, o2_ref)
    m_refs = (m0_ref, m1_ref, m2_ref)
    l_refs = (l0_ref, l1_ref, l2_ref)
    lane = lax.broadcasted_iota(jnp.int32, (FFN_SUB_ROWS, LANES), 1)
    stat_lanes = (lane % HEAD_DIM) < HEAD_PAIRS

    def chain(sub):
        rows = slice(sub * FFN_SUB_ROWS, (sub + 1) * FFN_SUB_ROWS)

        ms = [ref[rows, :] for ref in m_refs]
        ls = [ref[rows, :] for ref in l_refs]
        top = jnp.maximum(jnp.maximum(ms[0], ms[1]), ms[2])
        es = [jnp.exp2(m - top) for m in ms]
        den = es[0] * ls[0] + es[1] * ls[1] + es[2] * ls[2]

        packed = jnp.zeros((FFN_SUB_ROWS, LANES), F32)
        for g in range(N_ATT_GROUPS):
            rest = jnp.where(stat_lanes, es[g] / den, 0.0)
            for piece in range(SPLIT_PIECES):
                part = rest.astype(BF16).astype(F32)
                rest = rest - part
                shift = _piece_shift(g, piece)
                packed = packed + (pltpu.roll(part, shift, axis=1) if shift else part)
        spread = jnp.dot(packed.astype(BF16), spread_ref[...], preferred_element_type=F32)
        yield

        for pair in range(HEAD_PAIRS):
            acc = jnp.zeros((FFN_SUB_ROWS, LANES), F32)
            for g in range(N_ATT_GROUPS):
                col = g * ATT_OUT_WIDTH + pair * LANES
                acc = acc + spread[:, col:col + LANES] * o_refs[g][pair, rows, :]
            att_ref[rows, pair * LANES:(pair + 1) * LANES] = acc.astype(BF16)
        yield

        y_att = jnp.dot(att_ref[rows, :], wab_ref[...], preferred_element_type=F32)
        y_pool = jnp.dot(pool_ref[rows, :], wpb_ref[...], preferred_element_type=F32)
        yield
        merged = (jax.nn.sigmoid(gate_ref[rows, 0:d_model]) * y_att
                  + jax.nn.sigmoid(gate_ref[rows, d_model:2 * d_model]) * y_pool)
        yield
        h = x_ref[rows, :] + jnp.dot(merged.astype(BF16), wout_ref[...],
                                     preferred_element_type=F32)
        yield
        f = _rms_norm(h, gffn_ref[...]).astype(BF16)
        yield
        ffn = jnp.zeros((FFN_SUB_ROWS, d_model), F32)
        for lo, hi in FFN_CHUNKS:
            gate = jnp.dot(f, wg_ref[:, lo:hi], preferred_element_type=F32)
            up = jnp.dot(f, wu_ref[:, lo:hi], preferred_element_type=F32)
            yield
            hidden = (jax.nn.silu(gate) * up).astype(BF16)
            yield
            ffn = ffn + jnp.dot(hidden, wd_ref[lo:hi, :], preferred_element_type=F32)
            yield
        out_ref[rows, :] = _rms_norm(h + ffn, gfin_ref[...])

    _run_staggered([chain(sub) for sub in range(x_ref.shape[0] // FFN_SUB_ROWS)])


def _merge_ffn(x2, outs, maxes, sums, pool_feat, gates, w_att_branch, w_pool_branch, w_out, norm_ffn,
               w_gate, w_up, w_down, norm_final, seq_len):
    tokens, d_model = x2.shape
    rows = FFN_ROWS
    tiles_per_seq = seq_len // rows

    def tile(width):
        return pl.BlockSpec((rows, width), lambda i: (i, 0))

    def whole(arr):
        return pl.BlockSpec(arr.shape, lambda i: (0, 0), pipeline_mode=pl.Buffered(1))

    o_spec = pl.BlockSpec((None, HEAD_PAIRS, rows, LANES),
                          lambda i: (i // tiles_per_seq, 0, i % tiles_per_seq, 0))
    stat_spec = pl.BlockSpec((None, rows, LANES),
                             lambda i: (i // tiles_per_seq, i % tiles_per_seq, 0))
    weights = (_spread_matrix(), w_att_branch, w_pool_branch, w_out, norm_ffn, w_gate, w_up,
               w_down, norm_final)
    return pl.pallas_call(
        _merge_ffn_kernel,
        name="merge_ffn",
        grid=(tokens // rows,),
        in_specs=[tile(d_model)]
        + [o_spec] * N_ATT_GROUPS
        + [stat_spec] * (2 * N_ATT_GROUPS)
        + [tile(POOL_WIDTH), tile(gates.shape[1])]
        + [whole(w) for w in weights],
        out_specs=tile(d_model),
        out_shape=jax.ShapeDtypeStruct((tokens, d_model), F32),
        scratch_shapes=[pltpu.VMEM((rows, ATT_OUT_WIDTH), BF16)],
        compiler_params=pltpu.CompilerParams(
            dimension_semantics=("arbitrary",), vmem_limit_bytes=V7X_VMEM_LIMIT_BYTES),
    )(x2, *outs, *maxes, *sums, pool_feat, gates, *weights)


def _rope_tables(seq_len):
    inv_freq = ROPE_THETA ** (-np.arange(0, ROT_DIM, 2, dtype=np.float64) / ROT_DIM)
    ang = np.arange(seq_len, dtype=np.float64)[:, None] * inv_freq[None, :]
    cos, sin = np.cos(ang), np.sin(ang)
    zeros = np.zeros((seq_len, HEAD_DIM - ROT_DIM))
    half0 = np.zeros((seq_len, ROT_HALF))
    c_head = np.concatenate([cos, cos, zeros + 1.0], axis=1)
    lo_head = np.concatenate([-sin, half0, zeros], axis=1)
    hi_head = np.concatenate([half0, sin, zeros], axis=1)
    tab = np.stack([c_head, lo_head, hi_head])
    return jnp.asarray(np.concatenate([tab] * (LANES // HEAD_DIM), axis=2), dtype=F32)


def _spread_matrix():
    row = jnp.arange(LANES)[:, None]
    col = jnp.arange(N_ATT_GROUPS * ATT_OUT_WIDTH)[None, :]
    group = col // ATT_OUT_WIDTH
    head = (col % ATT_OUT_WIDTH) // HEAD_DIM
    stat = head // 2 + jnp.where(head % 2 == 0, HEAD_DIM, 0)
    hit = jnp.zeros((LANES, N_ATT_GROUPS * ATT_OUT_WIDTH), jnp.bool_)
    for piece in range(SPLIT_PIECES):
        hit = hit | (row == stat + HEAD_PAIRS * (SPLIT_PIECES * group + piece))
    return hit.astype(BF16)


def _band_bias():
    kb = jnp.arange(2 * Q_BLOCK)[:, None]
    qi = jnp.arange(Q_BLOCK)[None, :]
    dist = qi + Q_BLOCK - kb
    valid = (dist >= 0) & (dist <= WINDOW_KEYS)
    both = jnp.stack([valid, valid & (kb >= Q_BLOCK)])
    return jnp.where(both, 0.0, MASKED).astype(BF16)


def _stacked_identity():
    row = jnp.arange(2 * Q_BLOCK)[:, None]
    col = jnp.arange(LANES)[None, :]
    return (row % Q_BLOCK == col).astype(BF16)


def kernel(x, norm_mix, w_in, w_pool_group, pool_scale, w_att_branch, w_pool_branch, w_out,
           norm_ffn, w_ffn_gate, w_ffn_up, w_ffn_down, norm_final):
    batch, seq_len, d_model = x.shape
    depth = w_in.shape[0]
    tokens = batch * seq_len
    assert depth == 1, "stacked layers need the un-normalised residual between layers"
    bias = _band_bias()

    h = x.reshape(tokens, d_model)
    qkv0, qkv1, qkv2, pool_feat, gates = _in_proj(
        h, norm_mix[0][None, :], w_in[0].astype(BF16), w_pool_group[0].astype(BF16),
        pool_scale[0][None, :], _rope_tables(seq_len), batch, seq_len)
    qkv_groups = (qkv0.reshape(batch, 1, seq_len, GROUP_QKV_WIDTH), qkv1, qkv2)
    eye = _stacked_identity()
    att = [_attention_group(qkv_groups[g], bias, eye, batch, seq_len, g)
           for g in range(N_ATT_GROUPS)]
    out = _merge_ffn(
        h, [a[0] for a in att], [a[1] for a in att], [a[2] for a in att], pool_feat, gates,
        w_att_branch[0].astype(BF16), w_pool_branch[0].astype(BF16), w_out[0].astype(BF16),
        norm_ffn[0][None, :], w_ffn_gate[0].astype(BF16), w_ffn_up[0].astype(BF16),
        w_ffn_down[0].astype(BF16), norm_final[None, :], seq_len)
    return out.reshape(batch, seq_len, d_model)
```

```python
import functools
import math

import jax
import jax.numpy as jnp
import numpy as np
from jax import lax
from jax.experimental import pallas as pl
from jax.experimental.pallas import tpu as pltpu

F32 = jnp.float32
BF16 = jnp.bfloat16

HEAD_DIM = 64
HEADS_PER_GROUP = 8
ATT_GROUPS = ((128, 1), (512, 4), (2048, 16))
N_ATT_GROUPS = len(ATT_GROUPS)
ATT_OUT_WIDTH = HEADS_PER_GROUP * HEAD_DIM
ATT_WIDTH = N_ATT_GROUPS * ATT_OUT_WIDTH
QKV_WIDTH = 3 * ATT_WIDTH
GROUP_QKV_WIDTH = 3 * ATT_OUT_WIDTH
ROT_DIM = HEAD_DIM // 4
ROT_HALF = ROT_DIM // 2
ROPE_THETA = 500000.0
POOL_WINDOWS = (2, 4, 8, 16)
POOL_GROUP_WIDTH = 128
POOL_WIDTH = len(POOL_WINDOWS) * POOL_GROUP_WIDTH
NORM_EPS = 1e-6
WINDOW_KEYS = 128
MASKED = -3.0e38
Q_SCALE = HEAD_DIM ** -0.5 * math.log2(math.e)

LANES = 128
BF16_SUBLANES = 16
V7X_VMEM_BYTES = 64 * 1024 * 1024
V7X_VMEM_LIMIT_BYTES = V7X_VMEM_BYTES - 4 * 1024 * 1024

PROJ_ROWS = 512
PROJ_SUB_ROWS = 256
PROJ_COLS = 512
POOL_HALO = 16
ATT_ROWS = 2048
Q_BLOCK = 128
HEAD_PAIRS = HEADS_PER_GROUP // 2
SCORE_LOOKAHEAD = 1
FFN_ROWS = 512
FFN_SUB_ROWS = 256
FFN_CHUNKS = ((0, 1024), (1024, 2048), (2048, 2816))
SPLIT_PIECES = 3


def _run_staggered(chains):
    live = list(enumerate(chains))
    tick = 0
    while live:
        for item in list(live):
            index, gen = item
            if tick >= index and next(gen, StopIteration) is StopIteration:
                live.remove(item)
        tick += 1


def _rms_norm(xf, gain):
    ms = jnp.mean(xf * xf, axis=-1, keepdims=True)
    return (xf * lax.rsqrt(ms + NORM_EPS)) * gain


def _in_proj_kernel(x_ref, gain_ref, w_ref, wpg_ref, pscale_ref, rope_ref, *rest,
                    tiles_per_seq, n_later):
    later_f32 = rest[:n_later]
    qkv0_ref, qkv1_ref, qkv2_ref, pool_ref, gate_ref = rest[n_later:n_later + 5]
    later_bf16 = rest[n_later + 5:2 * n_later + 5]
    (u_ref, u4_ref, u16_ref, uslab_ref, u4slab_ref, tab4_ref, tab16_ref,
     z0_ref, z1_ref, z2_ref, z3_ref) = rest[2 * n_later + 5:]
    rows, d_model = x_ref.shape
    seq_tile = pl.program_id(0) % tiles_per_seq
    n_slabs = d_model // LANES
    sub_rows = PROJ_SUB_ROWS
    n_sub = rows // sub_rows
    run4 = sub_rows // 4
    run16 = sub_rows // 16
    levels = (z0_ref, z1_ref, z2_ref, z3_ref)

    @pl.when(seq_tile == 0)
    def _():
        for ref in levels:
            ref[0:POOL_HALO, :] = jnp.zeros((POOL_HALO, POOL_WIDTH), F32)

    for src_ref, dst_ref in zip(later_f32, later_bf16):
        dst_ref[...] = src_ref[...].astype(BF16)

    def project(lhs_ref, sub, col0):
        return jnp.dot(lhs_ref[sub], w_ref[:, col0:col0 + PROJ_COLS],
                       preferred_element_type=F32)

    gate0 = QKV_WIDTH + POOL_WIDTH
    for sub in range(n_sub):
        row0 = sub * sub_rows
        natural = slice(row0, row0 + sub_rows)
        uf = _rms_norm(x_ref[natural, :], gain_ref[...])
        u_ref[sub] = uf.astype(BF16)

        z0_ref[POOL_HALO + row0:POOL_HALO + row0 + sub_rows, :] = project(u_ref, sub, QKV_WIDTH)
        for c in range(gate_ref.shape[1] // PROJ_COLS):
            gate_ref[natural, c * PROJ_COLS:(c + 1) * PROJ_COLS] = (
                project(u_ref, sub, gate0 + c * PROJ_COLS))

        for s in range(n_slabs):
            uslab_ref[sub, s] = uf[:, s * LANES:(s + 1) * LANES]
        for r in range(4):
            dst = slice(r * run4, (r + 1) * run4)
            for s in range(n_slabs):
                piece = uslab_ref[sub, s, pl.ds(r, run4, stride=4), :]
                u4slab_ref[sub, s, dst, :] = piece
                u4_ref[sub, dst, s * LANES:(s + 1) * LANES] = piece.astype(BF16)
            for t in range(3):
                tab4_ref[sub, t, dst, :] = rope_ref[t, pl.ds(row0 + r, run4, stride=4), :]
        for run in range(16):
            src0 = (run // 4) * run4 + run % 4
            dst = slice(run * run16, (run + 1) * run16)
            for s in range(n_slabs):
                piece = u4slab_ref[sub, s, pl.ds(src0, run16, stride=4), :]
                u16_ref[sub, dst, s * LANES:(s + 1) * LANES] = piece.astype(BF16)
            for t in range(3):
                tab16_ref[sub, t, dst, :] = tab4_ref[sub, t, pl.ds(src0, run16, stride=4), :]

    cur = slice(POOL_HALO, POOL_HALO + rows)
    sums = []
    prev = z0_ref[cur, :]
    for lvl, ref in enumerate(levels):
        shift = 1 << lvl
        lo = lvl * POOL_GROUP_WIDTH
        shifted = ref[POOL_HALO - shift:POOL_HALO - shift + rows, lo:]
        prev = prev[:, (POOL_GROUP_WIDTH if lvl else 0):] + shifted
        sums.append(prev[:, 0:POOL_GROUP_WIDTH])
        if lvl + 1 < len(levels):
            levels[lvl + 1][cur, lo:] = prev
    pos = seq_tile * rows + lax.broadcasted_iota(jnp.int32, (rows, 1), 0)
    pooled = []
    for g, w in enumerate(POOL_WINDOWS):
        cols = slice(g * POOL_GROUP_WIDTH, (g + 1) * POOL_GROUP_WIDTH)
        count = jnp.minimum(pos + 1, w).astype(F32)
        pooled.append((sums[g] / count - z0_ref[cur, cols]).astype(BF16))
    for lvl, ref in enumerate(levels):
        lo = lvl * POOL_GROUP_WIDTH
        ref[0:POOL_HALO, lo:] = ref[rows:rows + POOL_HALO, lo:]

    def rope(xs, table):
        up = pltpu.roll(xs, LANES - ROT_HALF, axis=1)
        down = pltpu.roll(xs, ROT_HALF, axis=1)
        return xs * table(0) + up * table(1) + down * table(2)

    def store_natural(sub, col, part):
        qkv0_ref[sub * sub_rows:(sub + 1) * sub_rows, col:col + LANES] = part

    def store_by4(sub, col, part):
        for r in range(4):
            qkv1_ref[r, sub * run4:(sub + 1) * run4, col:col + LANES] = (
                part[r * run4:(r + 1) * run4])

    def store_by16(sub, col, part):
        for run in range(16):
            stream = 4 * (run % 4) + run // 4
            qkv2_ref[stream, sub * run16:(sub + 1) * run16, col:col + LANES] = (
                part[run * run16:(run + 1) * run16])

    for sub in range(n_sub):
        natural = slice(sub * sub_rows, (sub + 1) * sub_rows)
        variants = (
            (u_ref, lambda t: rope_ref[t, natural, :], store_natural),
            (u4_ref, lambda t: tab4_ref[sub, t], store_by4),
            (u16_ref, lambda t: tab16_ref[sub, t], store_by16),
        )
        for group, (lhs_ref, table, store) in enumerate(variants):
            for which in range(3):
                acc = project(lhs_ref, sub, which * ATT_WIDTH + group * ATT_OUT_WIDTH)
                for j in range(PROJ_COLS // LANES):
                    part = acc[:, j * LANES:(j + 1) * LANES]
                    if which == 0:
                        part = rope(part, table) * Q_SCALE
                    elif which == 1:
                        part = rope(part, table)
                    store(sub, which * ATT_OUT_WIDTH + j * LANES, part.astype(BF16))

    for g in range(len(POOL_WINDOWS)):
        cols = slice(g * POOL_GROUP_WIDTH, (g + 1) * POOL_GROUP_WIDTH)
        mapped = jnp.dot(pooled[g], wpg_ref[g], preferred_element_type=F32)
        pool_ref[:, cols] = (mapped * pscale_ref[:, cols]).astype(BF16)


def _chunk_rows(total_rows, steps):
    chunk = BF16_SUBLANES * pl.cdiv(pl.cdiv(total_rows, steps), BF16_SUBLANES)
    while total_rows % chunk:
        chunk += BF16_SUBLANES
    return chunk


def _in_proj(x2, gain, w_in, w_pool_group, pool_scale, rope_tab, later_weights, batch, seq_len):
    tokens, d_model = x2.shape
    in_width = w_in.shape[1]
    gate_width = in_width - QKV_WIDTH - POOL_WIDTH
    rows = PROJ_ROWS
    n_sub = rows // PROJ_SUB_ROWS
    n_steps = tokens // rows
    tiles_per_seq = seq_len // rows
    const = dict(pipeline_mode=pl.Buffered(1))

    def stream_spec(dilation):
        return pl.BlockSpec((None, dilation, rows // dilation, GROUP_QKV_WIDTH),
                            lambda i: (i // tiles_per_seq, 0, i % tiles_per_seq, 0))

    def chunk_spec(weight):
        chunk = _chunk_rows(weight.shape[0], n_steps)
        last = weight.shape[0] // chunk - 1
        return pl.BlockSpec((chunk, weight.shape[1]), lambda i: (jnp.minimum(i, last), 0))

    later_specs = [chunk_spec(w) for w in later_weights]
    return pl.pallas_call(
        functools.partial(_in_proj_kernel, tiles_per_seq=tiles_per_seq,
                          n_later=len(later_weights)),
        name="in_proj",
        grid=(n_steps,),
        in_specs=[
            pl.BlockSpec((rows, d_model), lambda i: (i, 0)),
            pl.BlockSpec((1, d_model), lambda i: (0, 0), **const),
            pl.BlockSpec((d_model, in_width), lambda i: (0, 0), **const),
            pl.BlockSpec(w_pool_group.shape, lambda i: (0, 0, 0), **const),
            pl.BlockSpec((1, POOL_WIDTH), lambda i: (0, 0), **const),
            pl.BlockSpec((3, rows, LANES), lambda i: (0, i % tiles_per_seq, 0)),
        ] + later_specs,
        out_specs=[
            pl.BlockSpec((rows, GROUP_QKV_WIDTH), lambda i: (i, 0)),
            stream_spec(4),
            stream_spec(16),
            pl.BlockSpec((rows, POOL_WIDTH), lambda i: (i, 0)),
            pl.BlockSpec((rows, gate_width), lambda i: (i, 0)),
        ] + later_specs,
        out_shape=[
            jax.ShapeDtypeStruct((tokens, GROUP_QKV_WIDTH), BF16),
            jax.ShapeDtypeStruct((batch, 4, seq_len // 4, GROUP_QKV_WIDTH), BF16),
            jax.ShapeDtypeStruct((batch, 16, seq_len // 16, GROUP_QKV_WIDTH), BF16),
            jax.ShapeDtypeStruct((tokens, POOL_WIDTH), BF16),
            jax.ShapeDtypeStruct((tokens, gate_width), F32),
        ] + [jax.ShapeDtypeStruct(w.shape, BF16) for w in later_weights],
        scratch_shapes=[pltpu.VMEM((n_sub, PROJ_SUB_ROWS, d_model), BF16)] * 3
        + [pltpu.VMEM((n_sub, d_model // LANES, PROJ_SUB_ROWS, LANES), F32)] * 2
        + [pltpu.VMEM((n_sub, 3, PROJ_SUB_ROWS, LANES), F32)] * 2
        + [pltpu.VMEM((POOL_HALO + rows, POOL_WIDTH), F32) for _ in POOL_WINDOWS],
        compiler_params=pltpu.CompilerParams(
            dimension_semantics=("arbitrary",), vmem_limit_bytes=V7X_VMEM_LIMIT_BYTES),
    )(x2, gain, w_in, w_pool_group, pool_scale, rope_tab, *later_weights)


def _stat_lane(head):
    return head // 2 + (HEAD_DIM if head % 2 == 0 else 0)


def _piece_shift(group, piece):
    return HEAD_PAIRS * (SPLIT_PIECES * group + piece)


def _attention_kernel(q_ref, k_ref, v_ref, bias_ref, eye_ref, o_ref, m_ref, l_ref,
                      kcarry_ref, vaug_ref,
                      *, single_step):
    dilation, rows, _ = q_ref.shape
    first = pl.program_id(1) == 0

    lane = lax.broadcasted_iota(jnp.int32, (Q_BLOCK, LANES), 1)
    low_half = lane < HEAD_DIM
    head_mask = (jnp.where(low_half, 1.0, 0.0).astype(BF16),
                 jnp.where(low_half, 0.0, 1.0).astype(BF16))
    low_rows = lax.broadcasted_iota(jnp.int32, (rows, LANES), 1) < HEAD_DIM
    one = jnp.ones((rows, LANES), BF16)
    nt_dims = (((1,), (1,)), ((), ()))
    zero_block = jnp.zeros((Q_BLOCK, ATT_OUT_WIDTH), BF16)

    def clear_carry():
        for st in range(dilation):
            kcarry_ref[st, 0:Q_BLOCK, :] = zero_block
            vaug_ref[0, st, 0:Q_BLOCK, :] = zero_block
            vaug_ref[1, st, 0:Q_BLOCK, :] = zero_block

    if single_step:
        clear_carry()
    else:
        pl.when(first)(clear_carry)

    for st in range(dilation):
        kcarry_ref[st, Q_BLOCK:2 * Q_BLOCK, :] = k_ref[st, 0:Q_BLOCK, :]
        for pair in range(HEAD_PAIRS):
            cols = slice(pair * LANES, (pair + 1) * LANES)
            v_pair = v_ref[st, :, cols]
            vaug_ref[0, st, Q_BLOCK:Q_BLOCK + rows, cols] = jnp.where(low_rows, v_pair, one)
            vaug_ref[1, st, Q_BLOCK:Q_BLOCK + rows, cols] = jnp.where(low_rows, one, v_pair)

    def scores(st, sub, pair):
        cols = slice(pair * LANES, (pair + 1) * LANES)
        q_pair = q_ref[st, sub * Q_BLOCK:(sub + 1) * Q_BLOCK, cols]
        if sub == 0:
            k_pair = kcarry_ref[st, :, cols]
            bias_idx = 1 if single_step else jnp.where(first, 1, 0)
        else:
            k_pair = k_ref[st, (sub - 1) * Q_BLOCK:(sub + 1) * Q_BLOCK, cols]
            bias_idx = 0
        q_both = jnp.concatenate([q_pair * head_mask[0], q_pair * head_mask[1]], axis=0)
        q_both = jnp.concatenate([q_both, eye_ref[...]], axis=1)
        k_aug = jnp.concatenate([k_pair, bias_ref[bias_idx]], axis=1)
        return lax.dot_general(q_both, k_aug, nt_dims, preferred_element_type=F32)

    def finish(st, sub, pair, s_both, stats):
        cols = slice(pair * LANES, (pair + 1) * LANES)
        band = slice(sub * Q_BLOCK, (sub + 2) * Q_BLOCK)
        if dilation == 1:
            tokens = pl.ds(sub * Q_BLOCK, Q_BLOCK)
        else:
            tokens = pl.ds(sub * Q_BLOCK * dilation + st, Q_BLOCK, stride=dilation)
        m_tile, l_tile = stats
        pvs = []
        for half in range(2):
            s = s_both[half * Q_BLOCK:(half + 1) * Q_BLOCK]
            m = jnp.max(s, axis=-1, keepdims=True)
            p = jnp.exp2(s - m).astype(BF16)
            pv = jnp.dot(p, vaug_ref[half, st, band, cols], preferred_element_type=F32)
            pvs.append(pv)
            at_stat = lane == _stat_lane(2 * pair + half)
            m_tile = jnp.where(at_stat, m, m_tile)
            l_tile = jnp.where(at_stat, pv, l_tile)
        o_ref[pair, tokens, :] = jnp.where(low_half, pvs[0], pvs[1])
        if pair == HEAD_PAIRS - 1:
            m_ref[tokens, :] = m_tile
            l_ref[tokens, :] = l_tile
        return m_tile, l_tile

    units = [(st, sub, pair) for st in range(dilation) for sub in range(rows // Q_BLOCK)
             for pair in range(HEAD_PAIRS)]
    ahead = [scores(*unit) for unit in units[:SCORE_LOOKAHEAD]]
    stats = None
    for index, unit in enumerate(units):
        s_both = ahead.pop(0)
        if index + SCORE_LOOKAHEAD < len(units):
            ahead.append(scores(*units[index + SCORE_LOOKAHEAD]))
        if unit[2] == 0:
            stats = (jnp.zeros((Q_BLOCK, LANES), F32), jnp.ones((Q_BLOCK, LANES), F32))
        stats = finish(*unit, s_both, stats)

    for st in range(dilation):
        kcarry_ref[st, 0:Q_BLOCK, :] = k_ref[st, rows - Q_BLOCK:rows, :]
        for half in range(2):
            vaug_ref[half, st, 0:Q_BLOCK, :] = vaug_ref[half, st, rows:rows + Q_BLOCK, :]


def _attention_group(qkv, bias, eye, batch, seq_len, group):
    _, dilation = ATT_GROUPS[group]
    stream_len = seq_len // dilation
    rows = max(Q_BLOCK, ATT_ROWS // dilation)
    block_tokens = rows * dilation
    single_step = rows == stream_len

    def qkv_spec(which):
        return pl.BlockSpec((None, dilation, rows, ATT_OUT_WIDTH), lambda b, m: (b, 0, m, which))

    stat_spec = pl.BlockSpec((None, block_tokens, LANES), lambda b, m: (b, m, 0))
    stat_shape = jax.ShapeDtypeStruct((batch, seq_len, LANES), F32)
    return pl.pallas_call(
        functools.partial(_attention_kernel, single_step=single_step),
        name=f"attention_g{group}",
        grid=(batch, stream_len // rows),
        in_specs=[
            qkv_spec(0), qkv_spec(1), qkv_spec(2),
            pl.BlockSpec(bias.shape, lambda b, m: (0, 0, 0), pipeline_mode=pl.Buffered(1)),
            pl.BlockSpec(eye.shape, lambda b, m: (0, 0), pipeline_mode=pl.Buffered(1)),
        ],
        out_specs=[
            pl.BlockSpec((None, HEAD_PAIRS, block_tokens, LANES), lambda b, m: (b, 0, m, 0)),
            stat_spec, stat_spec,
        ],
        out_shape=[
            jax.ShapeDtypeStruct((batch, HEAD_PAIRS, seq_len, LANES), F32),
            stat_shape, stat_shape,
        ],
        scratch_shapes=[
            pltpu.VMEM((dilation, 2 * Q_BLOCK, ATT_OUT_WIDTH), BF16),
            pltpu.VMEM((2, dilation, Q_BLOCK + rows, ATT_OUT_WIDTH), BF16),
        ],
        compiler_params=pltpu.CompilerParams(
            dimension_semantics=("arbitrary", "arbitrary"),
            vmem_limit_bytes=V7X_VMEM_LIMIT_BYTES),
    )(qkv, qkv, qkv, bias, eye)


def _merge_ffn_kernel(x_ref, o0_ref, o1_ref, o2_ref, m0_ref, m1_ref, m2_ref, l0_ref, l1_ref, l2_ref,
                      pool_ref, gate_ref, spread_ref,
                      wab_ref, wpb_ref, wout_ref, gffn_ref, wg_ref, wu_ref, wd_ref, gfin_ref,
                      out_ref, att_ref):
    d_model = x_ref.shape[1]
    o_refs = (o0_ref, o1_ref, o2_ref)
    m_refs = (m0_ref, m1_ref, m2_ref)
    l_refs = (l0_ref, l1_ref, l2_ref)
    lane = lax.broadcasted_iota(jnp.int32, (FFN_SUB_ROWS, LANES), 1)
    stat_lanes = (lane % HEAD_DIM) < HEAD_PAIRS

    def chain(sub):
        rows = slice(sub * FFN_SUB_ROWS, (sub + 1) * FFN_SUB_ROWS)

        ms = [ref[rows, :] for ref in m_refs]
        ls = [ref[rows, :] for ref in l_refs]
        top = jnp.maximum(jnp.maximum(ms[0], ms[1]), ms[2])
        es = [jnp.exp2(m - top) for m in ms]
        den = es[0] * ls[0] + es[1] * ls[1] + es[2] * ls[2]

        packed = jnp.zeros((FFN_SUB_ROWS, LANES), F32)
        for g in range(N_ATT_GROUPS):
            rest = jnp.where(stat_lanes, es[g] / den, 0.0)
            for piece in range(SPLIT_PIECES):
                part = rest.astype(BF16).astype(F32)
                rest = rest - part
                shift = _piece_shift(g, piece)
                packed = packed + (pltpu.roll(part, shift, axis=1) if shift else part)
        spread = jnp.dot(packed.astype(BF16), spread_ref[...], preferred_element_type=F32)
        yield

        for pair in range(HEAD_PAIRS):
            acc = jnp.zeros((FFN_SUB_ROWS, LANES), F32)
            for g in range(N_ATT_GROUPS):
                col = g * ATT_OUT_WIDTH + pair * LANES
                acc = acc + spread[:, col:col + LANES] * o_refs[g][pair, rows, :]
            att_ref[rows, pair * LANES:(pair + 1) * LANES] = acc.astype(BF16)
        yield

        y_att = jnp.dot(att_ref[rows, :], wab_ref[...], preferred_element_type=F32)
        y_pool = jnp.dot(pool_ref[rows, :], wpb_ref[...], preferred_element_type=F32)
        yield
        merged = (jax.nn.sigmoid(gate_ref[rows, 0:d_model]) * y_att
                  + jax.nn.sigmoid(gate_ref[rows, d_model:2 * d_model]) * y_pool)
        yield
        h = x_ref[rows, :] + jnp.dot(merged.astype(BF16), wout_ref[...],
                                     preferred_element_type=F32)
        yield
        f = _rms_norm(h, gffn_ref[...]).astype(BF16)
        yield
        ffn = jnp.zeros((FFN_SUB_ROWS, d_model), F32)
        for lo, hi in FFN_CHUNKS:
            gate = jnp.dot(f, wg_ref[:, lo:hi], preferred_element_type=F32)
            up = jnp.dot(f, wu_ref[:, lo:hi], preferred_element_type=F32)
            yield
            hidden = (jax.nn.silu(gate) * up).astype(BF16)
            yield
            ffn = ffn + jnp.dot(hidden, wd_ref[lo:hi, :], preferred_element_type=F32)
            yield
        out_ref[rows, :] = _rms_norm(h + ffn, gfin_ref[...])

    _run_staggered([chain(sub) for sub in range(x_ref.shape[0] // FFN_SUB_ROWS)])


def _merge_ffn(x2, outs, maxes, sums, pool_feat, gates, w_att_branch, w_pool_branch, w_out, norm_ffn,
               w_gate, w_up, w_down, norm_final, seq_len):
    tokens, d_model = x2.shape
    rows = FFN_ROWS
    tiles_per_seq = seq_len // rows

    def tile(width):
        return pl.BlockSpec((rows, width), lambda i: (i, 0))

    def whole(arr):
        return pl.BlockSpec(arr.shape, lambda i: (0, 0), pipeline_mode=pl.Buffered(1))

    o_spec = pl.BlockSpec((None, HEAD_PAIRS, rows, LANES),
                          lambda i: (i // tiles_per_seq, 0, i % tiles_per_seq, 0))
    stat_spec = pl.BlockSpec((None, rows, LANES),
                             lambda i: (i // tiles_per_seq, i % tiles_per_seq, 0))
    weights = (_spread_matrix(), w_att_branch, w_pool_branch, w_out, norm_ffn, w_gate, w_up,
               w_down, norm_final)
    return pl.pallas_call(
        _merge_ffn_kernel,
        name="merge_ffn",
        grid=(tokens // rows,),
        in_specs=[tile(d_model)]
        + [o_spec] * N_ATT_GROUPS
        + [stat_spec] * (2 * N_ATT_GROUPS)
        + [tile(POOL_WIDTH), tile(gates.shape[1])]
        + [whole(w) for w in weights],
        out_specs=tile(d_model),
        out_shape=jax.ShapeDtypeStruct((tokens, d_model), F32),
        scratch_shapes=[pltpu.VMEM((rows, ATT_OUT_WIDTH), BF16)],
        compiler_params=pltpu.CompilerParams(
            dimension_semantics=("arbitrary",), vmem_limit_bytes=V7X_VMEM_LIMIT_BYTES),
    )(x2, *outs, *maxes, *sums, pool_feat, gates, *weights)


def _rope_tables(seq_len):
    inv_freq = ROPE_THETA ** (-np.arange(0, ROT_DIM, 2, dtype=np.float64) / ROT_DIM)
    ang = np.arange(seq_len, dtype=np.float64)[:, None] * inv_freq[None, :]
    cos, sin = np.cos(ang), np.sin(ang)
    zeros = np.zeros((seq_len, HEAD_DIM - ROT_DIM))
    half0 = np.zeros((seq_len, ROT_HALF))
    c_head = np.concatenate([cos, cos, zeros + 1.0], axis=1)
    lo_head = np.concatenate([-sin, half0, zeros], axis=1)
    hi_head = np.concatenate([half0, sin, zeros], axis=1)
    tab = np.stack([c_head, lo_head, hi_head])
    return jnp.asarray(np.concatenate([tab] * (LANES // HEAD_DIM), axis=2), dtype=F32)


def _spread_matrix():
    row = jnp.arange(LANES)[:, None]
    col = jnp.arange(N_ATT_GROUPS * ATT_OUT_WIDTH)[None, :]
    group = col // ATT_OUT_WIDTH
    head = (col % ATT_OUT_WIDTH) // HEAD_DIM
    stat = head // 2 + jnp.where(head % 2 == 0, HEAD_DIM, 0)
    hit = jnp.zeros((LANES, N_ATT_GROUPS * ATT_OUT_WIDTH), jnp.bool_)
    for piece in range(SPLIT_PIECES):
        hit = hit | (row == stat + HEAD_PAIRS * (SPLIT_PIECES * group + piece))
    return hit.astype(BF16)


def _band_bias():
    kb = jnp.arange(2 * Q_BLOCK)[:, None]
    qi = jnp.arange(Q_BLOCK)[None, :]
    dist = qi + Q_BLOCK - kb
    valid = (dist >= 0) & (dist <= WINDOW_KEYS)
    both = jnp.stack([valid, valid & (kb >= Q_BLOCK)])
    return jnp.where(both, 0.0, MASKED).astype(BF16)


def _stacked_identity():
    row = jnp.arange(2 * Q_BLOCK)[:, None]
    col = jnp.arange(LANES)[None, :]
    return (row % Q_BLOCK == col).astype(BF16)


def kernel(x, norm_mix, w_in, w_pool_group, pool_scale, w_att_branch, w_pool_branch, w_out,
           norm_ffn, w_ffn_gate, w_ffn_up, w_ffn_down, norm_final):
    batch, seq_len, d_model = x.shape
    depth = w_in.shape[0]
    tokens = batch * seq_len
    assert depth == 1, "stacked layers need the un-normalised residual between layers"
    bias = _band_bias()

    h = x.reshape(tokens, d_model)
    later_weights = (w_att_branch[0], w_pool_branch[0], w_out[0], w_ffn_gate[0], w_ffn_up[0],
                     w_ffn_down[0])
    qkv0, qkv1, qkv2, pool_feat, gates, *later_bf16 = _in_proj(
        h, norm_mix[0][None, :], w_in[0].astype(BF16), w_pool_group[0].astype(BF16),
        pool_scale[0][None, :], _rope_tables(seq_len), later_weights, batch, seq_len)
    wab, wpb, wout, wgate, wup, wdown = later_bf16
    qkv_groups = (qkv0.reshape(batch, 1, seq_len, GROUP_QKV_WIDTH), qkv1, qkv2)
    eye = _stacked_identity()
    att = [_attention_group(qkv_groups[g], bias, eye, batch, seq_len, g)
           for g in range(N_ATT_GROUPS)]
    out = _merge_ffn(
        h, [a[0] for a in att], [a[1] for a in att], [a[2] for a in att], pool_feat, gates,
        wab, wpb, wout, norm_ffn[0][None, :], wgate, wup, wdown, norm_final[None, :], seq_len)
    return out.reshape(batch, seq_len, d_model)
```

```python
import functools
import math

import jax
import jax.numpy as jnp
import numpy as np
from jax import lax
from jax.experimental import pallas as pl
from jax.experimental.pallas import tpu as pltpu

F32 = jnp.float32
BF16 = jnp.bfloat16

HEAD_DIM = 64
HEADS_PER_GROUP = 8
ATT_GROUPS = ((128, 1), (512, 4), (2048, 16))
N_ATT_GROUPS = len(ATT_GROUPS)
ATT_OUT_WIDTH = HEADS_PER_GROUP * HEAD_DIM
ATT_WIDTH = N_ATT_GROUPS * ATT_OUT_WIDTH
QKV_WIDTH = 3 * ATT_WIDTH
GROUP_QKV_WIDTH = 3 * ATT_OUT_WIDTH
ROT_DIM = HEAD_DIM // 4
ROT_HALF = ROT_DIM // 2
ROPE_THETA = 500000.0
POOL_WINDOWS = (2, 4, 8, 16)
POOL_GROUP_WIDTH = 128
POOL_WIDTH = len(POOL_WINDOWS) * POOL_GROUP_WIDTH
NORM_EPS = 1e-6
WINDOW_KEYS = 128
MASKED = -3.0e38
Q_SCALE = HEAD_DIM ** -0.5 * math.log2(math.e)

LANES = 128
BF16_SUBLANES = 16
MAX_STORE_STRIDE = 4
V7X_VMEM_BYTES = 64 * 1024 * 1024
V7X_VMEM_LIMIT_BYTES = V7X_VMEM_BYTES - 4 * 1024 * 1024

PROJ_ROWS = 512
PROJ_SUB_ROWS = 256
PROJ_COLS = ATT_OUT_WIDTH
assert PROJ_COLS == POOL_WIDTH
POOL_HALO = 16
ATT_ROWS = 2048
Q_BLOCK = 128
HEAD_PAIRS = HEADS_PER_GROUP // 2
SCORE_LOOKAHEAD = 1
FFN_ROWS = 512
FFN_SUB_ROWS = 256
FFN_CHUNKS = ((0, 1024), (1024, 2048), (2048, 2816))
SPLIT_PIECES = 3


def _run_staggered(chains):
    live = list(enumerate(chains))
    tick = 0
    while live:
        for item in list(live):
            index, gen = item
            if tick >= index and next(gen, StopIteration) is StopIteration:
                live.remove(item)
        tick += 1


def _rms_norm(xf, gain):
    ms = jnp.mean(xf * xf, axis=-1, keepdims=True)
    return (xf * lax.rsqrt(ms + NORM_EPS)) * gain


def _in_proj_kernel(x_ref, gain_ref, w_ref, wpg_ref, pscale_ref, rope_ref, *rest,
                    tiles_per_seq, n_later):
    later_f32 = rest[:n_later]
    qkv0_ref, qkv1_ref, qkv2_ref, pool_ref, gate_ref = rest[n_later:n_later + 5]
    later_bf16 = rest[n_later + 5:2 * n_later + 5]
    (u_ref, u4_ref, u16_ref, uslab_ref, u4slab_ref, tab4_ref, tab16_ref,
     z0_ref, z1_ref, z2_ref, z3_ref) = rest[2 * n_later + 5:]
    rows, d_model = x_ref.shape
    seq_tile = pl.program_id(0) % tiles_per_seq
    n_slabs = d_model // LANES
    sub_rows = PROJ_SUB_ROWS
    n_sub = rows // sub_rows
    run4 = sub_rows // 4
    run16 = sub_rows // 16
    levels = (z0_ref, z1_ref, z2_ref, z3_ref)

    @pl.when(seq_tile == 0)
    def _():
        for ref in levels:
            ref[0:POOL_HALO, :] = jnp.zeros((POOL_HALO, POOL_WIDTH), F32)

    def project(lhs_ref, sub, col0):
        return jnp.dot(lhs_ref[sub], w_ref[:, col0:col0 + PROJ_COLS],
                       preferred_element_type=F32)

    gate0 = QKV_WIDTH + POOL_WIDTH
    for sub in range(n_sub):
        row0 = sub * sub_rows
        natural = slice(row0, row0 + sub_rows)
        uf = _rms_norm(x_ref[natural, :], gain_ref[...])
        u_ref[sub] = uf.astype(BF16)

        z0_ref[POOL_HALO + row0:POOL_HALO + row0 + sub_rows, :] = project(u_ref, sub, QKV_WIDTH)
        for c in range(gate_ref.shape[1] // PROJ_COLS):
            gate_ref[natural, c * PROJ_COLS:(c + 1) * PROJ_COLS] = (
                project(u_ref, sub, gate0 + c * PROJ_COLS))

        for s in range(n_slabs):
            uslab_ref[sub, s] = uf[:, s * LANES:(s + 1) * LANES]
        for r in range(4):
            dst = slice(r * run4, (r + 1) * run4)
            for s in range(n_slabs):
                piece = uslab_ref[sub, s, pl.ds(r, run4, stride=4), :]
                u4slab_ref[sub, s, dst, :] = piece
                u4_ref[sub, dst, s * LANES:(s + 1) * LANES] = piece.astype(BF16)
            for t in range(3):
                tab4_ref[sub, t, dst, :] = rope_ref[t, pl.ds(row0 + r, run4, stride=4), :]
        for run in range(16):
            src0 = (run // 4) * run4 + run % 4
            dst = slice(run * run16, (run + 1) * run16)
            for s in range(n_slabs):
                piece = u4slab_ref[sub, s, pl.ds(src0, run16, stride=4), :]
                u16_ref[sub, dst, s * LANES:(s + 1) * LANES] = piece.astype(BF16)
            for t in range(3):
                tab16_ref[sub, t, dst, :] = tab4_ref[sub, t, pl.ds(src0, run16, stride=4), :]

    cur = slice(POOL_HALO, POOL_HALO + rows)
    sums = []
    prev = z0_ref[cur, :]
    for lvl, ref in enumerate(levels):
        shift = 1 << lvl
        lo = lvl * POOL_GROUP_WIDTH
        shifted = ref[POOL_HALO - shift:POOL_HALO - shift + rows, lo:]
        prev = prev[:, (POOL_GROUP_WIDTH if lvl else 0):] + shifted
        sums.append(prev[:, 0:POOL_GROUP_WIDTH])
        if lvl + 1 < len(levels):
            levels[lvl + 1][cur, lo:] = prev
    pos = seq_tile * rows + lax.broadcasted_iota(jnp.int32, (rows, 1), 0)
    pooled = []
    for g, w in enumerate(POOL_WINDOWS):
        cols = slice(g * POOL_GROUP_WIDTH, (g + 1) * POOL_GROUP_WIDTH)
        count = jnp.minimum(pos + 1, w).astype(F32)
        pooled.append((sums[g] / count - z0_ref[cur, cols]).astype(BF16))
    for lvl, ref in enumerate(levels):
        lo = lvl * POOL_GROUP_WIDTH
        ref[0:POOL_HALO, lo:] = ref[rows:rows + POOL_HALO, lo:]

    def rope(xs, table):
        up = pltpu.roll(xs, LANES - ROT_HALF, axis=1)
        down = pltpu.roll(xs, ROT_HALF, axis=1)
        return xs * table(0) + up * table(1) + down * table(2)

    def store_natural(sub, col, part):
        qkv0_ref[sub * sub_rows:(sub + 1) * sub_rows, col:col + LANES] = part

    def store_by4(sub, col, part):
        for r in range(4):
            qkv1_ref[r, sub * run4:(sub + 1) * run4, col:col + LANES] = (
                part[r * run4:(r + 1) * run4])

    def store_by16(sub, col, part):
        for run in range(16):
            stream = 4 * (run % 4) + run // 4
            qkv2_ref[stream, sub * run16:(sub + 1) * run16, col:col + LANES] = (
                part[run * run16:(run + 1) * run16])

    for sub in range(n_sub):
        natural = slice(sub * sub_rows, (sub + 1) * sub_rows)
        variants = (
            (u_ref, lambda t: rope_ref[t, natural, :], store_natural),
            (u4_ref, lambda t: tab4_ref[sub, t], store_by4),
            (u16_ref, lambda t: tab16_ref[sub, t], store_by16),
        )
        for group, (lhs_ref, table, store) in enumerate(variants):
            for which in range(3):
                acc = project(lhs_ref, sub, which * ATT_WIDTH + group * ATT_OUT_WIDTH)
                for j in range(PROJ_COLS // LANES):
                    part = acc[:, j * LANES:(j + 1) * LANES]
                    if which == 0:
                        part = rope(part, table) * Q_SCALE
                    elif which == 1:
                        part = rope(part, table)
                    store(sub, which * ATT_OUT_WIDTH + j * LANES, part.astype(BF16))

    for g in range(len(POOL_WINDOWS)):
        cols = slice(g * POOL_GROUP_WIDTH, (g + 1) * POOL_GROUP_WIDTH)
        mapped = jnp.dot(pooled[g], wpg_ref[g], preferred_element_type=F32)
        pool_ref[:, cols] = (mapped * pscale_ref[:, cols]).astype(BF16)

    for src_ref, dst_ref in zip(later_f32, later_bf16):
        dst_ref[...] = src_ref[...].astype(BF16)


def _chunk_rows(total_rows, steps):
    chunk = BF16_SUBLANES * pl.cdiv(pl.cdiv(total_rows, steps), BF16_SUBLANES)
    while total_rows % chunk:
        chunk += BF16_SUBLANES
    return chunk


def _in_proj(x2, gain, w_in, w_pool_group, pool_scale, rope_tab, later_weights, batch, seq_len):
    tokens, d_model = x2.shape
    in_width = w_in.shape[1]
    gate_width = in_width - QKV_WIDTH - POOL_WIDTH
    rows = PROJ_ROWS
    n_sub = rows // PROJ_SUB_ROWS
    n_steps = tokens // rows
    tiles_per_seq = seq_len // rows
    const = dict(pipeline_mode=pl.Buffered(1))

    def stream_spec(dilation):
        return pl.BlockSpec((None, dilation, rows // dilation, GROUP_QKV_WIDTH),
                            lambda i: (i // tiles_per_seq, 0, i % tiles_per_seq, 0))

    def chunk_spec(weight):
        chunk = _chunk_rows(weight.shape[0], n_steps)
        last = weight.shape[0] // chunk - 1
        return pl.BlockSpec((chunk, weight.shape[1]), lambda i: (jnp.minimum(i, last), 0))

    later_specs = [chunk_spec(w) for w in later_weights]
    return pl.pallas_call(
        functools.partial(_in_proj_kernel, tiles_per_seq=tiles_per_seq,
                          n_later=len(later_weights)),
        name="in_proj",
        grid=(n_steps,),
        in_specs=[
            pl.BlockSpec((rows, d_model), lambda i: (i, 0)),
            pl.BlockSpec((1, d_model), lambda i: (0, 0), **const),
            pl.BlockSpec((d_model, in_width), lambda i: (0, 0), **const),
            pl.BlockSpec(w_pool_group.shape, lambda i: (0, 0, 0), **const),
            pl.BlockSpec((1, POOL_WIDTH), lambda i: (0, 0), **const),
            pl.BlockSpec((3, rows, LANES), lambda i: (0, i % tiles_per_seq, 0)),
        ] + later_specs,
        out_specs=[
            pl.BlockSpec((rows, GROUP_QKV_WIDTH), lambda i: (i, 0)),
            stream_spec(4),
            stream_spec(16),
            pl.BlockSpec((rows, POOL_WIDTH), lambda i: (i, 0)),
            pl.BlockSpec((rows, gate_width), lambda i: (i, 0)),
        ] + later_specs,
        out_shape=[
            jax.ShapeDtypeStruct((tokens, GROUP_QKV_WIDTH), BF16),
            jax.ShapeDtypeStruct((batch, 4, seq_len // 4, GROUP_QKV_WIDTH), BF16),
            jax.ShapeDtypeStruct((batch, 16, seq_len // 16, GROUP_QKV_WIDTH), BF16),
            jax.ShapeDtypeStruct((tokens, POOL_WIDTH), BF16),
            jax.ShapeDtypeStruct((tokens, gate_width), F32),
        ] + [jax.ShapeDtypeStruct(w.shape, BF16) for w in later_weights],
        scratch_shapes=[pltpu.VMEM((n_sub, PROJ_SUB_ROWS, d_model), BF16)] * 3
        + [pltpu.VMEM((n_sub, d_model // LANES, PROJ_SUB_ROWS, LANES), F32)] * 2
        + [pltpu.VMEM((n_sub, 3, PROJ_SUB_ROWS, LANES), F32)] * 2
        + [pltpu.VMEM((POOL_HALO + rows, POOL_WIDTH), F32) for _ in POOL_WINDOWS],
        compiler_params=pltpu.CompilerParams(
            dimension_semantics=("arbitrary",), vmem_limit_bytes=V7X_VMEM_LIMIT_BYTES),
    )(x2, gain, w_in, w_pool_group, pool_scale, rope_tab, *later_weights)


def _stat_lane(head):
    return head // 2 + (HEAD_DIM if head % 2 == 0 else 0)


def _store_planes(dilation):
    return max(1, dilation // MAX_STORE_STRIDE)


def _multi_plane_groups():
    return [g for g, (_, dilation) in enumerate(ATT_GROUPS) if _store_planes(dilation) > 1]


def _piece_shift(group, piece):
    return HEAD_PAIRS * (SPLIT_PIECES * group + piece)


def _attention_kernel(q_ref, k_ref, v_ref, bias_ref, eye_ref, o_ref, m_ref, l_ref,
                      kcarry_ref, vaug_ref,
                      *, single_step):
    dilation, rows, _ = q_ref.shape
    first = pl.program_id(1) == 0

    lane = lax.broadcasted_iota(jnp.int32, (Q_BLOCK, LANES), 1)
    low_half = lane < HEAD_DIM
    head_mask = (jnp.where(low_half, 1.0, 0.0).astype(BF16),
                 jnp.where(low_half, 0.0, 1.0).astype(BF16))
    low_rows = lax.broadcasted_iota(jnp.int32, (rows, LANES), 1) < HEAD_DIM
    one = jnp.ones((rows, LANES), BF16)
    nt_dims = (((1,), (1,)), ((), ()))
    zero_block = jnp.zeros((Q_BLOCK, ATT_OUT_WIDTH), BF16)

    def clear_carry():
        for st in range(dilation):
            kcarry_ref[st, 0:Q_BLOCK, :] = zero_block
            vaug_ref[0, st, 0:Q_BLOCK, :] = zero_block
            vaug_ref[1, st, 0:Q_BLOCK, :] = zero_block

    if single_step:
        clear_carry()
    else:
        pl.when(first)(clear_carry)

    for st in range(dilation):
        kcarry_ref[st, Q_BLOCK:2 * Q_BLOCK, :] = k_ref[st, 0:Q_BLOCK, :]
        for pair in range(HEAD_PAIRS):
            cols = slice(pair * LANES, (pair + 1) * LANES)
            v_pair = v_ref[st, :, cols]
            vaug_ref[0, st, Q_BLOCK:Q_BLOCK + rows, cols] = jnp.where(low_rows, v_pair, one)
            vaug_ref[1, st, Q_BLOCK:Q_BLOCK + rows, cols] = jnp.where(low_rows, one, v_pair)

    def scores(st, sub, pair):
        cols = slice(pair * LANES, (pair + 1) * LANES)
        q_pair = q_ref[st, sub * Q_BLOCK:(sub + 1) * Q_BLOCK, cols]
        if sub == 0:
            k_pair = kcarry_ref[st, :, cols]
            bias_idx = 1 if single_step else jnp.where(first, 1, 0)
        else:
            k_pair = k_ref[st, (sub - 1) * Q_BLOCK:(sub + 1) * Q_BLOCK, cols]
            bias_idx = 0
        q_both = jnp.concatenate([q_pair * head_mask[0], q_pair * head_mask[1]], axis=0)
        q_both = jnp.concatenate([q_both, eye_ref[...]], axis=1)
        k_aug = jnp.concatenate([k_pair, bias_ref[bias_idx]], axis=1)
        return lax.dot_general(q_both, k_aug, nt_dims, preferred_element_type=F32)

    def finish(st, sub, pair, s_both, stats):
        cols = slice(pair * LANES, (pair + 1) * LANES)
        band = slice(sub * Q_BLOCK, (sub + 2) * Q_BLOCK)
        planes = _store_planes(dilation)
        stride = dilation // planes
        start = sub * Q_BLOCK * stride + st // planes
        tokens = pl.ds(start, Q_BLOCK, stride=stride) if stride > 1 else pl.ds(start, Q_BLOCK)
        plane = st % planes
        m_tile, l_tile = stats
        pvs = []
        for half in range(2):
            s = s_both[half * Q_BLOCK:(half + 1) * Q_BLOCK]
            m = jnp.max(s, axis=-1, keepdims=True)
            p = jnp.exp2(s - m).astype(BF16)
            pv = jnp.dot(p, vaug_ref[half, st, band, cols], preferred_element_type=F32)
            pvs.append(pv)
            at_stat = lane == _stat_lane(2 * pair + half)
            m_tile = jnp.where(at_stat, m, m_tile)
            l_tile = jnp.where(at_stat, pv, l_tile)
        o_ref[pair, plane, tokens, :] = jnp.where(low_half, pvs[0], pvs[1])
        if pair == HEAD_PAIRS - 1:
            m_ref[plane, tokens, :] = m_tile
            l_ref[plane, tokens, :] = l_tile
        return m_tile, l_tile

    units = [(st, sub, pair) for st in range(dilation) for sub in range(rows // Q_BLOCK)
             for pair in range(HEAD_PAIRS)]
    ahead = [scores(*unit) for unit in units[:SCORE_LOOKAHEAD]]
    stats = None
    for index, unit in enumerate(units):
        s_both = ahead.pop(0)
        if index + SCORE_LOOKAHEAD < len(units):
            ahead.append(scores(*units[index + SCORE_LOOKAHEAD]))
        if unit[2] == 0:
            stats = (jnp.zeros((Q_BLOCK, LANES), F32), jnp.ones((Q_BLOCK, LANES), F32))
        stats = finish(*unit, s_both, stats)

    for st in range(dilation):
        kcarry_ref[st, 0:Q_BLOCK, :] = k_ref[st, rows - Q_BLOCK:rows, :]
        for half in range(2):
            vaug_ref[half, st, 0:Q_BLOCK, :] = vaug_ref[half, st, rows:rows + Q_BLOCK, :]


def _attention_group(qkv, bias, eye, batch, seq_len, group):
    _, dilation = ATT_GROUPS[group]
    stream_len = seq_len // dilation
    rows = max(Q_BLOCK, ATT_ROWS // dilation)
    planes = _store_planes(dilation)
    plane_rows = rows * dilation // planes
    single_step = rows == stream_len

    def qkv_spec(which):
        return pl.BlockSpec((None, dilation, rows, ATT_OUT_WIDTH), lambda b, m: (b, 0, m, which))

    stat_spec = pl.BlockSpec((None, planes, plane_rows, LANES), lambda b, m: (b, 0, m, 0))
    stat_shape = jax.ShapeDtypeStruct((batch, planes, seq_len // planes, LANES), F32)
    return pl.pallas_call(
        functools.partial(_attention_kernel, single_step=single_step),
        name=f"attention_g{group}",
        grid=(batch, stream_len // rows),
        in_specs=[
            qkv_spec(0), qkv_spec(1), qkv_spec(2),
            pl.BlockSpec(bias.shape, lambda b, m: (0, 0, 0), pipeline_mode=pl.Buffered(1)),
            pl.BlockSpec(eye.shape, lambda b, m: (0, 0), pipeline_mode=pl.Buffered(1)),
        ],
        out_specs=[
            pl.BlockSpec((None, HEAD_PAIRS, planes, plane_rows, LANES),
                         lambda b, m: (b, 0, 0, m, 0)),
            stat_spec, stat_spec,
        ],
        out_shape=[
            jax.ShapeDtypeStruct((batch, HEAD_PAIRS, planes, seq_len // planes, LANES), F32),
            stat_shape, stat_shape,
        ],
        scratch_shapes=[
            pltpu.VMEM((dilation, 2 * Q_BLOCK, ATT_OUT_WIDTH), BF16),
            pltpu.VMEM((2, dilation, Q_BLOCK + rows, ATT_OUT_WIDTH), BF16),
        ],
        compiler_params=pltpu.CompilerParams(
            dimension_semantics=("arbitrary", "arbitrary"),
            vmem_limit_bytes=V7X_VMEM_LIMIT_BYTES),
    )(qkv, qkv, qkv, bias, eye)


def _merge_ffn_kernel(x_ref, o0_ref, o1_ref, o2_ref, m0_ref, m1_ref, m2_ref, l0_ref, l1_ref, l2_ref,
                      pool_ref, gate_ref, spread_ref,
                      wab_ref, wpb_ref, wout_ref, gffn_ref, wg_ref, wu_ref, wd_ref, gfin_ref,
                      out_ref, att_ref, onat_ref, snat_ref):
    d_model = x_ref.shape[1]
    o_refs = (o0_ref, o1_ref, o2_ref)
    m_refs = (m0_ref, m1_ref, m2_ref)
    l_refs = (l0_ref, l1_ref, l2_ref)
    lane = lax.broadcasted_iota(jnp.int32, (FFN_SUB_ROWS, LANES), 1)
    stat_lanes = (lane % HEAD_DIM) < HEAD_PAIRS

    slots = {g: slot for slot, g in enumerate(_multi_plane_groups())}

    def o_dst(g, pair):
        return onat_ref.at[slots[g], pair] if g in slots else None

    def stat_dst(g, which):
        return snat_ref.at[slots[g], which] if g in slots else None

    def chain(sub):
        rows = slice(sub * FFN_SUB_ROWS, (sub + 1) * FFN_SUB_ROWS)

        def in_token_order(src_ref, dst_ref):
            planes = src_ref.shape[0]
            if planes == 1:
                return src_ref[0, rows, :]
            per_plane = FFN_SUB_ROWS // planes
            for plane in range(planes):
                dst_ref[pl.ds(sub * FFN_SUB_ROWS + plane, per_plane, stride=planes), :] = (
                    src_ref[plane, sub * per_plane:(sub + 1) * per_plane, :])
            return dst_ref[rows, :]

        ms = [in_token_order(ref, stat_dst(g, 0)) for g, ref in enumerate(m_refs)]
        ls = [in_token_order(ref, stat_dst(g, 1)) for g, ref in enumerate(l_refs)]
        top = jnp.maximum(jnp.maximum(ms[0], ms[1]), ms[2])
        es = [jnp.exp2(m - top) for m in ms]
        den = es[0] * ls[0] + es[1] * ls[1] + es[2] * ls[2]

        packed = jnp.zeros((FFN_SUB_ROWS, LANES), F32)
        for g in range(N_ATT_GROUPS):
            rest = jnp.where(stat_lanes, es[g] / den, 0.0)
            for piece in range(SPLIT_PIECES):
                part = rest.astype(BF16).astype(F32)
                rest = rest - part
                shift = _piece_shift(g, piece)
                packed = packed + (pltpu.roll(part, shift, axis=1) if shift else part)
        spread = jnp.dot(packed.astype(BF16), spread_ref[...], preferred_element_type=F32)
        yield

        for pair in range(HEAD_PAIRS):
            acc = jnp.zeros((FFN_SUB_ROWS, LANES), F32)
            for g in range(N_ATT_GROUPS):
                col = g * ATT_OUT_WIDTH + pair * LANES
                o_pair = in_token_order(o_refs[g].at[pair], o_dst(g, pair))
                acc = acc + spread[:, col:col + LANES] * o_pair
            att_ref[rows, pair * LANES:(pair + 1) * LANES] = acc.astype(BF16)
        yield

        y_att = jnp.dot(att_ref[rows, :], wab_ref[...], preferred_element_type=F32)
        y_pool = jnp.dot(pool_ref[rows, :], wpb_ref[...], preferred_element_type=F32)
        yield
        merged = (jax.nn.sigmoid(gate_ref[rows, 0:d_model]) * y_att
                  + jax.nn.sigmoid(gate_ref[rows, d_model:2 * d_model]) * y_pool)
        yield
        h = x_ref[rows, :] + jnp.dot(merged.astype(BF16), wout_ref[...],
                                     preferred_element_type=F32)
        yield
        f = _rms_norm(h, gffn_ref[...]).astype(BF16)
        yield
        ffn = jnp.zeros((FFN_SUB_ROWS, d_model), F32)
        for lo, hi in FFN_CHUNKS:
            gate = jnp.dot(f, wg_ref[:, lo:hi], preferred_element_type=F32)
            up = jnp.dot(f, wu_ref[:, lo:hi], preferred_element_type=F32)
            yield
            hidden = (jax.nn.silu(gate) * up).astype(BF16)
            yield
            ffn = ffn + jnp.dot(hidden, wd_ref[lo:hi, :], preferred_element_type=F32)
            yield
        out_ref[rows, :] = _rms_norm(h + ffn, gfin_ref[...])

    _run_staggered([chain(sub) for sub in range(x_ref.shape[0] // FFN_SUB_ROWS)])


def _merge_ffn(x2, outs, maxes, sums, pool_feat, gates, w_att_branch, w_pool_branch, w_out, norm_ffn,
               w_gate, w_up, w_down, norm_final, seq_len):
    tokens, d_model = x2.shape
    rows = FFN_ROWS
    tiles_per_seq = seq_len // rows

    def tile(width):
        return pl.BlockSpec((rows, width), lambda i: (i, 0))

    def whole(arr):
        return pl.BlockSpec(arr.shape, lambda i: (0, 0), pipeline_mode=pl.Buffered(1))

    def o_spec(planes):
        return pl.BlockSpec((None, HEAD_PAIRS, planes, rows // planes, LANES),
                            lambda i: (i // tiles_per_seq, 0, 0, i % tiles_per_seq, 0))

    def stat_spec(planes):
        return pl.BlockSpec((None, planes, rows // planes, LANES),
                            lambda i: (i // tiles_per_seq, 0, i % tiles_per_seq, 0))

    group_planes = [_store_planes(dilation) for _, dilation in ATT_GROUPS]
    n_slots = max(1, len(_multi_plane_groups()))
    weights = (_spread_matrix(), w_att_branch, w_pool_branch, w_out, norm_ffn, w_gate, w_up,
               w_down, norm_final)
    return pl.pallas_call(
        _merge_ffn_kernel,
        name="merge_ffn",
        grid=(tokens // rows,),
        in_specs=[tile(d_model)]
        + [o_spec(planes) for planes in group_planes]
        + [stat_spec(planes) for planes in group_planes] * 2
        + [tile(POOL_WIDTH), tile(gates.shape[1])]
        + [whole(w) for w in weights],
        out_specs=tile(d_model),
        out_shape=jax.ShapeDtypeStruct((tokens, d_model), F32),
        scratch_shapes=[
            pltpu.VMEM((rows, ATT_OUT_WIDTH), BF16),
            pltpu.VMEM((n_slots, HEAD_PAIRS, rows, LANES), F32),
            pltpu.VMEM((n_slots, 2, rows, LANES), F32),
        ],
        compiler_params=pltpu.CompilerParams(
            dimension_semantics=("arbitrary",), vmem_limit_bytes=V7X_VMEM_LIMIT_BYTES),
    )(x2, *outs, *maxes, *sums, pool_feat, gates, *weights)


def _rope_tables(seq_len):
    inv_freq = ROPE_THETA ** (-np.arange(0, ROT_DIM, 2, dtype=np.float64) / ROT_DIM)
    ang = np.arange(seq_len, dtype=np.float64)[:, None] * inv_freq[None, :]
    cos, sin = np.cos(ang), np.sin(ang)
    zeros = np.zeros((seq_len, HEAD_DIM - ROT_DIM))
    half0 = np.zeros((seq_len, ROT_HALF))
    c_head = np.concatenate([cos, cos, zeros + 1.0], axis=1)
    lo_head = np.concatenate([-sin, half0, zeros], axis=1)
    hi_head = np.concatenate([half0, sin, zeros], axis=1)
    tab = np.stack([c_head, lo_head, hi_head])
    return jnp.asarray(np.concatenate([tab] * (LANES // HEAD_DIM), axis=2), dtype=F32)


def _spread_matrix():
    row = jnp.arange(LANES)[:, None]
    col = jnp.arange(N_ATT_GROUPS * ATT_OUT_WIDTH)[None, :]
    group = col // ATT_OUT_WIDTH
    head = (col % ATT_OUT_WIDTH) // HEAD_DIM
    stat = head // 2 + jnp.where(head % 2 == 0, HEAD_DIM, 0)
    hit = jnp.zeros((LANES, N_ATT_GROUPS * ATT_OUT_WIDTH), jnp.bool_)
    for piece in range(SPLIT_PIECES):
        hit = hit | (row == stat + HEAD_PAIRS * (SPLIT_PIECES * group + piece))
    return hit.astype(BF16)


def _band_bias():
    kb = jnp.arange(2 * Q_BLOCK)[:, None]
    qi = jnp.arange(Q_BLOCK)[None, :]
    dist = qi + Q_BLOCK - kb
    valid = (dist >= 0) & (dist <= WINDOW_KEYS)
    both = jnp.stack([valid, valid & (kb >= Q_BLOCK)])
    return jnp.where(both, 0.0, MASKED).astype(BF16)


def _stacked_identity():
    row = jnp.arange(2 * Q_BLOCK)[:, None]
    col = jnp.arange(LANES)[None, :]
    return (row % Q_BLOCK == col).astype(BF16)


def kernel(x, norm_mix, w_in, w_pool_group, pool_scale, w_att_branch, w_pool_branch, w_out,
           norm_ffn, w_ffn_gate, w_ffn_up, w_ffn_down, norm_final):
    batch, seq_len, d_model = x.shape
    depth = w_in.shape[0]
    tokens = batch * seq_len
    assert depth == 1, "stacked layers need the un-normalised residual between layers"
    bias = _band_bias()

    h = x.reshape(tokens, d_model)
    later_weights = (w_att_branch[0], w_pool_branch[0], w_out[0], w_ffn_gate[0], w_ffn_up[0],
                     w_ffn_down[0])
    qkv0, qkv1, qkv2, pool_feat, gates, *later_bf16 = _in_proj(
        h, norm_mix[0][None, :], w_in[0].astype(BF16), w_pool_group[0].astype(BF16),
        pool_scale[0][None, :], _rope_tables(seq_len), later_weights, batch, seq_len)
    wab, wpb, wout, wgate, wup, wdown = later_bf16
    qkv_groups = (qkv0.reshape(batch, 1, seq_len, GROUP_QKV_WIDTH), qkv1, qkv2)
    eye = _stacked_identity()
    att = [_attention_group(qkv_groups[g], bias, eye, batch, seq_len, g)
           for g in range(N_ATT_GROUPS)]
    out = _merge_ffn(
        h, [a[0] for a in att], [a[1] for a in att], [a[2] for a in att], pool_feat, gates,
        wab, wpb, wout, norm_ffn[0][None, :], wgate, wup, wdown, norm_final[None, :], seq_len)
    return out.reshape(batch, seq_len, d_model)
```

```python
import functools
import math

import jax
import jax.numpy as jnp
import numpy as np
from jax import lax
from jax.experimental import pallas as pl
from jax.experimental.pallas import tpu as pltpu

F32 = jnp.float32
BF16 = jnp.bfloat16

HEAD_DIM = 64
HEADS_PER_GROUP = 8
ATT_GROUPS = ((128, 1), (512, 4), (2048, 16))
N_ATT_GROUPS = len(ATT_GROUPS)
ATT_OUT_WIDTH = HEADS_PER_GROUP * HEAD_DIM
ATT_WIDTH = N_ATT_GROUPS * ATT_OUT_WIDTH
QKV_WIDTH = 3 * ATT_WIDTH
GROUP_QKV_WIDTH = 3 * ATT_OUT_WIDTH
ROT_DIM = HEAD_DIM // 4
ROT_HALF = ROT_DIM // 2
ROPE_THETA = 500000.0
POOL_WINDOWS = (2, 4, 8, 16)
POOL_GROUP_WIDTH = 128
POOL_WIDTH = len(POOL_WINDOWS) * POOL_GROUP_WIDTH
NORM_EPS = 1e-6
WINDOW_KEYS = 128
MASKED = -3.0e38
Q_SCALE = HEAD_DIM ** -0.5 * math.log2(math.e)

LANES = 128
BF16_SUBLANES = 16
MAX_STORE_STRIDE = 4
V7X_VMEM_BYTES = 64 * 1024 * 1024
V7X_VMEM_LIMIT_BYTES = V7X_VMEM_BYTES - 4 * 1024 * 1024

PROJ_ROWS = 512
PROJ_SUB_ROWS = 256
PROJ_COLS = ATT_OUT_WIDTH
assert PROJ_COLS == POOL_WIDTH
POOL_HALO = 16
ATT_ROWS = 2048
Q_BLOCK = 128
HEAD_PAIRS = HEADS_PER_GROUP // 2
SUM_LANE_OFFSET = 2 * HEAD_PAIRS
SCORE_LOOKAHEAD = 1
FFN_ROWS = 512
FFN_SUB_ROWS = 256
FFN_CHUNKS = ((0, 1024), (1024, 2048), (2048, 2816))
SPLIT_PIECES = 3


def _run_staggered(chains):
    live = list(enumerate(chains))
    tick = 0
    while live:
        for item in list(live):
            index, gen = item
            if tick >= index and next(gen, StopIteration) is StopIteration:
                live.remove(item)
        tick += 1


def _rms_norm(xf, gain):
    ms = jnp.mean(xf * xf, axis=-1, keepdims=True)
    return (xf * lax.rsqrt(ms + NORM_EPS)) * gain


def _in_proj_kernel(x_ref, gain_ref, w_ref, wpg_ref, pscale_ref, rope_ref, *rest,
                    tiles_per_seq, n_later):
    later_f32 = rest[:n_later]
    qkv0_ref, qkv1_ref, qkv2_ref, pool_ref, gate_ref = rest[n_later:n_later + 5]
    later_bf16 = rest[n_later + 5:2 * n_later + 5]
    (u_ref, u4_ref, u16_ref, uslab_ref, u4slab_ref, tab4_ref, tab16_ref,
     z0_ref, z1_ref, z2_ref, z3_ref) = rest[2 * n_later + 5:]
    rows, d_model = x_ref.shape
    seq_tile = pl.program_id(0) % tiles_per_seq
    n_slabs = d_model // LANES
    sub_rows = PROJ_SUB_ROWS
    n_sub = rows // sub_rows
    run4 = sub_rows // 4
    run16 = sub_rows // 16
    levels = (z0_ref, z1_ref, z2_ref, z3_ref)

    @pl.when(seq_tile == 0)
    def _():
        for ref in levels:
            ref[0:POOL_HALO, :] = jnp.zeros((POOL_HALO, POOL_WIDTH), F32)

    def project(lhs_ref, sub, col0):
        return jnp.dot(lhs_ref[sub], w_ref[:, col0:col0 + PROJ_COLS],
                       preferred_element_type=F32)

    gate0 = QKV_WIDTH + POOL_WIDTH
    for sub in range(n_sub):
        row0 = sub * sub_rows
        natural = slice(row0, row0 + sub_rows)
        uf = _rms_norm(x_ref[natural, :], gain_ref[...])
        u_ref[sub] = uf.astype(BF16)

        z0_ref[POOL_HALO + row0:POOL_HALO + row0 + sub_rows, :] = project(u_ref, sub, QKV_WIDTH)
        for c in range(gate_ref.shape[1] // PROJ_COLS):
            gate_ref[natural, c * PROJ_COLS:(c + 1) * PROJ_COLS] = (
                project(u_ref, sub, gate0 + c * PROJ_COLS))

        for s in range(n_slabs):
            uslab_ref[sub, s] = uf[:, s * LANES:(s + 1) * LANES]
        for r in range(4):
            dst = slice(r * run4, (r + 1) * run4)
            for s in range(n_slabs):
                piece = uslab_ref[sub, s, pl.ds(r, run4, stride=4), :]
                u4slab_ref[sub, s, dst, :] = piece
                u4_ref[sub, dst, s * LANES:(s + 1) * LANES] = piece.astype(BF16)
            for t in range(3):
                tab4_ref[sub, t, dst, :] = rope_ref[t, pl.ds(row0 + r, run4, stride=4), :]
        for run in range(16):
            src0 = (run // 4) * run4 + run % 4
            dst = slice(run * run16, (run + 1) * run16)
            for s in range(n_slabs):
                piece = u4slab_ref[sub, s, pl.ds(src0, run16, stride=4), :]
                u16_ref[sub, dst, s * LANES:(s + 1) * LANES] = piece.astype(BF16)
            for t in range(3):
                tab16_ref[sub, t, dst, :] = tab4_ref[sub, t, pl.ds(src0, run16, stride=4), :]

    cur = slice(POOL_HALO, POOL_HALO + rows)
    sums = []
    prev = z0_ref[cur, :]
    for lvl, ref in enumerate(levels):
        shift = 1 << lvl
        lo = lvl * POOL_GROUP_WIDTH
        shifted = ref[POOL_HALO - shift:POOL_HALO - shift + rows, lo:]
        prev = prev[:, (POOL_GROUP_WIDTH if lvl else 0):] + shifted
        sums.append(prev[:, 0:POOL_GROUP_WIDTH])
        if lvl + 1 < len(levels):
            levels[lvl + 1][cur, lo:] = prev
    pos = seq_tile * rows + lax.broadcasted_iota(jnp.int32, (rows, 1), 0)
    pooled = []
    for g, w in enumerate(POOL_WINDOWS):
        cols = slice(g * POOL_GROUP_WIDTH, (g + 1) * POOL_GROUP_WIDTH)
        count = jnp.minimum(pos + 1, w).astype(F32)
        pooled.append((sums[g] / count - z0_ref[cur, cols]).astype(BF16))
    for lvl, ref in enumerate(levels):
        lo = lvl * POOL_GROUP_WIDTH
        ref[0:POOL_HALO, lo:] = ref[rows:rows + POOL_HALO, lo:]

    def rope(xs, table):
        up = pltpu.roll(xs, LANES - ROT_HALF, axis=1)
        down = pltpu.roll(xs, ROT_HALF, axis=1)
        return xs * table(0) + up * table(1) + down * table(2)

    def store_natural(sub, col, part):
        qkv0_ref[sub * sub_rows:(sub + 1) * sub_rows, col:col + LANES] = part

    def store_by4(sub, col, part):
        for r in range(4):
            qkv1_ref[r, sub * run4:(sub + 1) * run4, col:col + LANES] = (
                part[r * run4:(r + 1) * run4])

    def store_by16(sub, col, part):
        for run in range(16):
            stream = 4 * (run % 4) + run // 4
            qkv2_ref[stream, sub * run16:(sub + 1) * run16, col:col + LANES] = (
                part[run * run16:(run + 1) * run16])

    for sub in range(n_sub):
        natural = slice(sub * sub_rows, (sub + 1) * sub_rows)
        variants = (
            (u_ref, lambda t: rope_ref[t, natural, :], store_natural),
            (u4_ref, lambda t: tab4_ref[sub, t], store_by4),
            (u16_ref, lambda t: tab16_ref[sub, t], store_by16),
        )
        for group, (lhs_ref, table, store) in enumerate(variants):
            for which in range(3):
                acc = project(lhs_ref, sub, which * ATT_WIDTH + group * ATT_OUT_WIDTH)
                for j in range(PROJ_COLS // LANES):
                    part = acc[:, j * LANES:(j + 1) * LANES]
                    if which == 0:
                        part = rope(part, table) * Q_SCALE
                    elif which == 1:
                        part = rope(part, table)
                    store(sub, which * ATT_OUT_WIDTH + j * LANES, part.astype(BF16))

    for g in range(len(POOL_WINDOWS)):
        cols = slice(g * POOL_GROUP_WIDTH, (g + 1) * POOL_GROUP_WIDTH)
        mapped = jnp.dot(pooled[g], wpg_ref[g], preferred_element_type=F32)
        pool_ref[:, cols] = (mapped * pscale_ref[:, cols]).astype(BF16)

    for src_ref, dst_ref in zip(later_f32, later_bf16):
        dst_ref[...] = src_ref[...].astype(BF16)


def _chunk_rows(total_rows, steps):
    chunk = BF16_SUBLANES * pl.cdiv(pl.cdiv(total_rows, steps), BF16_SUBLANES)
    while total_rows % chunk:
        chunk += BF16_SUBLANES
    return chunk


def _in_proj(x2, gain, w_in, w_pool_group, pool_scale, rope_tab, later_weights, batch, seq_len):
    tokens, d_model = x2.shape
    in_width = w_in.shape[1]
    gate_width = in_width - QKV_WIDTH - POOL_WIDTH
    rows = PROJ_ROWS
    n_sub = rows // PROJ_SUB_ROWS
    n_steps = tokens // rows
    tiles_per_seq = seq_len // rows
    const = dict(pipeline_mode=pl.Buffered(1))

    def stream_spec(dilation):
        return pl.BlockSpec((None, dilation, rows // dilation, GROUP_QKV_WIDTH),
                            lambda i: (i // tiles_per_seq, 0, i % tiles_per_seq, 0))

    def chunk_spec(weight):
        chunk = _chunk_rows(weight.shape[0], n_steps)
        last = weight.shape[0] // chunk - 1
        return pl.BlockSpec((chunk, weight.shape[1]), lambda i: (jnp.minimum(i, last), 0))

    later_specs = [chunk_spec(w) for w in later_weights]
    return pl.pallas_call(
        functools.partial(_in_proj_kernel, tiles_per_seq=tiles_per_seq,
                          n_later=len(later_weights)),
        name="in_proj",
        grid=(n_steps,),
        in_specs=[
            pl.BlockSpec((rows, d_model), lambda i: (i, 0)),
            pl.BlockSpec((1, d_model), lambda i: (0, 0), **const),
            pl.BlockSpec((d_model, in_width), lambda i: (0, 0), **const),
            pl.BlockSpec(w_pool_group.shape, lambda i: (0, 0, 0), **const),
            pl.BlockSpec((1, POOL_WIDTH), lambda i: (0, 0), **const),
            pl.BlockSpec((3, rows, LANES), lambda i: (0, i % tiles_per_seq, 0)),
        ] + later_specs,
        out_specs=[
            pl.BlockSpec((rows, GROUP_QKV_WIDTH), lambda i: (i, 0)),
            stream_spec(4),
            stream_spec(16),
            pl.BlockSpec((rows, POOL_WIDTH), lambda i: (i, 0)),
            pl.BlockSpec((rows, gate_width), lambda i: (i, 0)),
        ] + later_specs,
        out_shape=[
            jax.ShapeDtypeStruct((tokens, GROUP_QKV_WIDTH), BF16),
            jax.ShapeDtypeStruct((batch, 4, seq_len // 4, GROUP_QKV_WIDTH), BF16),
            jax.ShapeDtypeStruct((batch, 16, seq_len // 16, GROUP_QKV_WIDTH), BF16),
            jax.ShapeDtypeStruct((tokens, POOL_WIDTH), BF16),
            jax.ShapeDtypeStruct((tokens, gate_width), F32),
        ] + [jax.ShapeDtypeStruct(w.shape, BF16) for w in later_weights],
        scratch_shapes=[pltpu.VMEM((n_sub, PROJ_SUB_ROWS, d_model), BF16)] * 3
        + [pltpu.VMEM((n_sub, d_model // LANES, PROJ_SUB_ROWS, LANES), F32)] * 2
        + [pltpu.VMEM((n_sub, 3, PROJ_SUB_ROWS, LANES), F32)] * 2
        + [pltpu.VMEM((POOL_HALO + rows, POOL_WIDTH), F32) for _ in POOL_WINDOWS],
        compiler_params=pltpu.CompilerParams(
            dimension_semantics=("arbitrary",), vmem_limit_bytes=V7X_VMEM_LIMIT_BYTES),
    )(x2, gain, w_in, w_pool_group, pool_scale, rope_tab, *later_weights)


def _stat_lane(head):
    return head // 2 + (HEAD_DIM if head % 2 == 0 else 0)


def _store_planes(dilation):
    return max(1, dilation // MAX_STORE_STRIDE)


def _multi_plane_groups():
    return [g for g, (_, dilation) in enumerate(ATT_GROUPS) if _store_planes(dilation) > 1]


def _piece_shift(group, piece):
    return HEAD_PAIRS * (SPLIT_PIECES * group + piece)


def _attention_kernel(q_ref, k_ref, v_ref, bias_ref, eye_ref, o_ref, stat_ref,
                      kcarry_ref, vaug_ref,
                      *, single_step):
    dilation, rows, _ = q_ref.shape
    first = pl.program_id(1) == 0

    lane = lax.broadcasted_iota(jnp.int32, (Q_BLOCK, LANES), 1)
    low_half = lane < HEAD_DIM
    head_mask = (jnp.where(low_half, 1.0, 0.0).astype(BF16),
                 jnp.where(low_half, 0.0, 1.0).astype(BF16))
    low_rows = lax.broadcasted_iota(jnp.int32, (rows, LANES), 1) < HEAD_DIM
    one = jnp.ones((rows, LANES), BF16)
    nt_dims = (((1,), (1,)), ((), ()))
    zero_block = jnp.zeros((Q_BLOCK, ATT_OUT_WIDTH), BF16)

    def clear_carry():
        for st in range(dilation):
            kcarry_ref[st, 0:Q_BLOCK, :] = zero_block
            vaug_ref[0, st, 0:Q_BLOCK, :] = zero_block
            vaug_ref[1, st, 0:Q_BLOCK, :] = zero_block

    if single_step:
        clear_carry()
    else:
        pl.when(first)(clear_carry)

    for st in range(dilation):
        kcarry_ref[st, Q_BLOCK:2 * Q_BLOCK, :] = k_ref[st, 0:Q_BLOCK, :]
        for pair in range(HEAD_PAIRS):
            cols = slice(pair * LANES, (pair + 1) * LANES)
            v_pair = v_ref[st, :, cols]
            vaug_ref[0, st, Q_BLOCK:Q_BLOCK + rows, cols] = jnp.where(low_rows, v_pair, one)
            vaug_ref[1, st, Q_BLOCK:Q_BLOCK + rows, cols] = jnp.where(low_rows, one, v_pair)

    def scores(st, sub, pair):
        cols = slice(pair * LANES, (pair + 1) * LANES)
        q_pair = q_ref[st, sub * Q_BLOCK:(sub + 1) * Q_BLOCK, cols]
        if sub == 0:
            k_pair = kcarry_ref[st, :, cols]
            bias_idx = 1 if single_step else jnp.where(first, 1, 0)
        else:
            k_pair = k_ref[st, (sub - 1) * Q_BLOCK:(sub + 1) * Q_BLOCK, cols]
            bias_idx = 0
        q_both = jnp.concatenate([q_pair * head_mask[0], q_pair * head_mask[1]], axis=0)
        q_both = jnp.concatenate([q_both, eye_ref[...]], axis=1)
        k_aug = jnp.concatenate([k_pair, bias_ref[bias_idx]], axis=1)
        return lax.dot_general(q_both, k_aug, nt_dims, preferred_element_type=F32)

    def finish(st, sub, pair, s_both, stats):
        cols = slice(pair * LANES, (pair + 1) * LANES)
        band = slice(sub * Q_BLOCK, (sub + 2) * Q_BLOCK)
        planes = _store_planes(dilation)
        stride = dilation // planes
        start = sub * Q_BLOCK * stride + st // planes
        tokens = pl.ds(start, Q_BLOCK, stride=stride) if stride > 1 else pl.ds(start, Q_BLOCK)
        plane = st % planes
        pvs = []
        for half in range(2):
            s = s_both[half * Q_BLOCK:(half + 1) * Q_BLOCK]
            m = jnp.max(s, axis=-1, keepdims=True)
            p = jnp.exp2(s - m).astype(BF16)
            pv = jnp.dot(p, vaug_ref[half, st, band, cols], preferred_element_type=F32)
            pvs.append(pv)
            stat_lane = _stat_lane(2 * pair + half)
            stats = jnp.where(lane == stat_lane, m, stats)
            stats = jnp.where(lane == stat_lane + SUM_LANE_OFFSET, pv, stats)
        o_ref[pair, plane, tokens, :] = jnp.where(low_half, pvs[0], pvs[1])
        if pair == HEAD_PAIRS - 1:
            stat_ref[plane, tokens, :] = stats
        return stats

    units = [(st, sub, pair) for st in range(dilation) for sub in range(rows // Q_BLOCK)
             for pair in range(HEAD_PAIRS)]
    ahead = [scores(*unit) for unit in units[:SCORE_LOOKAHEAD]]
    stats = None
    for index, unit in enumerate(units):
        s_both = ahead.pop(0)
        if index + SCORE_LOOKAHEAD < len(units):
            ahead.append(scores(*units[index + SCORE_LOOKAHEAD]))
        if unit[2] == 0:
            stats = jnp.ones((Q_BLOCK, LANES), F32)
        stats = finish(*unit, s_both, stats)

    for st in range(dilation):
        kcarry_ref[st, 0:Q_BLOCK, :] = k_ref[st, rows - Q_BLOCK:rows, :]
        for half in range(2):
            vaug_ref[half, st, 0:Q_BLOCK, :] = vaug_ref[half, st, rows:rows + Q_BLOCK, :]


def _attention_group(qkv, bias, eye, batch, seq_len, group):
    _, dilation = ATT_GROUPS[group]
    stream_len = seq_len // dilation
    rows = max(Q_BLOCK, ATT_ROWS // dilation)
    planes = _store_planes(dilation)
    plane_rows = rows * dilation // planes
    single_step = rows == stream_len

    def qkv_spec(which):
        return pl.BlockSpec((None, dilation, rows, ATT_OUT_WIDTH), lambda b, m: (b, 0, m, which))

    stat_spec = pl.BlockSpec((None, planes, plane_rows, LANES), lambda b, m: (b, 0, m, 0))
    stat_shape = jax.ShapeDtypeStruct((batch, planes, seq_len // planes, LANES), F32)
    return pl.pallas_call(
        functools.partial(_attention_kernel, single_step=single_step),
        name=f"attention_g{group}",
        grid=(batch, stream_len // rows),
        in_specs=[
            qkv_spec(0), qkv_spec(1), qkv_spec(2),
            pl.BlockSpec(bias.shape, lambda b, m: (0, 0, 0), pipeline_mode=pl.Buffered(1)),
            pl.BlockSpec(eye.shape, lambda b, m: (0, 0), pipeline_mode=pl.Buffered(1)),
        ],
        out_specs=[
            pl.BlockSpec((None, HEAD_PAIRS, planes, plane_rows, LANES),
                         lambda b, m: (b, 0, 0, m, 0)),
            stat_spec,
        ],
        out_shape=[
            jax.ShapeDtypeStruct((batch, HEAD_PAIRS, planes, seq_len // planes, LANES), F32),
            stat_shape,
        ],
        scratch_shapes=[
            pltpu.VMEM((dilation, 2 * Q_BLOCK, ATT_OUT_WIDTH), BF16),
            pltpu.VMEM((2, dilation, Q_BLOCK + rows, ATT_OUT_WIDTH), BF16),
        ],
        compiler_params=pltpu.CompilerParams(
            dimension_semantics=("arbitrary", "arbitrary"),
            vmem_limit_bytes=V7X_VMEM_LIMIT_BYTES),
    )(qkv, qkv, qkv, bias, eye)


def _merge_ffn_kernel(x_ref, o0_ref, o1_ref, o2_ref, s0_ref, s1_ref, s2_ref,
                      pool_ref, gate_ref, spread_ref,
                      wab_ref, wpb_ref, wout_ref, gffn_ref, wg_ref, wu_ref, wd_ref, gfin_ref,
                      out_ref, att_ref, onat_ref, snat_ref):
    d_model = x_ref.shape[1]
    o_refs = (o0_ref, o1_ref, o2_ref)
    stat_refs = (s0_ref, s1_ref, s2_ref)
    lane = lax.broadcasted_iota(jnp.int32, (FFN_SUB_ROWS, LANES), 1)
    stat_lanes = (lane % HEAD_DIM) < HEAD_PAIRS

    slots = {g: slot for slot, g in enumerate(_multi_plane_groups())}

    def o_dst(g, pair):
        return onat_ref.at[slots[g], pair] if g in slots else None

    def stat_dst(g):
        return snat_ref.at[slots[g]] if g in slots else None

    def chain(sub):
        rows = slice(sub * FFN_SUB_ROWS, (sub + 1) * FFN_SUB_ROWS)

        def in_token_order(src_ref, dst_ref):
            planes = src_ref.shape[0]
            if planes == 1:
                return src_ref[0, rows, :]
            per_plane = FFN_SUB_ROWS // planes
            for plane in range(planes):
                dst_ref[pl.ds(sub * FFN_SUB_ROWS + plane, per_plane, stride=planes), :] = (
                    src_ref[plane, sub * per_plane:(sub + 1) * per_plane, :])
            return dst_ref[rows, :]

        ms = [in_token_order(ref, stat_dst(g)) for g, ref in enumerate(stat_refs)]
        ls = [pltpu.roll(m, LANES - SUM_LANE_OFFSET, axis=1) for m in ms]
        top = jnp.maximum(jnp.maximum(ms[0], ms[1]), ms[2])
        es = [jnp.exp2(m - top) for m in ms]
        den = es[0] * ls[0] + es[1] * ls[1] + es[2] * ls[2]

        packed = jnp.zeros((FFN_SUB_ROWS, LANES), F32)
        for g in range(N_ATT_GROUPS):
            rest = jnp.where(stat_lanes, es[g] / den, 0.0)
            for piece in range(SPLIT_PIECES):
                part = rest.astype(BF16).astype(F32)
                rest = rest - part
                shift = _piece_shift(g, piece)
                packed = packed + (pltpu.roll(part, shift, axis=1) if shift else part)
        spread = jnp.dot(packed.astype(BF16), spread_ref[...], preferred_element_type=F32)
        yield

        for pair in range(HEAD_PAIRS):
            acc = jnp.zeros((FFN_SUB_ROWS, LANES), F32)
            for g in range(N_ATT_GROUPS):
                col = g * ATT_OUT_WIDTH + pair * LANES
                o_pair = in_token_order(o_refs[g].at[pair], o_dst(g, pair))
                acc = acc + spread[:, col:col + LANES] * o_pair
            att_ref[rows, pair * LANES:(pair + 1) * LANES] = acc.astype(BF16)
        yield

        y_att = jnp.dot(att_ref[rows, :], wab_ref[...], preferred_element_type=F32)
        y_pool = jnp.dot(pool_ref[rows, :], wpb_ref[...], preferred_element_type=F32)
        yield
        merged = (jax.nn.sigmoid(gate_ref[rows, 0:d_model]) * y_att
                  + jax.nn.sigmoid(gate_ref[rows, d_model:2 * d_model]) * y_pool)
        yield
        h = x_ref[rows, :] + jnp.dot(merged.astype(BF16), wout_ref[...],
                                     preferred_element_type=F32)
        yield
        f = _rms_norm(h, gffn_ref[...]).astype(BF16)
        yield
        ffn = jnp.zeros((FFN_SUB_ROWS, d_model), F32)
        for lo, hi in FFN_CHUNKS:
            gate = jnp.dot(f, wg_ref[:, lo:hi], preferred_element_type=F32)
            up = jnp.dot(f, wu_ref[:, lo:hi], preferred_element_type=F32)
            yield
            hidden = (jax.nn.silu(gate) * up).astype(BF16)
            yield
            ffn = ffn + jnp.dot(hidden, wd_ref[lo:hi, :], preferred_element_type=F32)
            yield
        out_ref[rows, :] = _rms_norm(h + ffn, gfin_ref[...])

    _run_staggered([chain(sub) for sub in range(x_ref.shape[0] // FFN_SUB_ROWS)])


def _merge_ffn(x2, outs, stats, pool_feat, gates, w_att_branch, w_pool_branch, w_out, norm_ffn,
               w_gate, w_up, w_down, norm_final, seq_len):
    tokens, d_model = x2.shape
    rows = FFN_ROWS
    tiles_per_seq = seq_len // rows

    def tile(width):
        return pl.BlockSpec((rows, width), lambda i: (i, 0))

    def whole(arr):
        return pl.BlockSpec(arr.shape, lambda i: (0, 0), pipeline_mode=pl.Buffered(1))

    def o_spec(planes):
        return pl.BlockSpec((None, HEAD_PAIRS, planes, rows // planes, LANES),
                            lambda i: (i // tiles_per_seq, 0, 0, i % tiles_per_seq, 0))

    def stat_spec(planes):
        return pl.BlockSpec((None, planes, rows // planes, LANES),
                            lambda i: (i // tiles_per_seq, 0, i % tiles_per_seq, 0))

    group_planes = [_store_planes(dilation) for _, dilation in ATT_GROUPS]
    n_slots = max(1, len(_multi_plane_groups()))
    weights = (_spread_matrix(), w_att_branch, w_pool_branch, w_out, norm_ffn, w_gate, w_up,
               w_down, norm_final)
    return pl.pallas_call(
        _merge_ffn_kernel,
        name="merge_ffn",
        grid=(tokens // rows,),
        in_specs=[tile(d_model)]
        + [o_spec(planes) for planes in group_planes]
        + [stat_spec(planes) for planes in group_planes]
        + [tile(POOL_WIDTH), tile(gates.shape[1])]
        + [whole(w) for w in weights],
        out_specs=tile(d_model),
        out_shape=jax.ShapeDtypeStruct((tokens, d_model), F32),
        scratch_shapes=[
            pltpu.VMEM((rows, ATT_OUT_WIDTH), BF16),
            pltpu.VMEM((n_slots, HEAD_PAIRS, rows, LANES), F32),
            pltpu.VMEM((n_slots, rows, LANES), F32),
        ],
        compiler_params=pltpu.CompilerParams(
            dimension_semantics=("arbitrary",), vmem_limit_bytes=V7X_VMEM_LIMIT_BYTES),
    )(x2, *outs, *stats, pool_feat, gates, *weights)


def _rope_tables(seq_len):
    inv_freq = ROPE_THETA ** (-np.arange(0, ROT_DIM, 2, dtype=np.float64) / ROT_DIM)
    ang = np.arange(seq_len, dtype=np.float64)[:, None] * inv_freq[None, :]
    cos, sin = np.cos(ang), np.sin(ang)
    zeros = np.zeros((seq_len, HEAD_DIM - ROT_DIM))
    half0 = np.zeros((seq_len, ROT_HALF))
    c_head = np.concatenate([cos, cos, zeros + 1.0], axis=1)
    lo_head = np.concatenate([-sin, half0, zeros], axis=1)
    hi_head = np.concatenate([half0, sin, zeros], axis=1)
    tab = np.stack([c_head, lo_head, hi_head])
    return jnp.asarray(np.concatenate([tab] * (LANES // HEAD_DIM), axis=2), dtype=F32)


def _spread_matrix():
    row = jnp.arange(LANES)[:, None]
    col = jnp.arange(N_ATT_GROUPS * ATT_OUT_WIDTH)[None, :]
    group = col // ATT_OUT_WIDTH
    head = (col % ATT_OUT_WIDTH) // HEAD_DIM
    stat = head // 2 + jnp.where(head % 2 == 0, HEAD_DIM, 0)
    hit = jnp.zeros((LANES, N_ATT_GROUPS * ATT_OUT_WIDTH), jnp.bool_)
    for piece in range(SPLIT_PIECES):
        hit = hit | (row == stat + HEAD_PAIRS * (SPLIT_PIECES * group + piece))
    return hit.astype(BF16)


def _band_bias():
    kb = jnp.arange(2 * Q_BLOCK)[:, None]
    qi = jnp.arange(Q_BLOCK)[None, :]
    dist = qi + Q_BLOCK - kb
    valid = (dist >= 0) & (dist <= WINDOW_KEYS)
    both = jnp.stack([valid, valid & (kb >= Q_BLOCK)])
    return jnp.where(both, 0.0, MASKED).astype(BF16)


def _stacked_identity():
    row = jnp.arange(2 * Q_BLOCK)[:, None]
    col = jnp.arange(LANES)[None, :]
    return (row % Q_BLOCK == col).astype(BF16)


def kernel(x, norm_mix, w_in, w_pool_group, pool_scale, w_att_branch, w_pool_branch, w_out,
           norm_ffn, w_ffn_gate, w_ffn_up, w_ffn_down, norm_final):
    batch, seq_len, d_model = x.shape
    depth = w_in.shape[0]
    tokens = batch * seq_len
    assert depth == 1, "stacked layers need the un-normalised residual between layers"
    bias = _band_bias()

    h = x.reshape(tokens, d_model)
    later_weights = (w_att_branch[0], w_pool_branch[0], w_out[0], w_ffn_gate[0], w_ffn_up[0],
                     w_ffn_down[0])
    qkv0, qkv1, qkv2, pool_feat, gates, *later_bf16 = _in_proj(
        h, norm_mix[0][None, :], w_in[0].astype(BF16), w_pool_group[0].astype(BF16),
        pool_scale[0][None, :], _rope_tables(seq_len), later_weights, batch, seq_len)
    wab, wpb, wout, wgate, wup, wdown = later_bf16
    qkv_groups = (qkv0.reshape(batch, 1, seq_len, GROUP_QKV_WIDTH), qkv1, qkv2)
    eye = _stacked_identity()
    att = [_attention_group(qkv_groups[g], bias, eye, batch, seq_len, g)
           for g in range(N_ATT_GROUPS)]
    out = _merge_ffn(
        h, [a[0] for a in att], [a[1] for a in att], pool_feat, gates,
        wab, wpb, wout, norm_ffn[0][None, :], wgate, wup, wdown, norm_final[None, :], seq_len)
    return out.reshape(batch, seq_len, d_model)
```

```python
import functools
import math

import jax
import jax.numpy as jnp
import numpy as np
from jax import lax
from jax.experimental import pallas as pl
from jax.experimental.pallas import tpu as pltpu

F32 = jnp.float32
BF16 = jnp.bfloat16

HEAD_DIM = 64
HEADS_PER_GROUP = 8
ATT_GROUPS = ((128, 1), (512, 4), (2048, 16))
N_ATT_GROUPS = len(ATT_GROUPS)
ATT_OUT_WIDTH = HEADS_PER_GROUP * HEAD_DIM
ATT_WIDTH = N_ATT_GROUPS * ATT_OUT_WIDTH
QKV_WIDTH = 3 * ATT_WIDTH
GROUP_QKV_WIDTH = 3 * ATT_OUT_WIDTH
ROT_DIM = HEAD_DIM // 4
ROT_HALF = ROT_DIM // 2
ROPE_THETA = 500000.0
POOL_WINDOWS = (2, 4, 8, 16)
POOL_GROUP_WIDTH = 128
POOL_WIDTH = len(POOL_WINDOWS) * POOL_GROUP_WIDTH
NORM_EPS = 1e-6
WINDOW_KEYS = 128
MASKED = -3.0e38
Q_SCALE = HEAD_DIM ** -0.5 * math.log2(math.e)

LANES = 128
BF16_SUBLANES = 16
MAX_STORE_STRIDE = 4
V7X_VMEM_BYTES = 64 * 1024 * 1024
V7X_VMEM_LIMIT_BYTES = V7X_VMEM_BYTES - 4 * 1024 * 1024

PROJ_ROWS = 512
PROJ_SUB_ROWS = 256
PROJ_COLS = ATT_OUT_WIDTH
assert PROJ_COLS == POOL_WIDTH
POOL_HALO = 16
ATT_ROWS = 2048
Q_BLOCK = 128
HEAD_PAIRS = HEADS_PER_GROUP // 2
SUM_LANE_OFFSET = 2 * HEAD_PAIRS
SCORE_LOOKAHEAD = 1
FFN_ROWS = 512
FFN_SUB_ROWS = 256
FFN_CHUNKS = ((0, 1024), (1024, 2048), (2048, 2816))
SPLIT_PIECES = 3


def _run_staggered(chains):
    live = list(enumerate(chains))
    tick = 0
    while live:
        for item in list(live):
            index, gen = item
            if tick >= index and next(gen, StopIteration) is StopIteration:
                live.remove(item)
        tick += 1


def _rms_norm(xf, gain):
    ms = jnp.mean(xf * xf, axis=-1, keepdims=True)
    return (xf * lax.rsqrt(ms + NORM_EPS)) * gain


def _in_proj_kernel(x_ref, gain_ref, w_ref, wpg_ref, pscale_ref, rope_ref, *rest,
                    tiles_per_seq, n_later):
    later_f32 = rest[:n_later]
    qkv0_ref, qkv1_ref, qkv2_ref, pool_ref, gate_ref = rest[n_later:n_later + 5]
    later_bf16 = rest[n_later + 5:2 * n_later + 5]
    (u_ref, u4_ref, u16_ref, uslab_ref, u4slab_ref, tab4_ref, tab16_ref,
     z0_ref, z1_ref, z2_ref, z3_ref) = rest[2 * n_later + 5:]
    rows, d_model = x_ref.shape
    seq_tile = pl.program_id(0) % tiles_per_seq
    n_slabs = d_model // LANES
    sub_rows = PROJ_SUB_ROWS
    n_sub = rows // sub_rows
    run4 = sub_rows // 4
    run16 = sub_rows // 16
    levels = (z0_ref, z1_ref, z2_ref, z3_ref)

    @pl.when(seq_tile == 0)
    def _():
        for ref in levels:
            ref[0:POOL_HALO, :] = jnp.zeros((POOL_HALO, POOL_WIDTH), F32)

    def project(lhs_ref, sub, col0):
        return jnp.dot(lhs_ref[sub], w_ref[:, col0:col0 + PROJ_COLS],
                       preferred_element_type=F32)

    gate0 = QKV_WIDTH + POOL_WIDTH
    for sub in range(n_sub):
        row0 = sub * sub_rows
        natural = slice(row0, row0 + sub_rows)
        uf = _rms_norm(x_ref[natural, :], gain_ref[...])
        u_ref[sub] = uf.astype(BF16)

        z0_ref[POOL_HALO + row0:POOL_HALO + row0 + sub_rows, :] = project(u_ref, sub, QKV_WIDTH)
        for c in range(gate_ref.shape[1] // PROJ_COLS):
            gate_ref[natural, c * PROJ_COLS:(c + 1) * PROJ_COLS] = (
                project(u_ref, sub, gate0 + c * PROJ_COLS))

        for s in range(n_slabs):
            uslab_ref[sub, s] = uf[:, s * LANES:(s + 1) * LANES]
        for r in range(4):
            dst = slice(r * run4, (r + 1) * run4)
            for s in range(n_slabs):
                piece = uslab_ref[sub, s, pl.ds(r, run4, stride=4), :]
                u4slab_ref[sub, s, dst, :] = piece
                u4_ref[sub, dst, s * LANES:(s + 1) * LANES] = piece.astype(BF16)
            for t in range(3):
                tab4_ref[sub, t, dst, :] = rope_ref[t, pl.ds(row0 + r, run4, stride=4), :]
        for run in range(16):
            src0 = (run // 4) * run4 + run % 4
            dst = slice(run * run16, (run + 1) * run16)
            for s in range(n_slabs):
                piece = u4slab_ref[sub, s, pl.ds(src0, run16, stride=4), :]
                u16_ref[sub, dst, s * LANES:(s + 1) * LANES] = piece.astype(BF16)
            for t in range(3):
                tab16_ref[sub, t, dst, :] = tab4_ref[sub, t, pl.ds(src0, run16, stride=4), :]

    cur = slice(POOL_HALO, POOL_HALO + rows)
    sums = []
    prev = z0_ref[cur, :]
    for lvl, ref in enumerate(levels):
        shift = 1 << lvl
        lo = lvl * POOL_GROUP_WIDTH
        shifted = ref[POOL_HALO - shift:POOL_HALO - shift + rows, lo:]
        prev = prev[:, (POOL_GROUP_WIDTH if lvl else 0):] + shifted
        sums.append(prev[:, 0:POOL_GROUP_WIDTH])
        if lvl + 1 < len(levels):
            levels[lvl + 1][cur, lo:] = prev
    pos = seq_tile * rows + lax.broadcasted_iota(jnp.int32, (rows, 1), 0)
    pooled = []
    for g, w in enumerate(POOL_WINDOWS):
        cols = slice(g * POOL_GROUP_WIDTH, (g + 1) * POOL_GROUP_WIDTH)
        count = jnp.minimum(pos + 1, w).astype(F32)
        pooled.append((sums[g] / count - z0_ref[cur, cols]).astype(BF16))
    for lvl, ref in enumerate(levels):
        lo = lvl * POOL_GROUP_WIDTH
        ref[0:POOL_HALO, lo:] = ref[rows:rows + POOL_HALO, lo:]

    def rope(xs, table):
        up = pltpu.roll(xs, LANES - ROT_HALF, axis=1)
        down = pltpu.roll(xs, ROT_HALF, axis=1)
        return xs * table(0) + up * table(1) + down * table(2)

    def store_natural(sub, col, part):
        qkv0_ref[sub * sub_rows:(sub + 1) * sub_rows, col:col + LANES] = part

    def store_by4(sub, col, part):
        for r in range(4):
            qkv1_ref[r, sub * run4:(sub + 1) * run4, col:col + LANES] = (
                part[r * run4:(r + 1) * run4])

    def store_by16(sub, col, part):
        for run in range(16):
            stream = 4 * (run % 4) + run // 4
            qkv2_ref[stream, sub * run16:(sub + 1) * run16, col:col + LANES] = (
                part[run * run16:(run + 1) * run16])

    for sub in range(n_sub):
        natural = slice(sub * sub_rows, (sub + 1) * sub_rows)
        variants = (
            (u_ref, lambda t: rope_ref[t, natural, :], store_natural),
            (u4_ref, lambda t: tab4_ref[sub, t], store_by4),
            (u16_ref, lambda t: tab16_ref[sub, t], store_by16),
        )
        for group, (lhs_ref, table, store) in enumerate(variants):
            for which in range(3):
                acc = project(lhs_ref, sub, which * ATT_WIDTH + group * ATT_OUT_WIDTH)
                for j in range(PROJ_COLS // LANES):
                    part = acc[:, j * LANES:(j + 1) * LANES]
                    if which == 0:
                        part = rope(part, table) * Q_SCALE
                    elif which == 1:
                        part = rope(part, table)
                    store(sub, which * ATT_OUT_WIDTH + j * LANES, part.astype(BF16))

    for g in range(len(POOL_WINDOWS)):
        cols = slice(g * POOL_GROUP_WIDTH, (g + 1) * POOL_GROUP_WIDTH)
        mapped = jnp.dot(pooled[g], wpg_ref[g], preferred_element_type=F32)
        pool_ref[:, cols] = (mapped * pscale_ref[:, cols]).astype(BF16)

    for src_ref, dst_ref in zip(later_f32, later_bf16):
        dst_ref[...] = src_ref[...].astype(BF16)


def _chunk_rows(total_rows, steps):
    chunk = BF16_SUBLANES * pl.cdiv(pl.cdiv(total_rows, steps), BF16_SUBLANES)
    while total_rows % chunk:
        chunk += BF16_SUBLANES
    return chunk


def _in_proj(x2, gain, w_in, w_pool_group, pool_scale, rope_tab, later_weights, batch, seq_len):
    tokens, d_model = x2.shape
    in_width = w_in.shape[1]
    gate_width = in_width - QKV_WIDTH - POOL_WIDTH
    rows = PROJ_ROWS
    n_sub = rows // PROJ_SUB_ROWS
    n_steps = tokens // rows
    tiles_per_seq = seq_len // rows
    const = dict(pipeline_mode=pl.Buffered(1))

    def stream_spec(dilation):
        return pl.BlockSpec((None, dilation, rows // dilation, GROUP_QKV_WIDTH),
                            lambda i: (i // tiles_per_seq, 0, i % tiles_per_seq, 0))

    def chunk_spec(weight):
        chunk = _chunk_rows(weight.shape[0], n_steps)
        last = weight.shape[0] // chunk - 1
        return pl.BlockSpec((chunk, weight.shape[1]), lambda i: (jnp.minimum(i, last), 0))

    later_specs = [chunk_spec(w) for w in later_weights]
    return pl.pallas_call(
        functools.partial(_in_proj_kernel, tiles_per_seq=tiles_per_seq,
                          n_later=len(later_weights)),
        name="in_proj",
        grid=(n_steps,),
        in_specs=[
            pl.BlockSpec((rows, d_model), lambda i: (i, 0)),
            pl.BlockSpec((1, d_model), lambda i: (0, 0), **const),
            pl.BlockSpec((d_model, in_width), lambda i: (0, 0), **const),
            pl.BlockSpec(w_pool_group.shape, lambda i: (0, 0, 0), **const),
            pl.BlockSpec((1, POOL_WIDTH), lambda i: (0, 0), **const),
            pl.BlockSpec((3, rows, LANES), lambda i: (0, i % tiles_per_seq, 0)),
        ] + later_specs,
        out_specs=[
            pl.BlockSpec((rows, GROUP_QKV_WIDTH), lambda i: (i, 0)),
            stream_spec(4),
            stream_spec(16),
            pl.BlockSpec((rows, POOL_WIDTH), lambda i: (i, 0)),
            pl.BlockSpec((rows, gate_width), lambda i: (i, 0)),
        ] + later_specs,
        out_shape=[
            jax.ShapeDtypeStruct((tokens, GROUP_QKV_WIDTH), BF16),
            jax.ShapeDtypeStruct((batch, 4, seq_len // 4, GROUP_QKV_WIDTH), BF16),
            jax.ShapeDtypeStruct((batch, 16, seq_len // 16, GROUP_QKV_WIDTH), BF16),
            jax.ShapeDtypeStruct((tokens, POOL_WIDTH), BF16),
            jax.ShapeDtypeStruct((tokens, gate_width), F32),
        ] + [jax.ShapeDtypeStruct(w.shape, BF16) for w in later_weights],
        scratch_shapes=[pltpu.VMEM((n_sub, PROJ_SUB_ROWS, d_model), BF16)] * 3
        + [pltpu.VMEM((n_sub, d_model // LANES, PROJ_SUB_ROWS, LANES), F32)] * 2
        + [pltpu.VMEM((n_sub, 3, PROJ_SUB_ROWS, LANES), F32)] * 2
        + [pltpu.VMEM((POOL_HALO + rows, POOL_WIDTH), F32) for _ in POOL_WINDOWS],
        compiler_params=pltpu.CompilerParams(
            dimension_semantics=("arbitrary",), vmem_limit_bytes=V7X_VMEM_LIMIT_BYTES),
    )(x2, gain, w_in, w_pool_group, pool_scale, rope_tab, *later_weights)


def _stat_lane(head):
    return head // 2 + (HEAD_DIM if head % 2 == 0 else 0)


def _store_planes(dilation):
    return max(1, dilation // MAX_STORE_STRIDE)


def _multi_plane_groups():
    return [g for g, (_, dilation) in enumerate(ATT_GROUPS) if _store_planes(dilation) > 1]


def _piece_shift(group, piece):
    return HEAD_PAIRS * (SPLIT_PIECES * group + piece)


def _attention_kernel(q_ref, k_ref, v_ref, bias_ref, eye_ref, stat_ref, o_ref,
                      kcarry_ref, vaug_ref,
                      *, single_step):
    dilation, rows, _ = q_ref.shape
    first = pl.program_id(1) == 0

    lane = lax.broadcasted_iota(jnp.int32, (Q_BLOCK, LANES), 1)
    low_half = lane < HEAD_DIM
    head_mask = (jnp.where(low_half, 1.0, 0.0).astype(BF16),
                 jnp.where(low_half, 0.0, 1.0).astype(BF16))
    low_rows = lax.broadcasted_iota(jnp.int32, (rows, LANES), 1) < HEAD_DIM
    one = jnp.ones((rows, LANES), BF16)
    nt_dims = (((1,), (1,)), ((), ()))
    zero_block = jnp.zeros((Q_BLOCK, ATT_OUT_WIDTH), BF16)

    def clear_carry():
        for st in range(dilation):
            kcarry_ref[st, 0:Q_BLOCK, :] = zero_block
            vaug_ref[0, st, 0:Q_BLOCK, :] = zero_block
            vaug_ref[1, st, 0:Q_BLOCK, :] = zero_block

    if single_step:
        clear_carry()
    else:
        pl.when(first)(clear_carry)

    for st in range(dilation):
        kcarry_ref[st, Q_BLOCK:2 * Q_BLOCK, :] = k_ref[st, 0:Q_BLOCK, :]
        for pair in range(HEAD_PAIRS):
            cols = slice(pair * LANES, (pair + 1) * LANES)
            v_pair = v_ref[st, :, cols]
            vaug_ref[0, st, Q_BLOCK:Q_BLOCK + rows, cols] = jnp.where(low_rows, v_pair, one)
            vaug_ref[1, st, Q_BLOCK:Q_BLOCK + rows, cols] = jnp.where(low_rows, one, v_pair)

    def scores(st, sub, pair):
        cols = slice(pair * LANES, (pair + 1) * LANES)
        q_pair = q_ref[st, sub * Q_BLOCK:(sub + 1) * Q_BLOCK, cols]
        if sub == 0:
            k_pair = kcarry_ref[st, :, cols]
            bias_idx = 1 if single_step else jnp.where(first, 1, 0)
        else:
            k_pair = k_ref[st, (sub - 1) * Q_BLOCK:(sub + 1) * Q_BLOCK, cols]
            bias_idx = 0
        q_both = jnp.concatenate([q_pair * head_mask[0], q_pair * head_mask[1]], axis=0)
        q_both = jnp.concatenate([q_both, eye_ref[...]], axis=1)
        k_aug = jnp.concatenate([k_pair, bias_ref[bias_idx]], axis=1)
        return lax.dot_general(q_both, k_aug, nt_dims, preferred_element_type=F32)

    def finish(st, sub, pair, s_both, stats):
        cols = slice(pair * LANES, (pair + 1) * LANES)
        band = slice(sub * Q_BLOCK, (sub + 2) * Q_BLOCK)
        planes = _store_planes(dilation)
        stride = dilation // planes
        start = sub * Q_BLOCK * stride + st // planes
        tokens = pl.ds(start, Q_BLOCK, stride=stride) if stride > 1 else pl.ds(start, Q_BLOCK)
        plane = st % planes
        pvs = []
        for half in range(2):
            s = s_both[half * Q_BLOCK:(half + 1) * Q_BLOCK]
            m = jnp.max(s, axis=-1, keepdims=True)
            p = jnp.exp2(s - m).astype(BF16)
            pv = jnp.dot(p, vaug_ref[half, st, band, cols], preferred_element_type=F32)
            pvs.append(pv)
            stat_lane = _stat_lane(2 * pair + half)
            stats = jnp.where(lane == stat_lane, m, stats)
            stats = jnp.where(lane == stat_lane + SUM_LANE_OFFSET, pv, stats)
        o_ref[pair, plane, tokens, :] = jnp.where(low_half, pvs[0], pvs[1])
        if pair == HEAD_PAIRS - 1:
            stat_ref[plane, tokens, :] = stats
        return stats

    units = [(st, sub, pair) for st in range(dilation) for sub in range(rows // Q_BLOCK)
             for pair in range(HEAD_PAIRS)]
    ahead = [scores(*unit) for unit in units[:SCORE_LOOKAHEAD]]
    stats = None
    for index, unit in enumerate(units):
        s_both = ahead.pop(0)
        if index + SCORE_LOOKAHEAD < len(units):
            ahead.append(scores(*units[index + SCORE_LOOKAHEAD]))
        if unit[2] == 0:
            stats = jnp.ones((Q_BLOCK, LANES), F32)
        stats = finish(*unit, s_both, stats)

    for st in range(dilation):
        kcarry_ref[st, 0:Q_BLOCK, :] = k_ref[st, rows - Q_BLOCK:rows, :]
        for half in range(2):
            vaug_ref[half, st, 0:Q_BLOCK, :] = vaug_ref[half, st, rows:rows + Q_BLOCK, :]


def _attention_group(qkv, bias, eye, batch, seq_len, group):
    _, dilation = ATT_GROUPS[group]
    stream_len = seq_len // dilation
    rows = max(Q_BLOCK, ATT_ROWS // dilation)
    planes = _store_planes(dilation)
    plane_rows = rows * dilation // planes
    single_step = rows == stream_len

    def qkv_spec(which):
        return pl.BlockSpec((None, dilation, rows, ATT_OUT_WIDTH), lambda b, m: (b, 0, m, which))

    stat_spec = pl.BlockSpec((None, planes, plane_rows, LANES), lambda b, m: (b, 0, m, 0))
    stat_shape = jax.ShapeDtypeStruct((batch, planes, seq_len // planes, LANES), F32)
    return pl.pallas_call(
        functools.partial(_attention_kernel, single_step=single_step),
        name=f"attention_g{group}",
        grid=(batch, stream_len // rows),
        in_specs=[
            qkv_spec(0), qkv_spec(1), qkv_spec(2),
            pl.BlockSpec(bias.shape, lambda b, m: (0, 0, 0), pipeline_mode=pl.Buffered(1)),
            pl.BlockSpec(eye.shape, lambda b, m: (0, 0), pipeline_mode=pl.Buffered(1)),
        ],
        out_specs=[
            stat_spec,
            pl.BlockSpec((None, HEAD_PAIRS, planes, plane_rows, LANES),
                         lambda b, m: (b, 0, 0, m, 0)),
        ],
        out_shape=[
            stat_shape,
            jax.ShapeDtypeStruct((batch, HEAD_PAIRS, planes, seq_len // planes, LANES), F32),
        ],
        scratch_shapes=[
            pltpu.VMEM((dilation, 2 * Q_BLOCK, ATT_OUT_WIDTH), BF16),
            pltpu.VMEM((2, dilation, Q_BLOCK + rows, ATT_OUT_WIDTH), BF16),
        ],
        compiler_params=pltpu.CompilerParams(
            dimension_semantics=("arbitrary", "arbitrary"),
            vmem_limit_bytes=V7X_VMEM_LIMIT_BYTES),
    )(qkv, qkv, qkv, bias, eye)


def _merge_ffn_kernel(x_ref, o0_ref, o1_ref, o2_ref, s0_ref, s1_ref, s2_ref,
                      pool_ref, gate_ref, spread_ref,
                      wab_ref, wpb_ref, wout_ref, gffn_ref, wg_ref, wu_ref, wd_ref, gfin_ref,
                      out_ref, att_ref, onat_ref, snat_ref):
    d_model = x_ref.shape[1]
    o_refs = (o0_ref, o1_ref, o2_ref)
    stat_refs = (s0_ref, s1_ref, s2_ref)
    lane = lax.broadcasted_iota(jnp.int32, (FFN_SUB_ROWS, LANES), 1)
    stat_lanes = (lane % HEAD_DIM) < HEAD_PAIRS

    slots = {g: slot for slot, g in enumerate(_multi_plane_groups())}

    def o_dst(g, pair):
        return onat_ref.at[slots[g], pair] if g in slots else None

    def stat_dst(g):
        return snat_ref.at[slots[g]] if g in slots else None

    def chain(sub):
        rows = slice(sub * FFN_SUB_ROWS, (sub + 1) * FFN_SUB_ROWS)

        def in_token_order(src_ref, dst_ref):
            planes = src_ref.shape[0]
            if planes == 1:
                return src_ref[0, rows, :]
            per_plane = FFN_SUB_ROWS // planes
            for plane in range(planes):
                dst_ref[pl.ds(sub * FFN_SUB_ROWS + plane, per_plane, stride=planes), :] = (
                    src_ref[plane, sub * per_plane:(sub + 1) * per_plane, :])
            return dst_ref[rows, :]

        ms = [in_token_order(ref, stat_dst(g)) for g, ref in enumerate(stat_refs)]
        ls = [pltpu.roll(m, LANES - SUM_LANE_OFFSET, axis=1) for m in ms]
        top = jnp.maximum(jnp.maximum(ms[0], ms[1]), ms[2])
        es = [jnp.exp2(m - top) for m in ms]
        den = es[0] * ls[0] + es[1] * ls[1] + es[2] * ls[2]

        packed = jnp.zeros((FFN_SUB_ROWS, LANES), F32)
        for g in range(N_ATT_GROUPS):
            rest = jnp.where(stat_lanes, es[g] / den, 0.0)
            for piece in range(SPLIT_PIECES):
                part = rest.astype(BF16).astype(F32)
                rest = rest - part
                shift = _piece_shift(g, piece)
                packed = packed + (pltpu.roll(part, shift, axis=1) if shift else part)
        spread = jnp.dot(packed.astype(BF16), spread_ref[...], preferred_element_type=F32)
        yield

        for pair in range(HEAD_PAIRS):
            acc = jnp.zeros((FFN_SUB_ROWS, LANES), F32)
            for g in range(N_ATT_GROUPS):
                col = g * ATT_OUT_WIDTH + pair * LANES
                o_pair = in_token_order(o_refs[g].at[pair], o_dst(g, pair))
                acc = acc + spread[:, col:col + LANES] * o_pair
            att_ref[rows, pair * LANES:(pair + 1) * LANES] = acc.astype(BF16)
        yield

        y_att = jnp.dot(att_ref[rows, :], wab_ref[...], preferred_element_type=F32)
        y_pool = jnp.dot(pool_ref[rows, :], wpb_ref[...], preferred_element_type=F32)
        yield
        merged = (jax.nn.sigmoid(gate_ref[rows, 0:d_model]) * y_att
                  + jax.nn.sigmoid(gate_ref[rows, d_model:2 * d_model]) * y_pool)
        yield
        h = x_ref[rows, :] + jnp.dot(merged.astype(BF16), wout_ref[...],
                                     preferred_element_type=F32)
        yield
        f = _rms_norm(h, gffn_ref[...]).astype(BF16)
        yield
        ffn = jnp.zeros((FFN_SUB_ROWS, d_model), F32)
        for lo, hi in FFN_CHUNKS:
            gate = jnp.dot(f, wg_ref[:, lo:hi], preferred_element_type=F32)
            up = jnp.dot(f, wu_ref[:, lo:hi], preferred_element_type=F32)
            yield
            hidden = (jax.nn.silu(gate) * up).astype(BF16)
            yield
            ffn = ffn + jnp.dot(hidden, wd_ref[lo:hi, :], preferred_element_type=F32)
            yield
        out_ref[rows, :] = _rms_norm(h + ffn, gfin_ref[...])

    _run_staggered([chain(sub) for sub in range(x_ref.shape[0] // FFN_SUB_ROWS)])


def _merge_ffn(x2, outs, stats, pool_feat, gates, w_att_branch, w_pool_branch, w_out, norm_ffn,
               w_gate, w_up, w_down, norm_final, seq_len):
    tokens, d_model = x2.shape
    rows = FFN_ROWS
    tiles_per_seq = seq_len // rows

    def tile(width):
        return pl.BlockSpec((rows, width), lambda i: (i, 0))

    def whole(arr):
        return pl.BlockSpec(arr.shape, lambda i: (0, 0), pipeline_mode=pl.Buffered(1))

    def o_spec(planes):
        return pl.BlockSpec((None, HEAD_PAIRS, planes, rows // planes, LANES),
                            lambda i: (i // tiles_per_seq, 0, 0, i % tiles_per_seq, 0))

    def stat_spec(planes):
        return pl.BlockSpec((None, planes, rows // planes, LANES),
                            lambda i: (i // tiles_per_seq, 0, i % tiles_per_seq, 0))

    group_planes = [_store_planes(dilation) for _, dilation in ATT_GROUPS]
    n_slots = max(1, len(_multi_plane_groups()))
    weights = (_spread_matrix(), w_att_branch, w_pool_branch, w_out, norm_ffn, w_gate, w_up,
               w_down, norm_final)
    return pl.pallas_call(
        _merge_ffn_kernel,
        name="merge_ffn",
        grid=(tokens // rows,),
        in_specs=[tile(d_model)]
        + [o_spec(planes) for planes in group_planes]
        + [stat_spec(planes) for planes in group_planes]
        + [tile(POOL_WIDTH), tile(gates.shape[1])]
        + [whole(w) for w in weights],
        out_specs=tile(d_model),
        out_shape=jax.ShapeDtypeStruct((tokens, d_model), F32),
        scratch_shapes=[
            pltpu.VMEM((rows, ATT_OUT_WIDTH), BF16),
            pltpu.VMEM((n_slots, HEAD_PAIRS, rows, LANES), F32),
            pltpu.VMEM((n_slots, rows, LANES), F32),
        ],
        compiler_params=pltpu.CompilerParams(
            dimension_semantics=("arbitrary",), vmem_limit_bytes=V7X_VMEM_LIMIT_BYTES),
    )(x2, *outs, *stats, pool_feat, gates, *weights)


def _rope_tables(seq_len):
    inv_freq = ROPE_THETA ** (-np.arange(0, ROT_DIM, 2, dtype=np.float64) / ROT_DIM)
    ang = np.arange(seq_len, dtype=np.float64)[:, None] * inv_freq[None, :]
    cos, sin = np.cos(ang), np.sin(ang)
    zeros = np.zeros((seq_len, HEAD_DIM - ROT_DIM))
    half0 = np.zeros((seq_len, ROT_HALF))
    c_head = np.concatenate([cos, cos, zeros + 1.0], axis=1)
    lo_head = np.concatenate([-sin, half0, zeros], axis=1)
    hi_head = np.concatenate([half0, sin, zeros], axis=1)
    tab = np.stack([c_head, lo_head, hi_head])
    return jnp.asarray(np.concatenate([tab] * (LANES // HEAD_DIM), axis=2), dtype=F32)


def _spread_matrix():
    row = jnp.arange(LANES)[:, None]
    col = jnp.arange(N_ATT_GROUPS * ATT_OUT_WIDTH)[None, :]
    group = col // ATT_OUT_WIDTH
    head = (col % ATT_OUT_WIDTH) // HEAD_DIM
    stat = head // 2 + jnp.where(head % 2 == 0, HEAD_DIM, 0)
    hit = jnp.zeros((LANES, N_ATT_GROUPS * ATT_OUT_WIDTH), jnp.bool_)
    for piece in range(SPLIT_PIECES):
        hit = hit | (row == stat + HEAD_PAIRS * (SPLIT_PIECES * group + piece))
    return hit.astype(BF16)


def _band_bias():
    kb = jnp.arange(2 * Q_BLOCK)[:, None]
    qi = jnp.arange(Q_BLOCK)[None, :]
    dist = qi + Q_BLOCK - kb
    valid = (dist >= 0) & (dist <= WINDOW_KEYS)
    both = jnp.stack([valid, valid & (kb >= Q_BLOCK)])
    return jnp.where(both, 0.0, MASKED).astype(BF16)


def _stacked_identity():
    row = jnp.arange(2 * Q_BLOCK)[:, None]
    col = jnp.arange(LANES)[None, :]
    return (row % Q_BLOCK == col).astype(BF16)


def kernel(x, norm_mix, w_in, w_pool_group, pool_scale, w_att_branch, w_pool_branch, w_out,
           norm_ffn, w_ffn_gate, w_ffn_up, w_ffn_down, norm_final):
    batch, seq_len, d_model = x.shape
    depth = w_in.shape[0]
    tokens = batch * seq_len
    assert depth == 1, "stacked layers need the un-normalised residual between layers"
    bias = _band_bias()

    h = x.reshape(tokens, d_model)
    later_weights = (w_att_branch[0], w_pool_branch[0], w_out[0], w_ffn_gate[0], w_ffn_up[0],
                     w_ffn_down[0])
    qkv0, qkv1, qkv2, pool_feat, gates, *later_bf16 = _in_proj(
        h, norm_mix[0][None, :], w_in[0].astype(BF16), w_pool_group[0].astype(BF16),
        pool_scale[0][None, :], _rope_tables(seq_len), later_weights, batch, seq_len)
    wab, wpb, wout, wgate, wup, wdown = later_bf16
    qkv_groups = (qkv0.reshape(batch, 1, seq_len, GROUP_QKV_WIDTH), qkv1, qkv2)
    eye = _stacked_identity()
    att = [_attention_group(qkv_groups[g], bias, eye, batch, seq_len, g)
           for g in range(N_ATT_GROUPS)]
    out = _merge_ffn(
        h, [o for _, o in att], [stats for stats, _ in att], pool_feat, gates,
        wab, wpb, wout, norm_ffn[0][None, :], wgate, wup, wdown, norm_final[None, :], seq_len)
    return out.reshape(batch, seq_len, d_model)
```

```python
import functools
import math

import jax
import jax.numpy as jnp
import numpy as np
from jax import lax
from jax.experimental import pallas as pl
from jax.experimental.pallas import tpu as pltpu

F32 = jnp.float32
BF16 = jnp.bfloat16

HEAD_DIM = 64
HEADS_PER_GROUP = 8
ATT_GROUPS = ((128, 1), (512, 4), (2048, 16))
N_ATT_GROUPS = len(ATT_GROUPS)
ATT_OUT_WIDTH = HEADS_PER_GROUP * HEAD_DIM
ATT_WIDTH = N_ATT_GROUPS * ATT_OUT_WIDTH
QKV_WIDTH = 3 * ATT_WIDTH
GROUP_QKV_WIDTH = 3 * ATT_OUT_WIDTH
ROT_DIM = HEAD_DIM // 4
ROT_HALF = ROT_DIM // 2
ROPE_THETA = 500000.0
POOL_WINDOWS = (2, 4, 8, 16)
POOL_GROUP_WIDTH = 128
POOL_WIDTH = len(POOL_WINDOWS) * POOL_GROUP_WIDTH
NORM_EPS = 1e-6
WINDOW_KEYS = 128
MASKED = -3.0e38
Q_SCALE = HEAD_DIM ** -0.5 * math.log2(math.e)

LANES = 128
BF16_SUBLANES = 16
MAX_STORE_STRIDE = 4
V7X_VMEM_BYTES = 64 * 1024 * 1024
V7X_VMEM_LIMIT_BYTES = V7X_VMEM_BYTES - 4 * 1024 * 1024

PROJ_ROWS = 512
PROJ_SUB_ROWS = 256
PROJ_COLS = ATT_OUT_WIDTH
assert PROJ_COLS == POOL_WIDTH
POOL_HALO = 16
ATT_ROWS = 2048
Q_BLOCK = 128
HEAD_PAIRS = HEADS_PER_GROUP // 2
SUM_LANE_OFFSET = 2 * HEAD_PAIRS
SCORE_LOOKAHEAD = 1
FFN_ROWS = 512
FFN_SUB_ROWS = 256
FFN_CHUNKS = ((0, 1024), (1024, 2048), (2048, 2816))
SPLIT_PIECES = 3


def _run_staggered(chains):
    live = list(enumerate(chains))
    tick = 0
    while live:
        for item in list(live):
            index, gen = item
            if tick >= index and next(gen, StopIteration) is StopIteration:
                live.remove(item)
        tick += 1


def _rms_norm(xf, gain):
    ms = jnp.mean(xf * xf, axis=-1, keepdims=True)
    return (xf * lax.rsqrt(ms + NORM_EPS)) * gain


def _in_proj_kernel(x_ref, gain_ref, w_ref, wpg_ref, pscale_ref, rope_ref, *rest,
                    tiles_per_seq, n_later):
    later_f32 = rest[:n_later]
    qkv0_ref, qkv1_ref, qkv2_ref, pool_ref, gate_ref = rest[n_later:n_later + 5]
    later_bf16 = rest[n_later + 5:2 * n_later + 5]
    (u_ref, u4_ref, u16_ref, uslab_ref, u4slab_ref, tab4_ref, tab16_ref,
     z0_ref, z1_ref, z2_ref, z3_ref) = rest[2 * n_later + 5:]
    rows, d_model = x_ref.shape
    seq_tile = pl.program_id(0) % tiles_per_seq
    n_slabs = d_model // LANES
    sub_rows = PROJ_SUB_ROWS
    n_sub = rows // sub_rows
    run4 = sub_rows // 4
    run16 = sub_rows // 16
    levels = (z0_ref, z1_ref, z2_ref, z3_ref)

    @pl.when(seq_tile == 0)
    def _():
        for ref in levels:
            ref[0:POOL_HALO, :] = jnp.zeros((POOL_HALO, POOL_WIDTH), F32)

    def project(lhs_ref, sub, col0):
        return jnp.dot(lhs_ref[sub], w_ref[:, col0:col0 + PROJ_COLS],
                       preferred_element_type=F32)

    gate0 = QKV_WIDTH + POOL_WIDTH
    for sub in range(n_sub):
        row0 = sub * sub_rows
        natural = slice(row0, row0 + sub_rows)
        uf = _rms_norm(x_ref[natural, :], gain_ref[...])
        u_ref[sub] = uf.astype(BF16)

        z0_ref[POOL_HALO + row0:POOL_HALO + row0 + sub_rows, :] = project(u_ref, sub, QKV_WIDTH)
        for c in range(gate_ref.shape[1] // PROJ_COLS):
            gate_ref[natural, c * PROJ_COLS:(c + 1) * PROJ_COLS] = (
                project(u_ref, sub, gate0 + c * PROJ_COLS))

        for s in range(n_slabs):
            uslab_ref[sub, s] = uf[:, s * LANES:(s + 1) * LANES]
        for r in range(4):
            dst = slice(r * run4, (r + 1) * run4)
            for s in range(n_slabs):
                piece = uslab_ref[sub, s, pl.ds(r, run4, stride=4), :]
                u4slab_ref[sub, s, dst, :] = piece
                u4_ref[sub, dst, s * LANES:(s + 1) * LANES] = piece.astype(BF16)
            for t in range(3):
                tab4_ref[sub, t, dst, :] = rope_ref[t, pl.ds(row0 + r, run4, stride=4), :]
        for run in range(16):
            src0 = (run // 4) * run4 + run % 4
            dst = slice(run * run16, (run + 1) * run16)
            for s in range(n_slabs):
                piece = u4slab_ref[sub, s, pl.ds(src0, run16, stride=4), :]
                u16_ref[sub, dst, s * LANES:(s + 1) * LANES] = piece.astype(BF16)
            for t in range(3):
                tab16_ref[sub, t, dst, :] = tab4_ref[sub, t, pl.ds(src0, run16, stride=4), :]

    cur = slice(POOL_HALO, POOL_HALO + rows)
    sums = []
    prev = z0_ref[cur, :]
    for lvl, ref in enumerate(levels):
        shift = 1 << lvl
        lo = lvl * POOL_GROUP_WIDTH
        shifted = ref[POOL_HALO - shift:POOL_HALO - shift + rows, lo:]
        prev = prev[:, (POOL_GROUP_WIDTH if lvl else 0):] + shifted
        sums.append(prev[:, 0:POOL_GROUP_WIDTH])
        if lvl + 1 < len(levels):
            levels[lvl + 1][cur, lo:] = prev
    pos = seq_tile * rows + lax.broadcasted_iota(jnp.int32, (rows, 1), 0)
    pooled = []
    for g, w in enumerate(POOL_WINDOWS):
        cols = slice(g * POOL_GROUP_WIDTH, (g + 1) * POOL_GROUP_WIDTH)
        count = jnp.minimum(pos + 1, w).astype(F32)
        pooled.append((sums[g] / count - z0_ref[cur, cols]).astype(BF16))
    for lvl, ref in enumerate(levels):
        lo = lvl * POOL_GROUP_WIDTH
        ref[0:POOL_HALO, lo:] = ref[rows:rows + POOL_HALO, lo:]

    def rope(xs, table):
        up = pltpu.roll(xs, LANES - ROT_HALF, axis=1)
        down = pltpu.roll(xs, ROT_HALF, axis=1)
        return xs * table(0) + up * table(1) + down * table(2)

    def store_natural(sub, col, part):
        qkv0_ref[sub * sub_rows:(sub + 1) * sub_rows, col:col + LANES] = part

    def store_by4(sub, col, part):
        for r in range(4):
            qkv1_ref[r, sub * run4:(sub + 1) * run4, col:col + LANES] = (
                part[r * run4:(r + 1) * run4])

    def store_by16(sub, col, part):
        for run in range(16):
            stream = 4 * (run % 4) + run // 4
            qkv2_ref[stream, sub * run16:(sub + 1) * run16, col:col + LANES] = (
                part[run * run16:(run + 1) * run16])

    for sub in range(n_sub):
        natural = slice(sub * sub_rows, (sub + 1) * sub_rows)
        variants = (
            (u_ref, lambda t: rope_ref[t, natural, :], store_natural),
            (u4_ref, lambda t: tab4_ref[sub, t], store_by4),
            (u16_ref, lambda t: tab16_ref[sub, t], store_by16),
        )
        for group, (lhs_ref, table, store) in enumerate(variants):
            for which in range(3):
                acc = project(lhs_ref, sub, which * ATT_WIDTH + group * ATT_OUT_WIDTH)
                for j in range(PROJ_COLS // LANES):
                    part = acc[:, j * LANES:(j + 1) * LANES]
                    if which == 0:
                        part = rope(part, table) * Q_SCALE
                    elif which == 1:
                        part = rope(part, table)
                    store(sub, which * ATT_OUT_WIDTH + j * LANES, part.astype(BF16))

    for g in range(len(POOL_WINDOWS)):
        cols = slice(g * POOL_GROUP_WIDTH, (g + 1) * POOL_GROUP_WIDTH)
        mapped = jnp.dot(pooled[g], wpg_ref[g], preferred_element_type=F32)
        pool_ref[:, cols] = (mapped * pscale_ref[:, cols]).astype(BF16)

    for src_ref, dst_ref in zip(later_f32, later_bf16):
        dst_ref[...] = src_ref[...].astype(BF16)


def _chunk_rows(total_rows, steps):
    chunk = BF16_SUBLANES * pl.cdiv(pl.cdiv(total_rows, steps), BF16_SUBLANES)
    while total_rows % chunk:
        chunk += BF16_SUBLANES
    return chunk


def _in_proj(x2, gain, w_in, w_pool_group, pool_scale, rope_tab, later_weights, batch, seq_len):
    tokens, d_model = x2.shape
    in_width = w_in.shape[1]
    gate_width = in_width - QKV_WIDTH - POOL_WIDTH
    rows = PROJ_ROWS
    n_sub = rows // PROJ_SUB_ROWS
    n_steps = tokens // rows
    tiles_per_seq = seq_len // rows
    const = dict(pipeline_mode=pl.Buffered(1))

    def stream_spec(dilation):
        return pl.BlockSpec((None, dilation, rows // dilation, GROUP_QKV_WIDTH),
                            lambda i: (i // tiles_per_seq, 0, i % tiles_per_seq, 0))

    def chunk_spec(weight):
        chunk = _chunk_rows(weight.shape[0], n_steps)
        last = weight.shape[0] // chunk - 1
        return pl.BlockSpec((chunk, weight.shape[1]), lambda i: (jnp.minimum(i, last), 0))

    later_specs = [chunk_spec(w) for w in later_weights]
    return pl.pallas_call(
        functools.partial(_in_proj_kernel, tiles_per_seq=tiles_per_seq,
                          n_later=len(later_weights)),
        name="in_proj",
        grid=(n_steps,),
        in_specs=[
            pl.BlockSpec((rows, d_model), lambda i: (i, 0)),
            pl.BlockSpec((1, d_model), lambda i: (0, 0), **const),
            pl.BlockSpec((d_model, in_width), lambda i: (0, 0), **const),
            pl.BlockSpec(w_pool_group.shape, lambda i: (0, 0, 0), **const),
            pl.BlockSpec((1, POOL_WIDTH), lambda i: (0, 0), **const),
            pl.BlockSpec((3, rows, LANES), lambda i: (0, i % tiles_per_seq, 0)),
        ] + later_specs,
        out_specs=[
            pl.BlockSpec((rows, GROUP_QKV_WIDTH), lambda i: (i, 0)),
            stream_spec(4),
            stream_spec(16),
            pl.BlockSpec((rows, POOL_WIDTH), lambda i: (i, 0)),
            pl.BlockSpec((rows, gate_width), lambda i: (i, 0)),
        ] + later_specs,
        out_shape=[
            jax.ShapeDtypeStruct((tokens, GROUP_QKV_WIDTH), BF16),
            jax.ShapeDtypeStruct((batch, 4, seq_len // 4, GROUP_QKV_WIDTH), BF16),
            jax.ShapeDtypeStruct((batch, 16, seq_len // 16, GROUP_QKV_WIDTH), BF16),
            jax.ShapeDtypeStruct((tokens, POOL_WIDTH), BF16),
            jax.ShapeDtypeStruct((tokens, gate_width), F32),
        ] + [jax.ShapeDtypeStruct(w.shape, BF16) for w in later_weights],
        scratch_shapes=[pltpu.VMEM((n_sub, PROJ_SUB_ROWS, d_model), BF16)] * 3
        + [pltpu.VMEM((n_sub, d_model // LANES, PROJ_SUB_ROWS, LANES), F32)] * 2
        + [pltpu.VMEM((n_sub, 3, PROJ_SUB_ROWS, LANES), F32)] * 2
        + [pltpu.VMEM((POOL_HALO + rows, POOL_WIDTH), F32) for _ in POOL_WINDOWS],
        compiler_params=pltpu.CompilerParams(
            dimension_semantics=("arbitrary",), vmem_limit_bytes=V7X_VMEM_LIMIT_BYTES),
    )(x2, gain, w_in, w_pool_group, pool_scale, rope_tab, *later_weights)


def _stat_lane(head):
    return head // 2 + (HEAD_DIM if head % 2 == 0 else 0)


def _store_planes(dilation):
    return max(1, dilation // MAX_STORE_STRIDE)


def _multi_plane_groups():
    return [g for g, (_, dilation) in enumerate(ATT_GROUPS) if _store_planes(dilation) > 1]


def _piece_shift(group, piece):
    return HEAD_PAIRS * (SPLIT_PIECES * group + piece)


def _attention_kernel(q_ref, k_ref, v_ref, bias_ref, eye_ref, stat_ref, o_ref,
                      kcarry_ref, vaug_ref,
                      *, single_step):
    dilation, rows, _ = q_ref.shape
    first = pl.program_id(1) == 0

    lane = lax.broadcasted_iota(jnp.int32, (Q_BLOCK, LANES), 1)
    low_half = lane < HEAD_DIM
    head_mask = (jnp.where(low_half, 1.0, 0.0).astype(BF16),
                 jnp.where(low_half, 0.0, 1.0).astype(BF16))
    low_rows = lax.broadcasted_iota(jnp.int32, (rows, LANES), 1) < HEAD_DIM
    one = jnp.ones((rows, LANES), BF16)
    nt_dims = (((1,), (1,)), ((), ()))
    zero_block = jnp.zeros((Q_BLOCK, ATT_OUT_WIDTH), BF16)

    def clear_carry():
        for st in range(dilation):
            kcarry_ref[st, 0:Q_BLOCK, :] = zero_block
            vaug_ref[0, st, 0:Q_BLOCK, :] = zero_block
            vaug_ref[1, st, 0:Q_BLOCK, :] = zero_block

    if single_step:
        clear_carry()
    else:
        pl.when(first)(clear_carry)

    def stage_keys(st):
        kcarry_ref[st, Q_BLOCK:2 * Q_BLOCK, :] = k_ref[st, 0:Q_BLOCK, :]

    def stage_values(st):
        for pair in range(HEAD_PAIRS):
            cols = slice(pair * LANES, (pair + 1) * LANES)
            v_pair = v_ref[st, :, cols]
            vaug_ref[0, st, Q_BLOCK:Q_BLOCK + rows, cols] = jnp.where(low_rows, v_pair, one)
            vaug_ref[1, st, Q_BLOCK:Q_BLOCK + rows, cols] = jnp.where(low_rows, one, v_pair)

    def scores(st, sub, pair):
        cols = slice(pair * LANES, (pair + 1) * LANES)
        q_pair = q_ref[st, sub * Q_BLOCK:(sub + 1) * Q_BLOCK, cols]
        if sub == 0:
            k_pair = kcarry_ref[st, :, cols]
            bias_idx = 1 if single_step else jnp.where(first, 1, 0)
        else:
            k_pair = k_ref[st, (sub - 1) * Q_BLOCK:(sub + 1) * Q_BLOCK, cols]
            bias_idx = 0
        q_both = jnp.concatenate([q_pair * head_mask[0], q_pair * head_mask[1]], axis=0)
        q_both = jnp.concatenate([q_both, eye_ref[...]], axis=1)
        k_aug = jnp.concatenate([k_pair, bias_ref[bias_idx]], axis=1)
        return lax.dot_general(q_both, k_aug, nt_dims, preferred_element_type=F32)

    def finish(st, sub, pair, s_both, stats):
        cols = slice(pair * LANES, (pair + 1) * LANES)
        band = slice(sub * Q_BLOCK, (sub + 2) * Q_BLOCK)
        planes = _store_planes(dilation)
        stride = dilation // planes
        start = sub * Q_BLOCK * stride + st // planes
        tokens = pl.ds(start, Q_BLOCK, stride=stride) if stride > 1 else pl.ds(start, Q_BLOCK)
        plane = st % planes
        pvs = []
        for half in range(2):
            s = s_both[half * Q_BLOCK:(half + 1) * Q_BLOCK]
            m = jnp.max(s, axis=-1, keepdims=True)
            p = jnp.exp2(s - m).astype(BF16)
            pv = jnp.dot(p, vaug_ref[half, st, band, cols], preferred_element_type=F32)
            pvs.append(pv)
            stat_lane = _stat_lane(2 * pair + half)
            stats = jnp.where(lane == stat_lane, m, stats)
            stats = jnp.where(lane == stat_lane + SUM_LANE_OFFSET, pv, stats)
        o_ref[pair, plane, tokens, :] = jnp.where(low_half, pvs[0], pvs[1])
        if pair == HEAD_PAIRS - 1:
            stat_ref[plane, tokens, :] = stats
        return stats

    units = [(st, sub, pair) for st in range(dilation) for sub in range(rows // Q_BLOCK)
             for pair in range(HEAD_PAIRS)]
    def scores_staged(st, sub, pair):
        opens_stream = sub == 0 and pair == 0
        if opens_stream:
            stage_keys(st)
        s_both = scores(st, sub, pair)
        if opens_stream:
            stage_values(st)
        return s_both

    ahead = [scores_staged(*unit) for unit in units[:SCORE_LOOKAHEAD]]
    stats = None
    for index, unit in enumerate(units):
        s_both = ahead.pop(0)
        if index + SCORE_LOOKAHEAD < len(units):
            ahead.append(scores_staged(*units[index + SCORE_LOOKAHEAD]))
        if unit[2] == 0:
            stats = jnp.ones((Q_BLOCK, LANES), F32)
        stats = finish(*unit, s_both, stats)

    for st in range(dilation):
        kcarry_ref[st, 0:Q_BLOCK, :] = k_ref[st, rows - Q_BLOCK:rows, :]
        for half in range(2):
            vaug_ref[half, st, 0:Q_BLOCK, :] = vaug_ref[half, st, rows:rows + Q_BLOCK, :]


def _attention_group(qkv, bias, eye, batch, seq_len, group):
    _, dilation = ATT_GROUPS[group]
    stream_len = seq_len // dilation
    rows = max(Q_BLOCK, ATT_ROWS // dilation)
    planes = _store_planes(dilation)
    plane_rows = rows * dilation // planes
    single_step = rows == stream_len

    def qkv_spec(which):
        return pl.BlockSpec((None, dilation, rows, ATT_OUT_WIDTH), lambda b, m: (b, 0, m, which))

    stat_spec = pl.BlockSpec((None, planes, plane_rows, LANES), lambda b, m: (b, 0, m, 0))
    stat_shape = jax.ShapeDtypeStruct((batch, planes, seq_len // planes, LANES), F32)
    return pl.pallas_call(
        functools.partial(_attention_kernel, single_step=single_step),
        name=f"attention_g{group}",
        grid=(batch, stream_len // rows),
        in_specs=[
            qkv_spec(0), qkv_spec(1), qkv_spec(2),
            pl.BlockSpec(bias.shape, lambda b, m: (0, 0, 0), pipeline_mode=pl.Buffered(1)),
            pl.BlockSpec(eye.shape, lambda b, m: (0, 0), pipeline_mode=pl.Buffered(1)),
        ],
        out_specs=[
            stat_spec,
            pl.BlockSpec((None, HEAD_PAIRS, planes, plane_rows, LANES),
                         lambda b, m: (b, 0, 0, m, 0)),
        ],
        out_shape=[
            stat_shape,
            jax.ShapeDtypeStruct((batch, HEAD_PAIRS, planes, seq_len // planes, LANES), F32),
        ],
        scratch_shapes=[
            pltpu.VMEM((dilation, 2 * Q_BLOCK, ATT_OUT_WIDTH), BF16),
            pltpu.VMEM((2, dilation, Q_BLOCK + rows, ATT_OUT_WIDTH), BF16),
        ],
        compiler_params=pltpu.CompilerParams(
            dimension_semantics=("arbitrary", "arbitrary"),
            vmem_limit_bytes=V7X_VMEM_LIMIT_BYTES),
    )(qkv, qkv, qkv, bias, eye)


def _merge_ffn_kernel(x_ref, o0_ref, o1_ref, o2_ref, s0_ref, s1_ref, s2_ref,
                      pool_ref, gate_ref, spread_ref,
                      wab_ref, wpb_ref, wout_ref, gffn_ref, wg_ref, wu_ref, wd_ref, gfin_ref,
                      out_ref, att_ref, onat_ref, snat_ref):
    d_model = x_ref.shape[1]
    o_refs = (o0_ref, o1_ref, o2_ref)
    stat_refs = (s0_ref, s1_ref, s2_ref)
    lane = lax.broadcasted_iota(jnp.int32, (FFN_SUB_ROWS, LANES), 1)
    stat_lanes = (lane % HEAD_DIM) < HEAD_PAIRS

    slots = {g: slot for slot, g in enumerate(_multi_plane_groups())}

    def o_dst(g, pair):
        return onat_ref.at[slots[g], pair] if g in slots else None

    def stat_dst(g):
        return snat_ref.at[slots[g]] if g in slots else None

    def chain(sub):
        rows = slice(sub * FFN_SUB_ROWS, (sub + 1) * FFN_SUB_ROWS)

        def in_token_order(src_ref, dst_ref):
            planes = src_ref.shape[0]
            if planes == 1:
                return src_ref[0, rows, :]
            per_plane = FFN_SUB_ROWS // planes
            for plane in range(planes):
                dst_ref[pl.ds(sub * FFN_SUB_ROWS + plane, per_plane, stride=planes), :] = (
                    src_ref[plane, sub * per_plane:(sub + 1) * per_plane, :])
            return dst_ref[rows, :]

        y_pool = jnp.dot(pool_ref[rows, :], wpb_ref[...], preferred_element_type=F32)

        ms = [in_token_order(ref, stat_dst(g)) for g, ref in enumerate(stat_refs)]
        ls = [pltpu.roll(m, LANES - SUM_LANE_OFFSET, axis=1) for m in ms]
        top = jnp.maximum(jnp.maximum(ms[0], ms[1]), ms[2])
        es = [jnp.exp2(m - top) for m in ms]
        den = es[0] * ls[0] + es[1] * ls[1] + es[2] * ls[2]

        packed = jnp.zeros((FFN_SUB_ROWS, LANES), F32)
        for g in range(N_ATT_GROUPS):
            rest = jnp.where(stat_lanes, es[g] / den, 0.0)
            for piece in range(SPLIT_PIECES):
                part = rest.astype(BF16).astype(F32)
                rest = rest - part
                shift = _piece_shift(g, piece)
                packed = packed + (pltpu.roll(part, shift, axis=1) if shift else part)
        spread = jnp.dot(packed.astype(BF16), spread_ref[...], preferred_element_type=F32)
        yield

        for pair in range(HEAD_PAIRS):
            acc = jnp.zeros((FFN_SUB_ROWS, LANES), F32)
            for g in range(N_ATT_GROUPS):
                col = g * ATT_OUT_WIDTH + pair * LANES
                o_pair = in_token_order(o_refs[g].at[pair], o_dst(g, pair))
                acc = acc + spread[:, col:col + LANES] * o_pair
            att_ref[rows, pair * LANES:(pair + 1) * LANES] = acc.astype(BF16)
        yield

        y_att = jnp.dot(att_ref[rows, :], wab_ref[...], preferred_element_type=F32)
        yield
        merged = (jax.nn.sigmoid(gate_ref[rows, 0:d_model]) * y_att
                  + jax.nn.sigmoid(gate_ref[rows, d_model:2 * d_model]) * y_pool)
        yield
        h = x_ref[rows, :] + jnp.dot(merged.astype(BF16), wout_ref[...],
                                     preferred_element_type=F32)
        yield
        f = _rms_norm(h, gffn_ref[...]).astype(BF16)
        yield
        ffn = jnp.zeros((FFN_SUB_ROWS, d_model), F32)
        for lo, hi in FFN_CHUNKS:
            gate = jnp.dot(f, wg_ref[:, lo:hi], preferred_element_type=F32)
            up = jnp.dot(f, wu_ref[:, lo:hi], preferred_element_type=F32)
            yield
            hidden = (jax.nn.silu(gate) * up).astype(BF16)
            yield
            ffn = ffn + jnp.dot(hidden, wd_ref[lo:hi, :], preferred_element_type=F32)
            yield
        out_ref[rows, :] = _rms_norm(h + ffn, gfin_ref[...])

    _run_staggered([chain(sub) for sub in range(x_ref.shape[0] // FFN_SUB_ROWS)])


def _merge_ffn(x2, outs, stats, pool_feat, gates, w_att_branch, w_pool_branch, w_out, norm_ffn,
               w_gate, w_up, w_down, norm_final, seq_len):
    tokens, d_model = x2.shape
    rows = FFN_ROWS
    tiles_per_seq = seq_len // rows

    def tile(width):
        return pl.BlockSpec((rows, width), lambda i: (i, 0))

    def whole(arr):
        return pl.BlockSpec(arr.shape, lambda i: (0, 0), pipeline_mode=pl.Buffered(1))

    def o_spec(planes):
        return pl.BlockSpec((None, HEAD_PAIRS, planes, rows // planes, LANES),
                            lambda i: (i // tiles_per_seq, 0, 0, i % tiles_per_seq, 0))

    def stat_spec(planes):
        return pl.BlockSpec((None, planes, rows // planes, LANES),
                            lambda i: (i // tiles_per_seq, 0, i % tiles_per_seq, 0))

    group_planes = [_store_planes(dilation) for _, dilation in ATT_GROUPS]
    n_slots = max(1, len(_multi_plane_groups()))
    weights = (_spread_matrix(), w_att_branch, w_pool_branch, w_out, norm_ffn, w_gate, w_up,
               w_down, norm_final)
    return pl.pallas_call(
        _merge_ffn_kernel,
        name="merge_ffn",
        grid=(tokens // rows,),
        in_specs=[tile(d_model)]
        + [o_spec(planes) for planes in group_planes]
        + [stat_spec(planes) for planes in group_planes]
        + [tile(POOL_WIDTH), tile(gates.shape[1])]
        + [whole(w) for w in weights],
        out_specs=tile(d_model),
        out_shape=jax.ShapeDtypeStruct((tokens, d_model), F32),
        scratch_shapes=[
            pltpu.VMEM((rows, ATT_OUT_WIDTH), BF16),
            pltpu.VMEM((n_slots, HEAD_PAIRS, rows, LANES), F32),
            pltpu.VMEM((n_slots, rows, LANES), F32),
        ],
        compiler_params=pltpu.CompilerParams(
            dimension_semantics=("arbitrary",), vmem_limit_bytes=V7X_VMEM_LIMIT_BYTES),
    )(x2, *outs, *stats, pool_feat, gates, *weights)


def _rope_tables(seq_len):
    inv_freq = ROPE_THETA ** (-np.arange(0, ROT_DIM, 2, dtype=np.float64) / ROT_DIM)
    ang = np.arange(seq_len, dtype=np.float64)[:, None] * inv_freq[None, :]
    cos, sin = np.cos(ang), np.sin(ang)
    zeros = np.zeros((seq_len, HEAD_DIM - ROT_DIM))
    half0 = np.zeros((seq_len, ROT_HALF))
    c_head = np.concatenate([cos, cos, zeros + 1.0], axis=1)
    lo_head = np.concatenate([-sin, half0, zeros], axis=1)
    hi_head = np.concatenate([half0, sin, zeros], axis=1)
    tab = np.stack([c_head, lo_head, hi_head])
    return jnp.asarray(np.concatenate([tab] * (LANES // HEAD_DIM), axis=2), dtype=F32)


def _spread_matrix():
    row = jnp.arange(LANES)[:, None]
    col = jnp.arange(N_ATT_GROUPS * ATT_OUT_WIDTH)[None, :]
    group = col // ATT_OUT_WIDTH
    head = (col % ATT_OUT_WIDTH) // HEAD_DIM
    stat = head // 2 + jnp.where(head % 2 == 0, HEAD_DIM, 0)
    hit = jnp.zeros((LANES, N_ATT_GROUPS * ATT_OUT_WIDTH), jnp.bool_)
    for piece in range(SPLIT_PIECES):
        hit = hit | (row == stat + HEAD_PAIRS * (SPLIT_PIECES * group + piece))
    return hit.astype(BF16)


def _band_bias():
    kb = jnp.arange(2 * Q_BLOCK)[:, None]
    qi = jnp.arange(Q_BLOCK)[None, :]
    dist = qi + Q_BLOCK - kb
    valid = (dist >= 0) & (dist <= WINDOW_KEYS)
    both = jnp.stack([valid, valid & (kb >= Q_BLOCK)])
    return jnp.where(both, 0.0, MASKED).astype(BF16)


def _stacked_identity():
    row = jnp.arange(2 * Q_BLOCK)[:, None]
    col = jnp.arange(LANES)[None, :]
    return (row % Q_BLOCK == col).astype(BF16)


def kernel(x, norm_mix, w_in, w_pool_group, pool_scale, w_att_branch, w_pool_branch, w_out,
           norm_ffn, w_ffn_gate, w_ffn_up, w_ffn_down, norm_final):
    batch, seq_len, d_model = x.shape
    depth = w_in.shape[0]
    tokens = batch * seq_len
    assert depth == 1, "stacked layers need the un-normalised residual between layers"
    bias = _band_bias()

    h = x.reshape(tokens, d_model)
    later_weights = (w_att_branch[0], w_pool_branch[0], w_out[0], w_ffn_gate[0], w_ffn_up[0],
                     w_ffn_down[0])
    qkv0, qkv1, qkv2, pool_feat, gates, *later_bf16 = _in_proj(
        h, norm_mix[0][None, :], w_in[0].astype(BF16), w_pool_group[0].astype(BF16),
        pool_scale[0][None, :], _rope_tables(seq_len), later_weights, batch, seq_len)
    wab, wpb, wout, wgate, wup, wdown = later_bf16
    qkv_groups = (qkv0.reshape(batch, 1, seq_len, GROUP_QKV_WIDTH), qkv1, qkv2)
    eye = _stacked_identity()
    att = [_attention_group(qkv_groups[g], bias, eye, batch, seq_len, g)
           for g in range(N_ATT_GROUPS)]
    out = _merge_ffn(
        h, [o for _, o in att], [stats for stats, _ in att], pool_feat, gates,
        wab, wpb, wout, norm_ffn[0][None, :], wgate, wup, wdown, norm_final[None, :], seq_len)
    return out.reshape(batch, seq_len, d_model)
```

```python
import functools
import math

import jax
import jax.numpy as jnp
import numpy as np
from jax import lax
from jax.experimental import pallas as pl
from jax.experimental.pallas import tpu as pltpu

F32 = jnp.float32
BF16 = jnp.bfloat16

HEAD_DIM = 64
HEADS_PER_GROUP = 8
ATT_GROUPS = ((128, 1), (512, 4), (2048, 16))
N_ATT_GROUPS = len(ATT_GROUPS)
ATT_OUT_WIDTH = HEADS_PER_GROUP * HEAD_DIM
ATT_WIDTH = N_ATT_GROUPS * ATT_OUT_WIDTH
QKV_WIDTH = 3 * ATT_WIDTH
GROUP_QKV_WIDTH = 3 * ATT_OUT_WIDTH
ROT_DIM = HEAD_DIM // 4
ROT_HALF = ROT_DIM // 2
ROPE_THETA = 500000.0
POOL_WINDOWS = (2, 4, 8, 16)
POOL_GROUP_WIDTH = 128
POOL_WIDTH = len(POOL_WINDOWS) * POOL_GROUP_WIDTH
NORM_EPS = 1e-6
WINDOW_KEYS = 128
MASKED = -3.0e38
Q_SCALE = HEAD_DIM ** -0.5 * math.log2(math.e)

LANES = 128
BF16_SUBLANES = 16
MAX_STORE_STRIDE = 4
V7X_VMEM_BYTES = 64 * 1024 * 1024
V7X_VMEM_LIMIT_BYTES = V7X_VMEM_BYTES - 4 * 1024 * 1024

PROJ_ROWS = 512
PROJ_SUB_ROWS = 256
PROJ_COLS = ATT_OUT_WIDTH
assert PROJ_COLS == POOL_WIDTH
POOL_HALO = 16
ATT_ROWS = 2048
Q_BLOCK = 128
HEAD_PAIRS = HEADS_PER_GROUP // 2
SUM_LANE_OFFSET = 2 * HEAD_PAIRS
SCORE_LOOKAHEAD = 1
FFN_ROWS = 512
FFN_SUB_ROWS = 256
FFN_CHUNKS = ((0, 1024), (1024, 2048), (2048, 2816))
SPLIT_PIECES = 3


def _run_staggered(chains):
    live = list(enumerate(chains))
    tick = 0
    while live:
        for item in list(live):
            index, gen = item
            if tick >= index and next(gen, StopIteration) is StopIteration:
                live.remove(item)
        tick += 1


def _rms_norm(xf, gain):
    ms = jnp.mean(xf * xf, axis=-1, keepdims=True)
    return (xf * lax.rsqrt(ms + NORM_EPS)) * gain


def _in_proj_kernel(x_ref, gain_ref, w_ref, wpg_ref, pscale_ref, rope_ref, *rest,
                    tiles_per_seq, n_later):
    later_f32 = rest[:n_later]
    qkv0_ref, qkv1_ref, qkv2_ref, pool_ref, gate_ref = rest[n_later:n_later + 5]
    later_bf16 = rest[n_later + 5:2 * n_later + 5]
    (u_ref, u4_ref, u16_ref, uslab_ref, u4slab_ref, tab4_ref, tab16_ref,
     z0_ref, z1_ref, z2_ref, z3_ref) = rest[2 * n_later + 5:]
    rows, d_model = x_ref.shape
    seq_tile = pl.program_id(0) % tiles_per_seq
    n_slabs = d_model // LANES
    sub_rows = PROJ_SUB_ROWS
    n_sub = rows // sub_rows
    run4 = sub_rows // 4
    run16 = sub_rows // 16
    levels = (z0_ref, z1_ref, z2_ref, z3_ref)

    @pl.when(seq_tile == 0)
    def _():
        for ref in levels:
            ref[0:POOL_HALO, :] = jnp.zeros((POOL_HALO, POOL_WIDTH), F32)

    def project(lhs_ref, sub, col0):
        return jnp.dot(lhs_ref[sub], w_ref[:, col0:col0 + PROJ_COLS],
                       preferred_element_type=F32)

    gate0 = QKV_WIDTH + POOL_WIDTH
    for sub in range(n_sub):
        row0 = sub * sub_rows
        natural = slice(row0, row0 + sub_rows)
        uf = _rms_norm(x_ref[natural, :], gain_ref[...])
        u_ref[sub] = uf.astype(BF16)

        z0_ref[POOL_HALO + row0:POOL_HALO + row0 + sub_rows, :] = project(u_ref, sub, QKV_WIDTH)
        for c in range(gate_ref.shape[1] // PROJ_COLS):
            gate_ref[natural, c * PROJ_COLS:(c + 1) * PROJ_COLS] = (
                project(u_ref, sub, gate0 + c * PROJ_COLS))

        for s in range(n_slabs):
            uslab_ref[sub, s] = uf[:, s * LANES:(s + 1) * LANES]
        for r in range(4):
            dst = slice(r * run4, (r + 1) * run4)
            for s in range(n_slabs):
                piece = uslab_ref[sub, s, pl.ds(r, run4, stride=4), :]
                u4slab_ref[sub, s, dst, :] = piece
                u4_ref[sub, dst, s * LANES:(s + 1) * LANES] = piece.astype(BF16)
            for t in range(3):
                tab4_ref[sub, t, dst, :] = rope_ref[t, pl.ds(row0 + r, run4, stride=4), :]
        for run in range(16):
            src0 = (run // 4) * run4 + run % 4
            dst = slice(run * run16, (run + 1) * run16)
            for s in range(n_slabs):
                piece = u4slab_ref[sub, s, pl.ds(src0, run16, stride=4), :]
                u16_ref[sub, dst, s * LANES:(s + 1) * LANES] = piece.astype(BF16)
            for t in range(3):
                tab16_ref[sub, t, dst, :] = tab4_ref[sub, t, pl.ds(src0, run16, stride=4), :]

    cur = slice(POOL_HALO, POOL_HALO + rows)
    sums = []
    prev = z0_ref[cur, :]
    for lvl, ref in enumerate(levels):
        shift = 1 << lvl
        lo = lvl * POOL_GROUP_WIDTH
        shifted = ref[POOL_HALO - shift:POOL_HALO - shift + rows, lo:]
        prev = prev[:, (POOL_GROUP_WIDTH if lvl else 0):] + shifted
        sums.append(prev[:, 0:POOL_GROUP_WIDTH])
        if lvl + 1 < len(levels):
            levels[lvl + 1][cur, lo:] = prev
    pos = seq_tile * rows + lax.broadcasted_iota(jnp.int32, (rows, 1), 0)
    pooled = []
    for g, w in enumerate(POOL_WINDOWS):
        cols = slice(g * POOL_GROUP_WIDTH, (g + 1) * POOL_GROUP_WIDTH)
        count = jnp.minimum(pos + 1, w).astype(F32)
        pooled.append((sums[g] / count - z0_ref[cur, cols]).astype(BF16))
    for lvl, ref in enumerate(levels):
        lo = lvl * POOL_GROUP_WIDTH
        ref[0:POOL_HALO, lo:] = ref[rows:rows + POOL_HALO, lo:]

    def rope(xs, table):
        up = pltpu.roll(xs, LANES - ROT_HALF, axis=1)
        down = pltpu.roll(xs, ROT_HALF, axis=1)
        return xs * table(0) + up * table(1) + down * table(2)

    def store_natural(sub, col, part):
        qkv0_ref[sub * sub_rows:(sub + 1) * sub_rows, col:col + LANES] = part

    def store_by4(sub, col, part):
        for r in range(4):
            qkv1_ref[r, sub * run4:(sub + 1) * run4, col:col + LANES] = (
                part[r * run4:(r + 1) * run4])

    def store_by16(sub, col, part):
        for run in range(16):
            stream = 4 * (run % 4) + run // 4
            qkv2_ref[stream, sub * run16:(sub + 1) * run16, col:col + LANES] = (
                part[run * run16:(run + 1) * run16])

    for sub in range(n_sub):
        natural = slice(sub * sub_rows, (sub + 1) * sub_rows)
        variants = (
            (u_ref, lambda t: rope_ref[t, natural, :], store_natural),
            (u4_ref, lambda t: tab4_ref[sub, t], store_by4),
            (u16_ref, lambda t: tab16_ref[sub, t], store_by16),
        )
        for group, (lhs_ref, table, store) in enumerate(variants):
            for which in range(3):
                acc = project(lhs_ref, sub, which * ATT_WIDTH + group * ATT_OUT_WIDTH)
                for j in range(PROJ_COLS // LANES):
                    part = acc[:, j * LANES:(j + 1) * LANES]
                    if which == 0:
                        part = rope(part, table) * Q_SCALE
                    elif which == 1:
                        part = rope(part, table)
                    store(sub, which * ATT_OUT_WIDTH + j * LANES, part.astype(BF16))

    for g in range(len(POOL_WINDOWS)):
        cols = slice(g * POOL_GROUP_WIDTH, (g + 1) * POOL_GROUP_WIDTH)
        mapped = jnp.dot(pooled[g], wpg_ref[g], preferred_element_type=F32)
        pool_ref[:, cols] = (mapped * pscale_ref[:, cols]).astype(BF16)

    for src_ref, dst_ref in zip(later_f32, later_bf16):
        dst_ref[...] = src_ref[...].astype(BF16)


def _chunk_rows(total_rows, steps):
    chunk = BF16_SUBLANES * pl.cdiv(pl.cdiv(total_rows, steps), BF16_SUBLANES)
    while total_rows % chunk:
        chunk += BF16_SUBLANES
    return chunk


def _in_proj(x2, gain, w_in, w_pool_group, pool_scale, rope_tab, later_weights, batch, seq_len):
    tokens, d_model = x2.shape
    in_width = w_in.shape[1]
    gate_width = in_width - QKV_WIDTH - POOL_WIDTH
    rows = PROJ_ROWS
    n_sub = rows // PROJ_SUB_ROWS
    n_steps = tokens // rows
    tiles_per_seq = seq_len // rows
    const = dict(pipeline_mode=pl.Buffered(1))

    def stream_spec(dilation):
        return pl.BlockSpec((None, dilation, rows // dilation, GROUP_QKV_WIDTH),
                            lambda i: (i // tiles_per_seq, 0, i % tiles_per_seq, 0))

    def chunk_spec(weight):
        chunk = _chunk_rows(weight.shape[0], n_steps)
        last = weight.shape[0] // chunk - 1
        return pl.BlockSpec((chunk, weight.shape[1]), lambda i: (jnp.minimum(i, last), 0))

    later_specs = [chunk_spec(w) for w in later_weights]
    return pl.pallas_call(
        functools.partial(_in_proj_kernel, tiles_per_seq=tiles_per_seq,
                          n_later=len(later_weights)),
        name="in_proj",
        grid=(n_steps,),
        in_specs=[
            pl.BlockSpec((rows, d_model), lambda i: (i, 0)),
            pl.BlockSpec((1, d_model), lambda i: (0, 0), **const),
            pl.BlockSpec((d_model, in_width), lambda i: (0, 0), **const),
            pl.BlockSpec(w_pool_group.shape, lambda i: (0, 0, 0), **const),
            pl.BlockSpec((1, POOL_WIDTH), lambda i: (0, 0), **const),
            pl.BlockSpec((3, rows, LANES), lambda i: (0, i % tiles_per_seq, 0)),
        ] + later_specs,
        out_specs=[
            pl.BlockSpec((rows, GROUP_QKV_WIDTH), lambda i: (i, 0)),
            stream_spec(4),
            stream_spec(16),
            pl.BlockSpec((rows, POOL_WIDTH), lambda i: (i, 0)),
            pl.BlockSpec((rows, gate_width), lambda i: (i, 0)),
        ] + later_specs,
        out_shape=[
            jax.ShapeDtypeStruct((tokens, GROUP_QKV_WIDTH), BF16),
            jax.ShapeDtypeStruct((batch, 4, seq_len // 4, GROUP_QKV_WIDTH), BF16),
            jax.ShapeDtypeStruct((batch, 16, seq_len // 16, GROUP_QKV_WIDTH), BF16),
            jax.ShapeDtypeStruct((tokens, POOL_WIDTH), BF16),
            jax.ShapeDtypeStruct((tokens, gate_width), F32),
        ] + [jax.ShapeDtypeStruct(w.shape, BF16) for w in later_weights],
        scratch_shapes=[pltpu.VMEM((n_sub, PROJ_SUB_ROWS, d_model), BF16)] * 3
        + [pltpu.VMEM((n_sub, d_model // LANES, PROJ_SUB_ROWS, LANES), F32)] * 2
        + [pltpu.VMEM((n_sub, 3, PROJ_SUB_ROWS, LANES), F32)] * 2
        + [pltpu.VMEM((POOL_HALO + rows, POOL_WIDTH), F32) for _ in POOL_WINDOWS],
        compiler_params=pltpu.CompilerParams(
            dimension_semantics=("arbitrary",), vmem_limit_bytes=V7X_VMEM_LIMIT_BYTES),
    )(x2, gain, w_in, w_pool_group, pool_scale, rope_tab, *later_weights)


def _stat_lane(head):
    return head // 2 + (HEAD_DIM if head % 2 == 0 else 0)


def _store_planes(dilation):
    return max(1, dilation // MAX_STORE_STRIDE)


def _stat_tiles(dilation):
    return 2 if dilation == 1 else 1


def _sum_lane_offset(n_stat_tiles):
    return SUM_LANE_OFFSET if n_stat_tiles == 1 else 0


def _multi_plane_groups():
    return [g for g, (_, dilation) in enumerate(ATT_GROUPS) if _store_planes(dilation) > 1]


def _piece_shift(group, piece):
    return HEAD_PAIRS * (SPLIT_PIECES * group + piece)


def _attention_kernel(q_ref, k_ref, v_ref, bias_ref, eye_ref, *rest, single_step, n_stat_tiles):
    stat_refs = rest[:n_stat_tiles]
    o_ref, kcarry_ref, vaug_ref = rest[n_stat_tiles:]
    dilation, rows, _ = q_ref.shape
    first = pl.program_id(1) == 0

    lane = lax.broadcasted_iota(jnp.int32, (Q_BLOCK, LANES), 1)
    low_half = lane < HEAD_DIM
    head_mask = (jnp.where(low_half, 1.0, 0.0).astype(BF16),
                 jnp.where(low_half, 0.0, 1.0).astype(BF16))
    low_rows = lax.broadcasted_iota(jnp.int32, (rows, LANES), 1) < HEAD_DIM
    one = jnp.ones((rows, LANES), BF16)
    nt_dims = (((1,), (1,)), ((), ()))
    zero_block = jnp.zeros((Q_BLOCK, ATT_OUT_WIDTH), BF16)

    def clear_carry():
        for st in range(dilation):
            kcarry_ref[st, 0:Q_BLOCK, :] = zero_block
            vaug_ref[0, st, 0:Q_BLOCK, :] = zero_block
            vaug_ref[1, st, 0:Q_BLOCK, :] = zero_block

    if single_step:
        clear_carry()
    else:
        pl.when(first)(clear_carry)

    for st in range(dilation):
        kcarry_ref[st, Q_BLOCK:2 * Q_BLOCK, :] = k_ref[st, 0:Q_BLOCK, :]
        for pair in range(HEAD_PAIRS):
            cols = slice(pair * LANES, (pair + 1) * LANES)
            v_pair = v_ref[st, :, cols]
            vaug_ref[0, st, Q_BLOCK:Q_BLOCK + rows, cols] = jnp.where(low_rows, v_pair, one)
            vaug_ref[1, st, Q_BLOCK:Q_BLOCK + rows, cols] = jnp.where(low_rows, one, v_pair)

    def scores(st, sub, pair):
        cols = slice(pair * LANES, (pair + 1) * LANES)
        q_pair = q_ref[st, sub * Q_BLOCK:(sub + 1) * Q_BLOCK, cols]
        if sub == 0:
            k_pair = kcarry_ref[st, :, cols]
            bias_idx = 1 if single_step else jnp.where(first, 1, 0)
        else:
            k_pair = k_ref[st, (sub - 1) * Q_BLOCK:(sub + 1) * Q_BLOCK, cols]
            bias_idx = 0
        q_both = jnp.concatenate([q_pair * head_mask[0], q_pair * head_mask[1]], axis=0)
        q_both = jnp.concatenate([q_both, eye_ref[...]], axis=1)
        k_aug = jnp.concatenate([k_pair, bias_ref[bias_idx]], axis=1)
        return lax.dot_general(q_both, k_aug, nt_dims, preferred_element_type=F32)

    def finish(st, sub, pair, s_both, stats):
        cols = slice(pair * LANES, (pair + 1) * LANES)
        band = slice(sub * Q_BLOCK, (sub + 2) * Q_BLOCK)
        planes = _store_planes(dilation)
        stride = dilation // planes
        start = sub * Q_BLOCK * stride + st // planes
        tokens = pl.ds(start, Q_BLOCK, stride=stride) if stride > 1 else pl.ds(start, Q_BLOCK)
        plane = st % planes
        pvs = []
        for half in range(2):
            s = s_both[half * Q_BLOCK:(half + 1) * Q_BLOCK]
            m = jnp.max(s, axis=-1, keepdims=True)
            p = jnp.exp2(s - m).astype(BF16)
            pv = jnp.dot(p, vaug_ref[half, st, band, cols], preferred_element_type=F32)
            pvs.append(pv)
            stat_lane = _stat_lane(2 * pair + half)
            stats[0] = jnp.where(lane == stat_lane, m, stats[0])
            stats[-1] = jnp.where(lane == stat_lane + _sum_lane_offset(len(stats)), pv, stats[-1])
        o_ref[pair, plane, tokens, :] = jnp.where(low_half, pvs[0], pvs[1])
        if pair == HEAD_PAIRS - 1:
            for stat_ref, tile in zip(stat_refs, stats):
                stat_ref[plane, tokens, :] = tile
        return stats

    units = [(st, sub, pair) for st in range(dilation) for sub in range(rows // Q_BLOCK)
             for pair in range(HEAD_PAIRS)]
    ahead = [scores(*unit) for unit in units[:SCORE_LOOKAHEAD]]
    stats = None
    for index, unit in enumerate(units):
        s_both = ahead.pop(0)
        if index + SCORE_LOOKAHEAD < len(units):
            ahead.append(scores(*units[index + SCORE_LOOKAHEAD]))
        if unit[2] == 0:
            stats = [jnp.ones((Q_BLOCK, LANES), F32) for _ in stat_refs]
        stats = finish(*unit, s_both, stats)

    for st in range(dilation):
        kcarry_ref[st, 0:Q_BLOCK, :] = k_ref[st, rows - Q_BLOCK:rows, :]
        for half in range(2):
            vaug_ref[half, st, 0:Q_BLOCK, :] = vaug_ref[half, st, rows:rows + Q_BLOCK, :]


def _attention_group(qkv, bias, eye, batch, seq_len, group):
    _, dilation = ATT_GROUPS[group]
    stream_len = seq_len // dilation
    rows = max(Q_BLOCK, ATT_ROWS // dilation)
    planes = _store_planes(dilation)
    plane_rows = rows * dilation // planes
    single_step = rows == stream_len

    def qkv_spec(which):
        return pl.BlockSpec((None, dilation, rows, ATT_OUT_WIDTH), lambda b, m: (b, 0, m, which))

    stat_spec = pl.BlockSpec((None, planes, plane_rows, LANES), lambda b, m: (b, 0, m, 0))
    stat_shape = jax.ShapeDtypeStruct((batch, planes, seq_len // planes, LANES), F32)
    n_stat_tiles = _stat_tiles(dilation)
    return pl.pallas_call(
        functools.partial(_attention_kernel, single_step=single_step,
                          n_stat_tiles=n_stat_tiles),
        name=f"attention_g{group}",
        grid=(batch, stream_len // rows),
        in_specs=[
            qkv_spec(0), qkv_spec(1), qkv_spec(2),
            pl.BlockSpec(bias.shape, lambda b, m: (0, 0, 0), pipeline_mode=pl.Buffered(1)),
            pl.BlockSpec(eye.shape, lambda b, m: (0, 0), pipeline_mode=pl.Buffered(1)),
        ],
        out_specs=[stat_spec] * n_stat_tiles + [
            pl.BlockSpec((None, HEAD_PAIRS, planes, plane_rows, LANES),
                         lambda b, m: (b, 0, 0, m, 0)),
        ],
        out_shape=[stat_shape] * n_stat_tiles + [
            jax.ShapeDtypeStruct((batch, HEAD_PAIRS, planes, seq_len // planes, LANES), F32),
        ],
        scratch_shapes=[
            pltpu.VMEM((dilation, 2 * Q_BLOCK, ATT_OUT_WIDTH), BF16),
            pltpu.VMEM((2, dilation, Q_BLOCK + rows, ATT_OUT_WIDTH), BF16),
        ],
        compiler_params=pltpu.CompilerParams(
            dimension_semantics=("arbitrary", "arbitrary"),
            vmem_limit_bytes=V7X_VMEM_LIMIT_BYTES),
    )(qkv, qkv, qkv, bias, eye)


def _merge_ffn_kernel(x_ref, o0_ref, o1_ref, o2_ref, *rest):
    stat_refs = []
    for _, dilation in ATT_GROUPS:
        stat_refs.append(rest[:_stat_tiles(dilation)])
        rest = rest[_stat_tiles(dilation):]
    (pool_ref, gate_ref, spread_ref,
     wab_ref, wpb_ref, wout_ref, gffn_ref, wg_ref, wu_ref, wd_ref, gfin_ref,
     out_ref, att_ref, onat_ref, snat_ref) = rest
    d_model = x_ref.shape[1]
    o_refs = (o0_ref, o1_ref, o2_ref)
    lane = lax.broadcasted_iota(jnp.int32, (FFN_SUB_ROWS, LANES), 1)
    stat_lanes = (lane % HEAD_DIM) < HEAD_PAIRS

    slots = {g: slot for slot, g in enumerate(_multi_plane_groups())}

    def o_dst(g, pair):
        return onat_ref.at[slots[g], pair] if g in slots else None

    def stat_dst(g):
        return snat_ref.at[slots[g]] if g in slots else None

    def chain(sub):
        rows = slice(sub * FFN_SUB_ROWS, (sub + 1) * FFN_SUB_ROWS)

        def in_token_order(src_ref, dst_ref):
            planes = src_ref.shape[0]
            if planes == 1:
                return src_ref[0, rows, :]
            per_plane = FFN_SUB_ROWS // planes
            for plane in range(planes):
                dst_ref[pl.ds(sub * FFN_SUB_ROWS + plane, per_plane, stride=planes), :] = (
                    src_ref[plane, sub * per_plane:(sub + 1) * per_plane, :])
            return dst_ref[rows, :]

        ms, ls = [], []
        for g, tiles in enumerate(stat_refs):
            ms.append(in_token_order(tiles[0], stat_dst(g)))
            offset = _sum_lane_offset(len(tiles))
            if offset:
                ls.append(pltpu.roll(ms[-1], LANES - offset, axis=1))
            else:
                ls.append(in_token_order(tiles[-1], stat_dst(g)))
        top = jnp.maximum(jnp.maximum(ms[0], ms[1]), ms[2])
        es = [jnp.exp2(m - top) for m in ms]
        den = es[0] * ls[0] + es[1] * ls[1] + es[2] * ls[2]

        packed = jnp.zeros((FFN_SUB_ROWS, LANES), F32)
        for g in range(N_ATT_GROUPS):
            rest = jnp.where(stat_lanes, es[g] / den, 0.0)
            for piece in range(SPLIT_PIECES):
                part = rest.astype(BF16).astype(F32)
                rest = rest - part
                shift = _piece_shift(g, piece)
                packed = packed + (pltpu.roll(part, shift, axis=1) if shift else part)
        spread = jnp.dot(packed.astype(BF16), spread_ref[...], preferred_element_type=F32)
        yield

        for pair in range(HEAD_PAIRS):
            acc = jnp.zeros((FFN_SUB_ROWS, LANES), F32)
            for g in range(N_ATT_GROUPS):
                col = g * ATT_OUT_WIDTH + pair * LANES
                o_pair = in_token_order(o_refs[g].at[pair], o_dst(g, pair))
                acc = acc + spread[:, col:col + LANES] * o_pair
            att_ref[rows, pair * LANES:(pair + 1) * LANES] = acc.astype(BF16)
        yield

        y_att = jnp.dot(att_ref[rows, :], wab_ref[...], preferred_element_type=F32)
        y_pool = jnp.dot(pool_ref[rows, :], wpb_ref[...], preferred_element_type=F32)
        yield
        merged = (jax.nn.sigmoid(gate_ref[rows, 0:d_model]) * y_att
                  + jax.nn.sigmoid(gate_ref[rows, d_model:2 * d_model]) * y_pool)
        yield
        h = x_ref[rows, :] + jnp.dot(merged.astype(BF16), wout_ref[...],
                                     preferred_element_type=F32)
        yield
        f = _rms_norm(h, gffn_ref[...]).astype(BF16)
        yield
        ffn = jnp.zeros((FFN_SUB_ROWS, d_model), F32)
        for lo, hi in FFN_CHUNKS:
            gate = jnp.dot(f, wg_ref[:, lo:hi], preferred_element_type=F32)
            up = jnp.dot(f, wu_ref[:, lo:hi], preferred_element_type=F32)
            yield
            hidden = (jax.nn.silu(gate) * up).astype(BF16)
            yield
            ffn = ffn + jnp.dot(hidden, wd_ref[lo:hi, :], preferred_element_type=F32)
            yield
        out_ref[rows, :] = _rms_norm(h + ffn, gfin_ref[...])

    _run_staggered([chain(sub) for sub in range(x_ref.shape[0] // FFN_SUB_ROWS)])


def _merge_ffn(x2, outs, stats, pool_feat, gates, w_att_branch, w_pool_branch, w_out, norm_ffn,
               w_gate, w_up, w_down, norm_final, seq_len):
    tokens, d_model = x2.shape
    rows = FFN_ROWS
    tiles_per_seq = seq_len // rows

    def tile(width):
        return pl.BlockSpec((rows, width), lambda i: (i, 0))

    def whole(arr):
        return pl.BlockSpec(arr.shape, lambda i: (0, 0), pipeline_mode=pl.Buffered(1))

    def o_spec(planes):
        return pl.BlockSpec((None, HEAD_PAIRS, planes, rows // planes, LANES),
                            lambda i: (i // tiles_per_seq, 0, 0, i % tiles_per_seq, 0))

    def stat_spec(planes):
        return pl.BlockSpec((None, planes, rows // planes, LANES),
                            lambda i: (i // tiles_per_seq, 0, i % tiles_per_seq, 0))

    group_planes = [_store_planes(dilation) for _, dilation in ATT_GROUPS]
    n_slots = max(1, len(_multi_plane_groups()))
    weights = (_spread_matrix(), w_att_branch, w_pool_branch, w_out, norm_ffn, w_gate, w_up,
               w_down, norm_final)
    return pl.pallas_call(
        _merge_ffn_kernel,
        name="merge_ffn",
        grid=(tokens // rows,),
        in_specs=[tile(d_model)]
        + [o_spec(planes) for planes in group_planes]
        + [stat_spec(planes) for planes, tiles in zip(group_planes, stats) for _ in tiles]
        + [tile(POOL_WIDTH), tile(gates.shape[1])]
        + [whole(w) for w in weights],
        out_specs=tile(d_model),
        out_shape=jax.ShapeDtypeStruct((tokens, d_model), F32),
        scratch_shapes=[
            pltpu.VMEM((rows, ATT_OUT_WIDTH), BF16),
            pltpu.VMEM((n_slots, HEAD_PAIRS, rows, LANES), F32),
            pltpu.VMEM((n_slots, rows, LANES), F32),
        ],
        compiler_params=pltpu.CompilerParams(
            dimension_semantics=("arbitrary",), vmem_limit_bytes=V7X_VMEM_LIMIT_BYTES),
    )(x2, *outs, *[tile for tiles in stats for tile in tiles], pool_feat, gates, *weights)


def _rope_tables(seq_len):
    inv_freq = ROPE_THETA ** (-np.arange(0, ROT_DIM, 2, dtype=np.float64) / ROT_DIM)
    ang = np.arange(seq_len, dtype=np.float64)[:, None] * inv_freq[None, :]
    cos, sin = np.cos(ang), np.sin(ang)
    zeros = np.zeros((seq_len, HEAD_DIM - ROT_DIM))
    half0 = np.zeros((seq_len, ROT_HALF))
    c_head = np.concatenate([cos, cos, zeros + 1.0], axis=1)
    lo_head = np.concatenate([-sin, half0, zeros], axis=1)
    hi_head = np.concatenate([half0, sin, zeros], axis=1)
    tab = np.stack([c_head, lo_head, hi_head])
    return jnp.asarray(np.concatenate([tab] * (LANES // HEAD_DIM), axis=2), dtype=F32)


def _spread_matrix():
    row = jnp.arange(LANES)[:, None]
    col = jnp.arange(N_ATT_GROUPS * ATT_OUT_WIDTH)[None, :]
    group = col // ATT_OUT_WIDTH
    head = (col % ATT_OUT_WIDTH) // HEAD_DIM
    stat = head // 2 + jnp.where(head % 2 == 0, HEAD_DIM, 0)
    hit = jnp.zeros((LANES, N_ATT_GROUPS * ATT_OUT_WIDTH), jnp.bool_)
    for piece in range(SPLIT_PIECES):
        hit = hit | (row == stat + HEAD_PAIRS * (SPLIT_PIECES * group + piece))
    return hit.astype(BF16)


def _band_bias():
    kb = jnp.arange(2 * Q_BLOCK)[:, None]
    qi = jnp.arange(Q_BLOCK)[None, :]
    dist = qi + Q_BLOCK - kb
    valid = (dist >= 0) & (dist <= WINDOW_KEYS)
    both = jnp.stack([valid, valid & (kb >= Q_BLOCK)])
    return jnp.where(both, 0.0, MASKED).astype(BF16)


def _stacked_identity():
    row = jnp.arange(2 * Q_BLOCK)[:, None]
    col = jnp.arange(LANES)[None, :]
    return (row % Q_BLOCK == col).astype(BF16)


def kernel(x, norm_mix, w_in, w_pool_group, pool_scale, w_att_branch, w_pool_branch, w_out,
           norm_ffn, w_ffn_gate, w_ffn_up, w_ffn_down, norm_final):
    batch, seq_len, d_model = x.shape
    depth = w_in.shape[0]
    tokens = batch * seq_len
    assert depth == 1, "stacked layers need the un-normalised residual between layers"
    bias = _band_bias()

    h = x.reshape(tokens, d_model)
    later_weights = (w_att_branch[0], w_pool_branch[0], w_out[0], w_ffn_gate[0], w_ffn_up[0],
                     w_ffn_down[0])
    qkv0, qkv1, qkv2, pool_feat, gates, *later_bf16 = _in_proj(
        h, norm_mix[0][None, :], w_in[0].astype(BF16), w_pool_group[0].astype(BF16),
        pool_scale[0][None, :], _rope_tables(seq_len), later_weights, batch, seq_len)
    wab, wpb, wout, wgate, wup, wdown = later_bf16
    qkv_groups = (qkv0.reshape(batch, 1, seq_len, GROUP_QKV_WIDTH), qkv1, qkv2)
    eye = _stacked_identity()
    att = [_attention_group(qkv_groups[g], bias, eye, batch, seq_len, g)
           for g in range(N_ATT_GROUPS)]
    out = _merge_ffn(
        h, [a[-1] for a in att], [a[:-1] for a in att], pool_feat, gates,
        wab, wpb, wout, norm_ffn[0][None, :], wgate, wup, wdown, norm_final[None, :], seq_len)
    return out.reshape(batch, seq_len, d_model)
```

```python
import functools
import math

import jax
import jax.numpy as jnp
import numpy as np
from jax import lax
from jax.experimental import pallas as pl
from jax.experimental.pallas import tpu as pltpu

F32 = jnp.float32
BF16 = jnp.bfloat16

HEAD_DIM = 64
HEADS_PER_GROUP = 8
ATT_GROUPS = ((128, 1), (512, 4), (2048, 16))
N_ATT_GROUPS = len(ATT_GROUPS)
ATT_OUT_WIDTH = HEADS_PER_GROUP * HEAD_DIM
ATT_WIDTH = N_ATT_GROUPS * ATT_OUT_WIDTH
QKV_WIDTH = 3 * ATT_WIDTH
GROUP_QKV_WIDTH = 3 * ATT_OUT_WIDTH
ROT_DIM = HEAD_DIM // 4
ROT_HALF = ROT_DIM // 2
ROPE_THETA = 500000.0
POOL_WINDOWS = (2, 4, 8, 16)
POOL_GROUP_WIDTH = 128
POOL_WIDTH = len(POOL_WINDOWS) * POOL_GROUP_WIDTH
NORM_EPS = 1e-6
WINDOW_KEYS = 128
MASKED = -3.0e38
Q_SCALE = HEAD_DIM ** -0.5 * math.log2(math.e)

LANES = 128
BF16_SUBLANES = 16
MAX_STORE_STRIDE = 4
V7X_VMEM_BYTES = 64 * 1024 * 1024
V7X_VMEM_LIMIT_BYTES = V7X_VMEM_BYTES - 4 * 1024 * 1024

PROJ_ROWS = 512
PROJ_SUB_ROWS = 256
PROJ_COLS = ATT_OUT_WIDTH
assert PROJ_COLS == POOL_WIDTH
POOL_HALO = 16
ATT_ROWS = 2048
Q_BLOCK = 128
HEAD_PAIRS = HEADS_PER_GROUP // 2
SUM_LANE_OFFSET = 2 * HEAD_PAIRS
SCORE_LOOKAHEAD = 1
FFN_ROWS = 512
FFN_SUB_ROWS = 256
FFN_CHUNKS = ((0, 1024), (1024, 2048), (2048, 2816))
SPLIT_PIECES = 3


def _run_staggered(chains):
    live = list(enumerate(chains))
    tick = 0
    while live:
        for item in list(live):
            index, gen = item
            if tick >= index and next(gen, StopIteration) is StopIteration:
                live.remove(item)
        tick += 1


def _rms_norm(xf, gain):
    ms = jnp.mean(xf * xf, axis=-1, keepdims=True)
    return (xf * lax.rsqrt(ms + NORM_EPS)) * gain


def _in_proj_kernel(x_ref, gain_ref, w_ref, wpg_ref, pscale_ref, rope_ref, *rest,
                    tiles_per_seq, n_later):
    later_f32 = rest[:n_later]
    qkv0_ref, qkv1_ref, qkv2_ref, pool_ref, gate_ref = rest[n_later:n_later + 5]
    later_bf16 = rest[n_later + 5:2 * n_later + 5]
    (u_ref, u4_ref, u16_ref, uslab_ref, u4slab_ref, tab4_ref, tab16_ref,
     z0_ref, z1_ref, z2_ref, z3_ref) = rest[2 * n_later + 5:]
    rows, d_model = x_ref.shape
    seq_tile = pl.program_id(0) % tiles_per_seq
    n_slabs = d_model // LANES
    sub_rows = PROJ_SUB_ROWS
    n_sub = rows // sub_rows
    run4 = sub_rows // 4
    run16 = sub_rows // 16
    levels = (z0_ref, z1_ref, z2_ref, z3_ref)

    @pl.when(seq_tile == 0)
    def _():
        for ref in levels:
            ref[0:POOL_HALO, :] = jnp.zeros((POOL_HALO, POOL_WIDTH), F32)

    def project(lhs_ref, sub, col0):
        return jnp.dot(lhs_ref[sub], w_ref[:, col0:col0 + PROJ_COLS],
                       preferred_element_type=F32)

    gate0 = QKV_WIDTH + POOL_WIDTH
    for sub in range(n_sub):
        row0 = sub * sub_rows
        natural = slice(row0, row0 + sub_rows)
        uf = _rms_norm(x_ref[natural, :], gain_ref[...])
        u_ref[sub] = uf.astype(BF16)

        z0_ref[POOL_HALO + row0:POOL_HALO + row0 + sub_rows, :] = project(u_ref, sub, QKV_WIDTH)
        for c in range(gate_ref.shape[1] // PROJ_COLS):
            gate_ref[natural, c * PROJ_COLS:(c + 1) * PROJ_COLS] = (
                project(u_ref, sub, gate0 + c * PROJ_COLS))

        for s in range(n_slabs):
            uslab_ref[sub, s] = uf[:, s * LANES:(s + 1) * LANES]
        for r in range(4):
            dst = slice(r * run4, (r + 1) * run4)
            for s in range(n_slabs):
                piece = uslab_ref[sub, s, pl.ds(r, run4, stride=4), :]
                u4slab_ref[sub, s, dst, :] = piece
                u4_ref[sub, dst, s * LANES:(s + 1) * LANES] = piece.astype(BF16)
            for t in range(3):
                tab4_ref[sub, t, dst, :] = rope_ref[t, pl.ds(row0 + r, run4, stride=4), :]
        for run in range(16):
            src0 = (run // 4) * run4 + run % 4
            dst = slice(run * run16, (run + 1) * run16)
            for s in range(n_slabs):
                piece = u4slab_ref[sub, s, pl.ds(src0, run16, stride=4), :]
                u16_ref[sub, dst, s * LANES:(s + 1) * LANES] = piece.astype(BF16)
            for t in range(3):
                tab16_ref[sub, t, dst, :] = tab4_ref[sub, t, pl.ds(src0, run16, stride=4), :]

    cur = slice(POOL_HALO, POOL_HALO + rows)
    sums = []
    prev = z0_ref[cur, :]
    for lvl, ref in enumerate(levels):
        shift = 1 << lvl
        lo = lvl * POOL_GROUP_WIDTH
        shifted = ref[POOL_HALO - shift:POOL_HALO - shift + rows, lo:]
        prev = prev[:, (POOL_GROUP_WIDTH if lvl else 0):] + shifted
        sums.append(prev[:, 0:POOL_GROUP_WIDTH])
        if lvl + 1 < len(levels):
            levels[lvl + 1][cur, lo:] = prev
    pos = seq_tile * rows + lax.broadcasted_iota(jnp.int32, (rows, 1), 0)
    pooled = []
    for g, w in enumerate(POOL_WINDOWS):
        cols = slice(g * POOL_GROUP_WIDTH, (g + 1) * POOL_GROUP_WIDTH)
        count = jnp.minimum(pos + 1, w).astype(F32)
        pooled.append((sums[g] / count - z0_ref[cur, cols]).astype(BF16))
    for lvl, ref in enumerate(levels):
        lo = lvl * POOL_GROUP_WIDTH
        ref[0:POOL_HALO, lo:] = ref[rows:rows + POOL_HALO, lo:]

    def rope(xs, table):
        up = pltpu.roll(xs, LANES - ROT_HALF, axis=1)
        down = pltpu.roll(xs, ROT_HALF, axis=1)
        return xs * table(0) + up * table(1) + down * table(2)

    def store_natural(sub, col, part):
        qkv0_ref[sub * sub_rows:(sub + 1) * sub_rows, col:col + LANES] = part

    def store_by4(sub, col, part):
        for r in range(4):
            qkv1_ref[r, sub * run4:(sub + 1) * run4, col:col + LANES] = (
                part[r * run4:(r + 1) * run4])

    def store_by16(sub, col, part):
        for run in range(16):
            stream = 4 * (run % 4) + run // 4
            qkv2_ref[stream, sub * run16:(sub + 1) * run16, col:col + LANES] = (
                part[run * run16:(run + 1) * run16])

    def qkv_chain(sub):
        natural = slice(sub * sub_rows, (sub + 1) * sub_rows)
        variants = (
            (u_ref, lambda t: rope_ref[t, natural, :], store_natural),
            (u4_ref, lambda t: tab4_ref[sub, t], store_by4),
            (u16_ref, lambda t: tab16_ref[sub, t], store_by16),
        )
        for group, (lhs_ref, table, store) in enumerate(variants):
            for which in range(3):
                acc = project(lhs_ref, sub, which * ATT_WIDTH + group * ATT_OUT_WIDTH)
                for j in range(PROJ_COLS // LANES):
                    part = acc[:, j * LANES:(j + 1) * LANES]
                    if which == 0:
                        part = rope(part, table) * Q_SCALE
                    elif which == 1:
                        part = rope(part, table)
                    store(sub, which * ATT_OUT_WIDTH + j * LANES, part.astype(BF16))
                yield

    _run_staggered([qkv_chain(sub) for sub in range(n_sub)])

    for g in range(len(POOL_WINDOWS)):
        cols = slice(g * POOL_GROUP_WIDTH, (g + 1) * POOL_GROUP_WIDTH)
        mapped = jnp.dot(pooled[g], wpg_ref[g], preferred_element_type=F32)
        pool_ref[:, cols] = (mapped * pscale_ref[:, cols]).astype(BF16)

    for src_ref, dst_ref in zip(later_f32, later_bf16):
        dst_ref[...] = src_ref[...].astype(BF16)


def _chunk_rows(total_rows, steps):
    chunk = BF16_SUBLANES * pl.cdiv(pl.cdiv(total_rows, steps), BF16_SUBLANES)
    while total_rows % chunk:
        chunk += BF16_SUBLANES
    return chunk


def _in_proj(x2, gain, w_in, w_pool_group, pool_scale, rope_tab, later_weights, batch, seq_len):
    tokens, d_model = x2.shape
    in_width = w_in.shape[1]
    gate_width = in_width - QKV_WIDTH - POOL_WIDTH
    rows = PROJ_ROWS
    n_sub = rows // PROJ_SUB_ROWS
    n_steps = tokens // rows
    tiles_per_seq = seq_len // rows
    const = dict(pipeline_mode=pl.Buffered(1))

    def stream_spec(dilation):
        return pl.BlockSpec((None, dilation, rows // dilation, GROUP_QKV_WIDTH),
                            lambda i: (i // tiles_per_seq, 0, i % tiles_per_seq, 0))

    def chunk_spec(weight):
        chunk = _chunk_rows(weight.shape[0], n_steps)
        last = weight.shape[0] // chunk - 1
        return pl.BlockSpec((chunk, weight.shape[1]), lambda i: (jnp.minimum(i, last), 0))

    later_specs = [chunk_spec(w) for w in later_weights]
    return pl.pallas_call(
        functools.partial(_in_proj_kernel, tiles_per_seq=tiles_per_seq,
                          n_later=len(later_weights)),
        name="in_proj",
        grid=(n_steps,),
        in_specs=[
            pl.BlockSpec((rows, d_model), lambda i: (i, 0)),
            pl.BlockSpec((1, d_model), lambda i: (0, 0), **const),
            pl.BlockSpec((d_model, in_width), lambda i: (0, 0), **const),
            pl.BlockSpec(w_pool_group.shape, lambda i: (0, 0, 0), **const),
            pl.BlockSpec((1, POOL_WIDTH), lambda i: (0, 0), **const),
            pl.BlockSpec((3, rows, LANES), lambda i: (0, i % tiles_per_seq, 0)),
        ] + later_specs,
        out_specs=[
            pl.BlockSpec((rows, GROUP_QKV_WIDTH), lambda i: (i, 0)),
            stream_spec(4),
            stream_spec(16),
            pl.BlockSpec((rows, POOL_WIDTH), lambda i: (i, 0)),
            pl.BlockSpec((rows, gate_width), lambda i: (i, 0)),
        ] + later_specs,
        out_shape=[
            jax.ShapeDtypeStruct((tokens, GROUP_QKV_WIDTH), BF16),
            jax.ShapeDtypeStruct((batch, 4, seq_len // 4, GROUP_QKV_WIDTH), BF16),
            jax.ShapeDtypeStruct((batch, 16, seq_len // 16, GROUP_QKV_WIDTH), BF16),
            jax.ShapeDtypeStruct((tokens, POOL_WIDTH), BF16),
            jax.ShapeDtypeStruct((tokens, gate_width), F32),
        ] + [jax.ShapeDtypeStruct(w.shape, BF16) for w in later_weights],
        scratch_shapes=[pltpu.VMEM((n_sub, PROJ_SUB_ROWS, d_model), BF16)] * 3
        + [pltpu.VMEM((n_sub, d_model // LANES, PROJ_SUB_ROWS, LANES), F32)] * 2
        + [pltpu.VMEM((n_sub, 3, PROJ_SUB_ROWS, LANES), F32)] * 2
        + [pltpu.VMEM((POOL_HALO + rows, POOL_WIDTH), F32) for _ in POOL_WINDOWS],
        compiler_params=pltpu.CompilerParams(
            dimension_semantics=("arbitrary",), vmem_limit_bytes=V7X_VMEM_LIMIT_BYTES),
    )(x2, gain, w_in, w_pool_group, pool_scale, rope_tab, *later_weights)


def _stat_lane(head):
    return head // 2 + (HEAD_DIM if head % 2 == 0 else 0)


def _store_planes(dilation):
    return max(1, dilation // MAX_STORE_STRIDE)


def _stat_tiles(dilation):
    return 2 if dilation == 1 else 1


def _sum_lane_offset(n_stat_tiles):
    return SUM_LANE_OFFSET if n_stat_tiles == 1 else 0


def _multi_plane_groups():
    return [g for g, (_, dilation) in enumerate(ATT_GROUPS) if _store_planes(dilation) > 1]


def _piece_shift(group, piece):
    return HEAD_PAIRS * (SPLIT_PIECES * group + piece)


def _attention_kernel(q_ref, k_ref, v_ref, bias_ref, eye_ref, *rest, single_step, n_stat_tiles):
    stat_refs = rest[:n_stat_tiles]
    o_ref, kcarry_ref, vaug_ref = rest[n_stat_tiles:]
    dilation, rows, _ = q_ref.shape
    first = pl.program_id(1) == 0

    lane = lax.broadcasted_iota(jnp.int32, (Q_BLOCK, LANES), 1)
    low_half = lane < HEAD_DIM
    head_mask = (jnp.where(low_half, 1.0, 0.0).astype(BF16),
                 jnp.where(low_half, 0.0, 1.0).astype(BF16))
    low_rows = lax.broadcasted_iota(jnp.int32, (rows, LANES), 1) < HEAD_DIM
    one = jnp.ones((rows, LANES), BF16)
    nt_dims = (((1,), (1,)), ((), ()))
    zero_block = jnp.zeros((Q_BLOCK, ATT_OUT_WIDTH), BF16)

    def clear_carry():
        for st in range(dilation):
            kcarry_ref[st, 0:Q_BLOCK, :] = zero_block
            vaug_ref[0, st, 0:Q_BLOCK, :] = zero_block
            vaug_ref[1, st, 0:Q_BLOCK, :] = zero_block

    if single_step:
        clear_carry()
    else:
        pl.when(first)(clear_carry)

    for st in range(dilation):
        kcarry_ref[st, Q_BLOCK:2 * Q_BLOCK, :] = k_ref[st, 0:Q_BLOCK, :]
        for pair in range(HEAD_PAIRS):
            cols = slice(pair * LANES, (pair + 1) * LANES)
            v_pair = v_ref[st, :, cols]
            vaug_ref[0, st, Q_BLOCK:Q_BLOCK + rows, cols] = jnp.where(low_rows, v_pair, one)
            vaug_ref[1, st, Q_BLOCK:Q_BLOCK + rows, cols] = jnp.where(low_rows, one, v_pair)

    def scores(st, sub, pair):
        cols = slice(pair * LANES, (pair + 1) * LANES)
        q_pair = q_ref[st, sub * Q_BLOCK:(sub + 1) * Q_BLOCK, cols]
        if sub == 0:
            k_pair = kcarry_ref[st, :, cols]
            bias_idx = 1 if single_step else jnp.where(first, 1, 0)
        else:
            k_pair = k_ref[st, (sub - 1) * Q_BLOCK:(sub + 1) * Q_BLOCK, cols]
            bias_idx = 0
        q_both = jnp.concatenate([q_pair * head_mask[0], q_pair * head_mask[1]], axis=0)
        q_both = jnp.concatenate([q_both, eye_ref[...]], axis=1)
        k_aug = jnp.concatenate([k_pair, bias_ref[bias_idx]], axis=1)
        return lax.dot_general(q_both, k_aug, nt_dims, preferred_element_type=F32)

    def finish(st, sub, pair, s_both, stats):
        cols = slice(pair * LANES, (pair + 1) * LANES)
        band = slice(sub * Q_BLOCK, (sub + 2) * Q_BLOCK)
        planes = _store_planes(dilation)
        stride = dilation // planes
        start = sub * Q_BLOCK * stride + st // planes
        tokens = pl.ds(start, Q_BLOCK, stride=stride) if stride > 1 else pl.ds(start, Q_BLOCK)
        plane = st % planes
        pvs = []
        for half in range(2):
            s = s_both[half * Q_BLOCK:(half + 1) * Q_BLOCK]
            m = jnp.max(s, axis=-1, keepdims=True)
            p = jnp.exp2(s - m).astype(BF16)
            pv = jnp.dot(p, vaug_ref[half, st, band, cols], preferred_element_type=F32)
            pvs.append(pv)
            stat_lane = _stat_lane(2 * pair + half)
            stats[0] = jnp.where(lane == stat_lane, m, stats[0])
            stats[-1] = jnp.where(lane == stat_lane + _sum_lane_offset(len(stats)), pv, stats[-1])
        o_ref[pair, plane, tokens, :] = jnp.where(low_half, pvs[0], pvs[1])
        if pair == HEAD_PAIRS - 1:
            for stat_ref, tile in zip(stat_refs, stats):
                stat_ref[plane, tokens, :] = tile
        return stats

    units = [(st, sub, pair) for st in range(dilation) for sub in range(rows // Q_BLOCK)
             for pair in range(HEAD_PAIRS)]
    ahead = [scores(*unit) for unit in units[:SCORE_LOOKAHEAD]]
    stats = None
    for index, unit in enumerate(units):
        s_both = ahead.pop(0)
        if index + SCORE_LOOKAHEAD < len(units):
            ahead.append(scores(*units[index + SCORE_LOOKAHEAD]))
        if unit[2] == 0:
            stats = [jnp.ones((Q_BLOCK, LANES), F32) for _ in stat_refs]
        stats = finish(*unit, s_both, stats)

    for st in range(dilation):
        kcarry_ref[st, 0:Q_BLOCK, :] = k_ref[st, rows - Q_BLOCK:rows, :]
        for half in range(2):
            vaug_ref[half, st, 0:Q_BLOCK, :] = vaug_ref[half, st, rows:rows + Q_BLOCK, :]


def _attention_group(qkv, bias, eye, batch, seq_len, group):
    _, dilation = ATT_GROUPS[group]
    stream_len = seq_len // dilation
    rows = max(Q_BLOCK, ATT_ROWS // dilation)
    planes = _store_planes(dilation)
    plane_rows = rows * dilation // planes
    single_step = rows == stream_len

    def qkv_spec(which):
        return pl.BlockSpec((None, dilation, rows, ATT_OUT_WIDTH), lambda b, m: (b, 0, m, which))

    stat_spec = pl.BlockSpec((None, planes, plane_rows, LANES), lambda b, m: (b, 0, m, 0))
    stat_shape = jax.ShapeDtypeStruct((batch, planes, seq_len // planes, LANES), F32)
    n_stat_tiles = _stat_tiles(dilation)
    return pl.pallas_call(
        functools.partial(_attention_kernel, single_step=single_step,
                          n_stat_tiles=n_stat_tiles),
        name=f"attention_g{group}",
        grid=(batch, stream_len // rows),
        in_specs=[
            qkv_spec(0), qkv_spec(1), qkv_spec(2),
            pl.BlockSpec(bias.shape, lambda b, m: (0, 0, 0), pipeline_mode=pl.Buffered(1)),
            pl.BlockSpec(eye.shape, lambda b, m: (0, 0), pipeline_mode=pl.Buffered(1)),
        ],
        out_specs=[stat_spec] * n_stat_tiles + [
            pl.BlockSpec((None, HEAD_PAIRS, planes, plane_rows, LANES),
                         lambda b, m: (b, 0, 0, m, 0)),
        ],
        out_shape=[stat_shape] * n_stat_tiles + [
            jax.ShapeDtypeStruct((batch, HEAD_PAIRS, planes, seq_len // planes, LANES), F32),
        ],
        scratch_shapes=[
            pltpu.VMEM((dilation, 2 * Q_BLOCK, ATT_OUT_WIDTH), BF16),
            pltpu.VMEM((2, dilation, Q_BLOCK + rows, ATT_OUT_WIDTH), BF16),
        ],
        compiler_params=pltpu.CompilerParams(
            dimension_semantics=("arbitrary", "arbitrary"),
            vmem_limit_bytes=V7X_VMEM_LIMIT_BYTES),
    )(qkv, qkv, qkv, bias, eye)


def _merge_ffn_kernel(x_ref, o0_ref, o1_ref, o2_ref, *rest):
    stat_refs = []
    for _, dilation in ATT_GROUPS:
        stat_refs.append(rest[:_stat_tiles(dilation)])
        rest = rest[_stat_tiles(dilation):]
    (pool_ref, gate_ref, spread_ref,
     wab_ref, wpb_ref, wout_ref, gffn_ref, wg_ref, wu_ref, wd_ref, gfin_ref,
     out_ref, att_ref, onat_ref, snat_ref) = rest
    d_model = x_ref.shape[1]
    o_refs = (o0_ref, o1_ref, o2_ref)
    lane = lax.broadcasted_iota(jnp.int32, (FFN_SUB_ROWS, LANES), 1)
    stat_lanes = (lane % HEAD_DIM) < HEAD_PAIRS

    slots = {g: slot for slot, g in enumerate(_multi_plane_groups())}

    def o_dst(g, pair):
        return onat_ref.at[slots[g], pair] if g in slots else None

    def stat_dst(g):
        return snat_ref.at[slots[g]] if g in slots else None

    def chain(sub):
        rows = slice(sub * FFN_SUB_ROWS, (sub + 1) * FFN_SUB_ROWS)

        def in_token_order(src_ref, dst_ref):
            planes = src_ref.shape[0]
            if planes == 1:
                return src_ref[0, rows, :]
            per_plane = FFN_SUB_ROWS // planes
            for plane in range(planes):
                dst_ref[pl.ds(sub * FFN_SUB_ROWS + plane, per_plane, stride=planes), :] = (
                    src_ref[plane, sub * per_plane:(sub + 1) * per_plane, :])
            return dst_ref[rows, :]

        ms, ls = [], []
        for g, tiles in enumerate(stat_refs):
            ms.append(in_token_order(tiles[0], stat_dst(g)))
            offset = _sum_lane_offset(len(tiles))
            if offset:
                ls.append(pltpu.roll(ms[-1], LANES - offset, axis=1))
            else:
                ls.append(in_token_order(tiles[-1], stat_dst(g)))
        top = jnp.maximum(jnp.maximum(ms[0], ms[1]), ms[2])
        es = [jnp.exp2(m - top) for m in ms]
        den = es[0] * ls[0] + es[1] * ls[1] + es[2] * ls[2]

        packed = jnp.zeros((FFN_SUB_ROWS, LANES), F32)
        for g in range(N_ATT_GROUPS):
            rest = jnp.where(stat_lanes, es[g] / den, 0.0)
            for piece in range(SPLIT_PIECES):
                part = rest.astype(BF16).astype(F32)
                rest = rest - part
                shift = _piece_shift(g, piece)
                packed = packed + (pltpu.roll(part, shift, axis=1) if shift else part)
        spread = jnp.dot(packed.astype(BF16), spread_ref[...], preferred_element_type=F32)
        yield

        for pair in range(HEAD_PAIRS):
            acc = jnp.zeros((FFN_SUB_ROWS, LANES), F32)
            for g in range(N_ATT_GROUPS):
                col = g * ATT_OUT_WIDTH + pair * LANES
                o_pair = in_token_order(o_refs[g].at[pair], o_dst(g, pair))
                acc = acc + spread[:, col:col + LANES] * o_pair
            att_ref[rows, pair * LANES:(pair + 1) * LANES] = acc.astype(BF16)
        yield

        y_att = jnp.dot(att_ref[rows, :], wab_ref[...], preferred_element_type=F32)
        y_pool = jnp.dot(pool_ref[rows, :], wpb_ref[...], preferred_element_type=F32)
        yield
        merged = (jax.nn.sigmoid(gate_ref[rows, 0:d_model]) * y_att
                  + jax.nn.sigmoid(gate_ref[rows, d_model:2 * d_model]) * y_pool)
        yield
        h = x_ref[rows, :] + jnp.dot(merged.astype(BF16), wout_ref[...],
                                     preferred_element_type=F32)
        yield
        f = _rms_norm(h, gffn_ref[...]).astype(BF16)
        yield
        ffn = jnp.zeros((FFN_SUB_ROWS, d_model), F32)
        for lo, hi in FFN_CHUNKS:
            gate = jnp.dot(f, wg_ref[:, lo:hi], preferred_element_type=F32)
            up = jnp.dot(f, wu_ref[:, lo:hi], preferred_element_type=F32)
            yield
            hidden = (jax.nn.silu(gate) * up).astype(BF16)
            yield
            ffn = ffn + jnp.dot(hidden, wd_ref[lo:hi, :], preferred_element_type=F32)
            yield
        out_ref[rows, :] = _rms_norm(h + ffn, gfin_ref[...])

    _run_staggered([chain(sub) for sub in range(x_ref.shape[0] // FFN_SUB_ROWS)])


def _merge_ffn(x2, outs, stats, pool_feat, gates, w_att_branch, w_pool_branch, w_out, norm_ffn,
               w_gate, w_up, w_down, norm_final, seq_len):
    tokens, d_model = x2.shape
    rows = FFN_ROWS
    tiles_per_seq = seq_len // rows

    def tile(width):
        return pl.BlockSpec((rows, width), lambda i: (i, 0))

    def whole(arr):
        return pl.BlockSpec(arr.shape, lambda i: (0, 0), pipeline_mode=pl.Buffered(1))

    def o_spec(planes):
        return pl.BlockSpec((None, HEAD_PAIRS, planes, rows // planes, LANES),
                            lambda i: (i // tiles_per_seq, 0, 0, i % tiles_per_seq, 0))

    def stat_spec(planes):
        return pl.BlockSpec((None, planes, rows // planes, LANES),
                            lambda i: (i // tiles_per_seq, 0, i % tiles_per_seq, 0))

    group_planes = [_store_planes(dilation) for _, dilation in ATT_GROUPS]
    n_slots = max(1, len(_multi_plane_groups()))
    weights = (_spread_matrix(), w_att_branch, w_pool_branch, w_out, norm_ffn, w_gate, w_up,
               w_down, norm_final)
    return pl.pallas_call(
        _merge_ffn_kernel,
        name="merge_ffn",
        grid=(tokens // rows,),
        in_specs=[tile(d_model)]
        + [o_spec(planes) for planes in group_planes]
        + [stat_spec(planes) for planes, tiles in zip(group_planes, stats) for _ in tiles]
        + [tile(POOL_WIDTH), tile(gates.shape[1])]
        + [whole(w) for w in weights],
        out_specs=tile(d_model),
        out_shape=jax.ShapeDtypeStruct((tokens, d_model), F32),
        scratch_shapes=[
            pltpu.VMEM((rows, ATT_OUT_WIDTH), BF16),
            pltpu.VMEM((n_slots, HEAD_PAIRS, rows, LANES), F32),
            pltpu.VMEM((n_slots, rows, LANES), F32),
        ],
        compiler_params=pltpu.CompilerParams(
            dimension_semantics=("arbitrary",), vmem_limit_bytes=V7X_VMEM_LIMIT_BYTES),
    )(x2, *outs, *[tile for tiles in stats for tile in tiles], pool_feat, gates, *weights)


def _rope_tables(seq_len):
    inv_freq = ROPE_THETA ** (-np.arange(0, ROT_DIM, 2, dtype=np.float64) / ROT_DIM)
    ang = np.arange(seq_len, dtype=np.float64)[:, None] * inv_freq[None, :]
    cos, sin = np.cos(ang), np.sin(ang)
    zeros = np.zeros((seq_len, HEAD_DIM - ROT_DIM))
    half0 = np.zeros((seq_len, ROT_HALF))
    c_head = np.concatenate([cos, cos, zeros + 1.0], axis=1)
    lo_head = np.concatenate([-sin, half0, zeros], axis=1)
    hi_head = np.concatenate([half0, sin, zeros], axis=1)
    tab = np.stack([c_head, lo_head, hi_head])
    return jnp.asarray(np.concatenate([tab] * (LANES // HEAD_DIM), axis=2), dtype=F32)


def _spread_matrix():
    row = jnp.arange(LANES)[:, None]
    col = jnp.arange(N_ATT_GROUPS * ATT_OUT_WIDTH)[None, :]
    group = col // ATT_OUT_WIDTH
    head = (col % ATT_OUT_WIDTH) // HEAD_DIM
    stat = head // 2 + jnp.where(head % 2 == 0, HEAD_DIM, 0)
    hit = jnp.zeros((LANES, N_ATT_GROUPS * ATT_OUT_WIDTH), jnp.bool_)
    for piece in range(SPLIT_PIECES):
        hit = hit | (row == stat + HEAD_PAIRS * (SPLIT_PIECES * group + piece))
    return hit.astype(BF16)


def _band_bias():
    kb = jnp.arange(2 * Q_BLOCK)[:, None]
    qi = jnp.arange(Q_BLOCK)[None, :]
    dist = qi + Q_BLOCK - kb
    valid = (dist >= 0) & (dist <= WINDOW_KEYS)
    both = jnp.stack([valid, valid & (kb >= Q_BLOCK)])
    return jnp.where(both, 0.0, MASKED).astype(BF16)


def _stacked_identity():
    row = jnp.arange(2 * Q_BLOCK)[:, None]
    col = jnp.arange(LANES)[None, :]
    return (row % Q_BLOCK == col).astype(BF16)


def kernel(x, norm_mix, w_in, w_pool_group, pool_scale, w_att_branch, w_pool_branch, w_out,
           norm_ffn, w_ffn_gate, w_ffn_up, w_ffn_down, norm_final):
    batch, seq_len, d_model = x.shape
    depth = w_in.shape[0]
    tokens = batch * seq_len
    assert depth == 1, "stacked layers need the un-normalised residual between layers"
    bias = _band_bias()

    h = x.reshape(tokens, d_model)
    later_weights = (w_att_branch[0], w_pool_branch[0], w_out[0], w_ffn_gate[0], w_ffn_up[0],
                     w_ffn_down[0])
    qkv0, qkv1, qkv2, pool_feat, gates, *later_bf16 = _in_proj(
        h, norm_mix[0][None, :], w_in[0].astype(BF16), w_pool_group[0].astype(BF16),
        pool_scale[0][None, :], _rope_tables(seq_len), later_weights, batch, seq_len)
    wab, wpb, wout, wgate, wup, wdown = later_bf16
    qkv_groups = (qkv0.reshape(batch, 1, seq_len, GROUP_QKV_WIDTH), qkv1, qkv2)
    eye = _stacked_identity()
    att = [_attention_group(qkv_groups[g], bias, eye, batch, seq_len, g)
           for g in range(N_ATT_GROUPS)]
    out = _merge_ffn(
        h, [a[-1] for a in att], [a[:-1] for a in att], pool_feat, gates,
        wab, wpb, wout, norm_ffn[0][None, :], wgate, wup, wdown, norm_final[None, :], seq_len)
    return out.reshape(batch, seq_len, d_model)
```

```python
import functools
import math

import jax
import jax.numpy as jnp
import numpy as np
from jax import lax
from jax.experimental import pallas as pl
from jax.experimental.pallas import tpu as pltpu

F32 = jnp.float32
BF16 = jnp.bfloat16

HEAD_DIM = 64
HEADS_PER_GROUP = 8
ATT_GROUPS = ((128, 1), (512, 4), (2048, 16))
N_ATT_GROUPS = len(ATT_GROUPS)
ATT_OUT_WIDTH = HEADS_PER_GROUP * HEAD_DIM
ATT_WIDTH = N_ATT_GROUPS * ATT_OUT_WIDTH
QKV_WIDTH = 3 * ATT_WIDTH
GROUP_QKV_WIDTH = 3 * ATT_OUT_WIDTH
ROT_DIM = HEAD_DIM // 4
ROT_HALF = ROT_DIM // 2
ROPE_THETA = 500000.0
POOL_WINDOWS = (2, 4, 8, 16)
POOL_GROUP_WIDTH = 128
POOL_WIDTH = len(POOL_WINDOWS) * POOL_GROUP_WIDTH
NORM_EPS = 1e-6
WINDOW_KEYS = 128
MASKED = -3.0e38
Q_SCALE = HEAD_DIM ** -0.5 * math.log2(math.e)

LANES = 128
BF16_SUBLANES = 16
MAX_STORE_STRIDE = 4
V7X_VMEM_BYTES = 64 * 1024 * 1024
V7X_VMEM_LIMIT_BYTES = V7X_VMEM_BYTES - 4 * 1024 * 1024

PROJ_ROWS = 512
PROJ_SUB_ROWS = 256
PROJ_COLS = ATT_OUT_WIDTH
assert PROJ_COLS == POOL_WIDTH
POOL_HALO = 16
ATT_ROWS = 2048
Q_BLOCK = 128
HEAD_PAIRS = HEADS_PER_GROUP // 2
SUM_LANE_OFFSET = 2 * HEAD_PAIRS
SCORE_LOOKAHEAD = 1
FFN_ROWS = 512
FFN_SUB_ROWS = 256
FFN_CHUNKS = ((0, 1024), (1024, 2048), (2048, 2816))
SPLIT_PIECES = 3


def _run_staggered(chains):
    live = list(enumerate(chains))
    tick = 0
    while live:
        for item in list(live):
            index, gen = item
            if tick >= index and next(gen, StopIteration) is StopIteration:
                live.remove(item)
        tick += 1


def _rms_norm(xf, gain):
    ms = jnp.mean(xf * xf, axis=-1, keepdims=True)
    return (xf * lax.rsqrt(ms + NORM_EPS)) * gain


def _in_proj_kernel(x_ref, gain_ref, w_ref, wpg_ref, pscale_ref, rope_ref, *rest,
                    tiles_per_seq, n_later):
    later_f32 = rest[:n_later]
    qkv0_ref, qkv1_ref, qkv2_ref, pool_ref = rest[n_later:n_later + 4]
    later_bf16 = rest[n_later + 4:2 * n_later + 4]
    (u_ref, u4_ref, u16_ref, uslab_ref, u4slab_ref, tab4_ref, tab16_ref,
     z0_ref, z1_ref, z2_ref, z3_ref) = rest[2 * n_later + 4:]
    rows, d_model = x_ref.shape
    seq_tile = pl.program_id(0) % tiles_per_seq
    n_slabs = d_model // LANES
    sub_rows = PROJ_SUB_ROWS
    n_sub = rows // sub_rows
    run4 = sub_rows // 4
    run16 = sub_rows // 16
    levels = (z0_ref, z1_ref, z2_ref, z3_ref)

    @pl.when(seq_tile == 0)
    def _():
        for ref in levels:
            ref[0:POOL_HALO, :] = jnp.zeros((POOL_HALO, POOL_WIDTH), F32)

    def project(lhs_ref, sub, col0):
        return jnp.dot(lhs_ref[sub], w_ref[:, col0:col0 + PROJ_COLS],
                       preferred_element_type=F32)

    for sub in range(n_sub):
        row0 = sub * sub_rows
        natural = slice(row0, row0 + sub_rows)
        uf = _rms_norm(x_ref[natural, :], gain_ref[...])
        u_ref[sub] = uf.astype(BF16)

        z0_ref[POOL_HALO + row0:POOL_HALO + row0 + sub_rows, :] = project(u_ref, sub, QKV_WIDTH)

        for s in range(n_slabs):
            uslab_ref[sub, s] = uf[:, s * LANES:(s + 1) * LANES]
        for r in range(4):
            dst = slice(r * run4, (r + 1) * run4)
            for s in range(n_slabs):
                piece = uslab_ref[sub, s, pl.ds(r, run4, stride=4), :]
                u4slab_ref[sub, s, dst, :] = piece
                u4_ref[sub, dst, s * LANES:(s + 1) * LANES] = piece.astype(BF16)
            for t in range(3):
                tab4_ref[sub, t, dst, :] = rope_ref[t, pl.ds(row0 + r, run4, stride=4), :]
        for run in range(16):
            src0 = (run // 4) * run4 + run % 4
            dst = slice(run * run16, (run + 1) * run16)
            for s in range(n_slabs):
                piece = u4slab_ref[sub, s, pl.ds(src0, run16, stride=4), :]
                u16_ref[sub, dst, s * LANES:(s + 1) * LANES] = piece.astype(BF16)
            for t in range(3):
                tab16_ref[sub, t, dst, :] = tab4_ref[sub, t, pl.ds(src0, run16, stride=4), :]

    cur = slice(POOL_HALO, POOL_HALO + rows)
    sums = []
    prev = z0_ref[cur, :]
    for lvl, ref in enumerate(levels):
        shift = 1 << lvl
        lo = lvl * POOL_GROUP_WIDTH
        shifted = ref[POOL_HALO - shift:POOL_HALO - shift + rows, lo:]
        prev = prev[:, (POOL_GROUP_WIDTH if lvl else 0):] + shifted
        sums.append(prev[:, 0:POOL_GROUP_WIDTH])
        if lvl + 1 < len(levels):
            levels[lvl + 1][cur, lo:] = prev
    pos = seq_tile * rows + lax.broadcasted_iota(jnp.int32, (rows, 1), 0)
    pooled = []
    for g, w in enumerate(POOL_WINDOWS):
        cols = slice(g * POOL_GROUP_WIDTH, (g + 1) * POOL_GROUP_WIDTH)
        count = jnp.minimum(pos + 1, w).astype(F32)
        pooled.append((sums[g] / count - z0_ref[cur, cols]).astype(BF16))
    for lvl, ref in enumerate(levels):
        lo = lvl * POOL_GROUP_WIDTH
        ref[0:POOL_HALO, lo:] = ref[rows:rows + POOL_HALO, lo:]

    def rope(xs, table):
        up = pltpu.roll(xs, LANES - ROT_HALF, axis=1)
        down = pltpu.roll(xs, ROT_HALF, axis=1)
        return xs * table(0) + up * table(1) + down * table(2)

    def store_natural(sub, col, part):
        qkv0_ref[sub * sub_rows:(sub + 1) * sub_rows, col:col + LANES] = part

    def store_by4(sub, col, part):
        for r in range(4):
            qkv1_ref[r, sub * run4:(sub + 1) * run4, col:col + LANES] = (
                part[r * run4:(r + 1) * run4])

    def store_by16(sub, col, part):
        for run in range(16):
            stream = 4 * (run % 4) + run // 4
            qkv2_ref[stream, sub * run16:(sub + 1) * run16, col:col + LANES] = (
                part[run * run16:(run + 1) * run16])

    for sub in range(n_sub):
        natural = slice(sub * sub_rows, (sub + 1) * sub_rows)
        variants = (
            (u_ref, lambda t: rope_ref[t, natural, :], store_natural),
            (u4_ref, lambda t: tab4_ref[sub, t], store_by4),
            (u16_ref, lambda t: tab16_ref[sub, t], store_by16),
        )
        for group, (lhs_ref, table, store) in enumerate(variants):
            for which in range(3):
                acc = project(lhs_ref, sub, which * ATT_WIDTH + group * ATT_OUT_WIDTH)
                for j in range(PROJ_COLS // LANES):
                    part = acc[:, j * LANES:(j + 1) * LANES]
                    if which == 0:
                        part = rope(part, table) * Q_SCALE
                    elif which == 1:
                        part = rope(part, table)
                    store(sub, which * ATT_OUT_WIDTH + j * LANES, part.astype(BF16))

    for g in range(len(POOL_WINDOWS)):
        cols = slice(g * POOL_GROUP_WIDTH, (g + 1) * POOL_GROUP_WIDTH)
        mapped = jnp.dot(pooled[g], wpg_ref[g], preferred_element_type=F32)
        pool_ref[:, cols] = (mapped * pscale_ref[:, cols]).astype(BF16)

    for src_ref, dst_ref in zip(later_f32, later_bf16):
        dst_ref[...] = src_ref[...].astype(BF16)


def _chunk_rows(total_rows, steps):
    chunk = BF16_SUBLANES * pl.cdiv(pl.cdiv(total_rows, steps), BF16_SUBLANES)
    while total_rows % chunk:
        chunk += BF16_SUBLANES
    return chunk


def _in_proj(x2, gain, w_in, w_pool_group, pool_scale, rope_tab, later_weights, batch, seq_len):
    tokens, d_model = x2.shape
    in_width = w_in.shape[1]
    rows = PROJ_ROWS
    n_sub = rows // PROJ_SUB_ROWS
    n_steps = tokens // rows
    tiles_per_seq = seq_len // rows
    const = dict(pipeline_mode=pl.Buffered(1))

    def stream_spec(dilation):
        return pl.BlockSpec((None, dilation, rows // dilation, GROUP_QKV_WIDTH),
                            lambda i: (i // tiles_per_seq, 0, i % tiles_per_seq, 0))

    def chunk_spec(weight, width, col_block):
        chunk = _chunk_rows(weight.shape[0], n_steps)
        last = weight.shape[0] // chunk - 1
        return pl.BlockSpec((chunk, width), lambda i: (jnp.minimum(i, last), col_block))

    later_in_specs = [chunk_spec(w, width, col) for w, width, col in later_weights]
    later_out_specs = [chunk_spec(w, width, 0) for w, width, _ in later_weights]
    return pl.pallas_call(
        functools.partial(_in_proj_kernel, tiles_per_seq=tiles_per_seq,
                          n_later=len(later_weights)),
        name="in_proj",
        grid=(n_steps,),
        in_specs=[
            pl.BlockSpec((rows, d_model), lambda i: (i, 0)),
            pl.BlockSpec((1, d_model), lambda i: (0, 0), **const),
            pl.BlockSpec((d_model, in_width), lambda i: (0, 0), **const),
            pl.BlockSpec(w_pool_group.shape, lambda i: (0, 0, 0), **const),
            pl.BlockSpec((1, POOL_WIDTH), lambda i: (0, 0), **const),
            pl.BlockSpec((3, rows, LANES), lambda i: (0, i % tiles_per_seq, 0)),
        ] + later_in_specs,
        out_specs=[
            pl.BlockSpec((rows, GROUP_QKV_WIDTH), lambda i: (i, 0)),
            stream_spec(4),
            stream_spec(16),
            pl.BlockSpec((rows, POOL_WIDTH), lambda i: (i, 0)),
        ] + later_out_specs,
        out_shape=[
            jax.ShapeDtypeStruct((tokens, GROUP_QKV_WIDTH), BF16),
            jax.ShapeDtypeStruct((batch, 4, seq_len // 4, GROUP_QKV_WIDTH), BF16),
            jax.ShapeDtypeStruct((batch, 16, seq_len // 16, GROUP_QKV_WIDTH), BF16),
            jax.ShapeDtypeStruct((tokens, POOL_WIDTH), BF16),
        ] + [jax.ShapeDtypeStruct((w.shape[0], width), BF16) for w, width, _ in later_weights],
        scratch_shapes=[pltpu.VMEM((n_sub, PROJ_SUB_ROWS, d_model), BF16)] * 3
        + [pltpu.VMEM((n_sub, d_model // LANES, PROJ_SUB_ROWS, LANES), F32)] * 2
        + [pltpu.VMEM((n_sub, 3, PROJ_SUB_ROWS, LANES), F32)] * 2
        + [pltpu.VMEM((POOL_HALO + rows, POOL_WIDTH), F32) for _ in POOL_WINDOWS],
        compiler_params=pltpu.CompilerParams(
            dimension_semantics=("arbitrary",), vmem_limit_bytes=V7X_VMEM_LIMIT_BYTES),
    )(x2, gain, w_in, w_pool_group, pool_scale, rope_tab, *[w for w, _, _ in later_weights])


def _stat_lane(head):
    return head // 2 + (HEAD_DIM if head % 2 == 0 else 0)


def _store_planes(dilation):
    return max(1, dilation // MAX_STORE_STRIDE)


def _stat_tiles(dilation):
    return 2 if dilation == 1 else 1


def _sum_lane_offset(n_stat_tiles):
    return SUM_LANE_OFFSET if n_stat_tiles == 1 else 0


def _multi_plane_groups():
    return [g for g, (_, dilation) in enumerate(ATT_GROUPS) if _store_planes(dilation) > 1]


def _piece_shift(group, piece):
    return HEAD_PAIRS * (SPLIT_PIECES * group + piece)


def _attention_kernel(q_ref, k_ref, v_ref, bias_ref, eye_ref, *rest, single_step, n_stat_tiles):
    stat_refs = rest[:n_stat_tiles]
    o_ref, kcarry_ref, vaug_ref = rest[n_stat_tiles:]
    dilation, rows, _ = q_ref.shape
    first = pl.program_id(1) == 0

    lane = lax.broadcasted_iota(jnp.int32, (Q_BLOCK, LANES), 1)
    low_half = lane < HEAD_DIM
    head_mask = (jnp.where(low_half, 1.0, 0.0).astype(BF16),
                 jnp.where(low_half, 0.0, 1.0).astype(BF16))
    low_rows = lax.broadcasted_iota(jnp.int32, (rows, LANES), 1) < HEAD_DIM
    one = jnp.ones((rows, LANES), BF16)
    nt_dims = (((1,), (1,)), ((), ()))
    zero_block = jnp.zeros((Q_BLOCK, ATT_OUT_WIDTH), BF16)

    def clear_carry():
        for st in range(dilation):
            kcarry_ref[st, 0:Q_BLOCK, :] = zero_block
            vaug_ref[0, st, 0:Q_BLOCK, :] = zero_block
            vaug_ref[1, st, 0:Q_BLOCK, :] = zero_block

    if single_step:
        clear_carry()
    else:
        pl.when(first)(clear_carry)

    for st in range(dilation):
        for pair in range(HEAD_PAIRS):
            cols = slice(pair * LANES, (pair + 1) * LANES)
            v_pair = v_ref[st, :, cols]
            vaug_ref[0, st, Q_BLOCK:Q_BLOCK + rows, cols] = jnp.where(low_rows, v_pair, one)
            vaug_ref[1, st, Q_BLOCK:Q_BLOCK + rows, cols] = jnp.where(low_rows, one, v_pair)

    def scores(st, sub, pair):
        cols = slice(pair * LANES, (pair + 1) * LANES)
        q_pair = q_ref[st, sub * Q_BLOCK:(sub + 1) * Q_BLOCK, cols]
        if sub == 0:
            k_pair = jnp.concatenate(
                [kcarry_ref[st, :, cols], k_ref[st, 0:Q_BLOCK, cols]], axis=0)
            bias_idx = 1 if single_step else jnp.where(first, 1, 0)
        else:
            k_pair = k_ref[st, (sub - 1) * Q_BLOCK:(sub + 1) * Q_BLOCK, cols]
            bias_idx = 0
        q_both = jnp.concatenate([q_pair * head_mask[0], q_pair * head_mask[1]], axis=0)
        q_both = jnp.concatenate([q_both, eye_ref[...]], axis=1)
        k_aug = jnp.concatenate([k_pair, bias_ref[bias_idx]], axis=1)
        return lax.dot_general(q_both, k_aug, nt_dims, preferred_element_type=F32)

    def finish(st, sub, pair, s_both, stats):
        cols = slice(pair * LANES, (pair + 1) * LANES)
        band = slice(sub * Q_BLOCK, (sub + 2) * Q_BLOCK)
        planes = _store_planes(dilation)
        stride = dilation // planes
        start = sub * Q_BLOCK * stride + st // planes
        tokens = pl.ds(start, Q_BLOCK, stride=stride) if stride > 1 else pl.ds(start, Q_BLOCK)
        plane = st % planes
        pvs = []
        for half in range(2):
            s = s_both[half * Q_BLOCK:(half + 1) * Q_BLOCK]
            m = jnp.max(s, axis=-1, keepdims=True)
            p = jnp.exp2(s - m).astype(BF16)
            pv = jnp.dot(p, vaug_ref[half, st, band, cols], preferred_element_type=F32)
            pvs.append(pv)
            stat_lane = _stat_lane(2 * pair + half)
            stats[0] = jnp.where(lane == stat_lane, m, stats[0])
            stats[-1] = jnp.where(lane == stat_lane + _sum_lane_offset(len(stats)), pv, stats[-1])
        o_ref[pair, plane, tokens, :] = jnp.where(low_half, pvs[0], pvs[1])
        if pair == HEAD_PAIRS - 1:
            for stat_ref, tile in zip(stat_refs, stats):
                stat_ref[plane, tokens, :] = tile
        return stats

    units = [(st, sub, pair) for st in range(dilation) for sub in range(rows // Q_BLOCK)
             for pair in range(HEAD_PAIRS)]
    ahead = [scores(*unit) for unit in units[:SCORE_LOOKAHEAD]]
    stats = None
    for index, unit in enumerate(units):
        s_both = ahead.pop(0)
        if index + SCORE_LOOKAHEAD < len(units):
            ahead.append(scores(*units[index + SCORE_LOOKAHEAD]))
        if unit[2] == 0:
            stats = [jnp.ones((Q_BLOCK, LANES), F32) for _ in stat_refs]
        stats = finish(*unit, s_both, stats)

    for st in range(dilation):
        kcarry_ref[st, 0:Q_BLOCK, :] = k_ref[st, rows - Q_BLOCK:rows, :]
        for half in range(2):
            vaug_ref[half, st, 0:Q_BLOCK, :] = vaug_ref[half, st, rows:rows + Q_BLOCK, :]


def _attention_group(qkv, bias, eye, batch, seq_len, group):
    _, dilation = ATT_GROUPS[group]
    stream_len = seq_len // dilation
    rows = max(Q_BLOCK, ATT_ROWS // dilation)
    planes = _store_planes(dilation)
    plane_rows = rows * dilation // planes
    single_step = rows == stream_len

    def qkv_spec(which):
        return pl.BlockSpec((None, dilation, rows, ATT_OUT_WIDTH), lambda b, m: (b, 0, m, which))

    stat_spec = pl.BlockSpec((None, planes, plane_rows, LANES), lambda b, m: (b, 0, m, 0))
    stat_shape = jax.ShapeDtypeStruct((batch, planes, seq_len // planes, LANES), F32)
    n_stat_tiles = _stat_tiles(dilation)
    return pl.pallas_call(
        functools.partial(_attention_kernel, single_step=single_step,
                          n_stat_tiles=n_stat_tiles),
        name=f"attention_g{group}",
        grid=(batch, stream_len // rows),
        in_specs=[
            qkv_spec(0), qkv_spec(1), qkv_spec(2),
            pl.BlockSpec(bias.shape, lambda b, m: (0, 0, 0), pipeline_mode=pl.Buffered(1)),
            pl.BlockSpec(eye.shape, lambda b, m: (0, 0), pipeline_mode=pl.Buffered(1)),
        ],
        out_specs=[stat_spec] * n_stat_tiles + [
            pl.BlockSpec((None, HEAD_PAIRS, planes, plane_rows, LANES),
                         lambda b, m: (b, 0, 0, m, 0)),
        ],
        out_shape=[stat_shape] * n_stat_tiles + [
            jax.ShapeDtypeStruct((batch, HEAD_PAIRS, planes, seq_len // planes, LANES), F32),
        ],
        scratch_shapes=[
            pltpu.VMEM((dilation, Q_BLOCK, ATT_OUT_WIDTH), BF16),
            pltpu.VMEM((2, dilation, Q_BLOCK + rows, ATT_OUT_WIDTH), BF16),
        ],
        compiler_params=pltpu.CompilerParams(
            dimension_semantics=("arbitrary", "arbitrary"),
            vmem_limit_bytes=V7X_VMEM_LIMIT_BYTES),
    )(qkv, qkv, qkv, bias, eye)


def _merge_ffn_kernel(x_ref, o0_ref, o1_ref, o2_ref, *rest):
    stat_refs = []
    for _, dilation in ATT_GROUPS:
        stat_refs.append(rest[:_stat_tiles(dilation)])
        rest = rest[_stat_tiles(dilation):]
    (pool_ref, gmix_ref, wga_ref, wgp_ref, spread_ref,
     wab_ref, wpb_ref, wout_ref, gffn_ref, wg_ref, wu_ref, wd_ref, gfin_ref,
     out_ref, att_ref, onat_ref, snat_ref) = rest
    d_model = x_ref.shape[1]
    o_refs = (o0_ref, o1_ref, o2_ref)
    lane = lax.broadcasted_iota(jnp.int32, (FFN_SUB_ROWS, LANES), 1)
    stat_lanes = (lane % HEAD_DIM) < HEAD_PAIRS

    slots = {g: slot for slot, g in enumerate(_multi_plane_groups())}

    def o_dst(g, pair):
        return onat_ref.at[slots[g], pair] if g in slots else None

    def stat_dst(g):
        return snat_ref.at[slots[g]] if g in slots else None

    def chain(sub):
        rows = slice(sub * FFN_SUB_ROWS, (sub + 1) * FFN_SUB_ROWS)

        u = _rms_norm(x_ref[rows, :], gmix_ref[...]).astype(BF16)
        gate_att = jnp.dot(u, wga_ref[...], preferred_element_type=F32)
        gate_pool = jnp.dot(u, wgp_ref[...], preferred_element_type=F32)

        def in_token_order(src_ref, dst_ref):
            planes = src_ref.shape[0]
            if planes == 1:
                return src_ref[0, rows, :]
            per_plane = FFN_SUB_ROWS // planes
            for plane in range(planes):
                dst_ref[pl.ds(sub * FFN_SUB_ROWS + plane, per_plane, stride=planes), :] = (
                    src_ref[plane, sub * per_plane:(sub + 1) * per_plane, :])
            return dst_ref[rows, :]

        ms, ls = [], []
        for g, tiles in enumerate(stat_refs):
            ms.append(in_token_order(tiles[0], stat_dst(g)))
            offset = _sum_lane_offset(len(tiles))
            if offset:
                ls.append(pltpu.roll(ms[-1], LANES - offset, axis=1))
            else:
                ls.append(in_token_order(tiles[-1], stat_dst(g)))
        top = jnp.maximum(jnp.maximum(ms[0], ms[1]), ms[2])
        es = [jnp.exp2(m - top) for m in ms]
        den = es[0] * ls[0] + es[1] * ls[1] + es[2] * ls[2]

        packed = jnp.zeros((FFN_SUB_ROWS, LANES), F32)
        for g in range(N_ATT_GROUPS):
            rest = jnp.where(stat_lanes, es[g] / den, 0.0)
            for piece in range(SPLIT_PIECES):
                part = rest.astype(BF16).astype(F32)
                rest = rest - part
                shift = _piece_shift(g, piece)
                packed = packed + (pltpu.roll(part, shift, axis=1) if shift else part)
        spread = jnp.dot(packed.astype(BF16), spread_ref[...], preferred_element_type=F32)
        yield

        for pair in range(HEAD_PAIRS):
            acc = jnp.zeros((FFN_SUB_ROWS, LANES), F32)
            for g in range(N_ATT_GROUPS):
                col = g * ATT_OUT_WIDTH + pair * LANES
                o_pair = in_token_order(o_refs[g].at[pair], o_dst(g, pair))
                acc = acc + spread[:, col:col + LANES] * o_pair
            att_ref[rows, pair * LANES:(pair + 1) * LANES] = acc.astype(BF16)
        yield

        y_att = jnp.dot(att_ref[rows, :], wab_ref[...], preferred_element_type=F32)
        y_pool = jnp.dot(pool_ref[rows, :], wpb_ref[...], preferred_element_type=F32)
        yield
        merged = jax.nn.sigmoid(gate_att) * y_att + jax.nn.sigmoid(gate_pool) * y_pool
        yield
        h = x_ref[rows, :] + jnp.dot(merged.astype(BF16), wout_ref[...],
                                     preferred_element_type=F32)
        yield
        f = _rms_norm(h, gffn_ref[...]).astype(BF16)
        yield
        ffn = jnp.zeros((FFN_SUB_ROWS, d_model), F32)
        for lo, hi in FFN_CHUNKS:
            gate = jnp.dot(f, wg_ref[:, lo:hi], preferred_element_type=F32)
            up = jnp.dot(f, wu_ref[:, lo:hi], preferred_element_type=F32)
            yield
            hidden = (jax.nn.silu(gate) * up).astype(BF16)
            yield
            ffn = ffn + jnp.dot(hidden, wd_ref[lo:hi, :], preferred_element_type=F32)
            yield
        out_ref[rows, :] = _rms_norm(h + ffn, gfin_ref[...])

    _run_staggered([chain(sub) for sub in range(x_ref.shape[0] // FFN_SUB_ROWS)])


def _merge_ffn(x2, outs, stats, pool_feat, norm_mix, w_gate_att, w_gate_pool, w_att_branch,
               w_pool_branch, w_out, norm_ffn, w_gate, w_up, w_down, norm_final, seq_len):
    tokens, d_model = x2.shape
    rows = FFN_ROWS
    tiles_per_seq = seq_len // rows

    def tile(width):
        return pl.BlockSpec((rows, width), lambda i: (i, 0))

    def whole(arr):
        return pl.BlockSpec(arr.shape, lambda i: (0, 0), pipeline_mode=pl.Buffered(1))

    def o_spec(planes):
        return pl.BlockSpec((None, HEAD_PAIRS, planes, rows // planes, LANES),
                            lambda i: (i // tiles_per_seq, 0, 0, i % tiles_per_seq, 0))

    def stat_spec(planes):
        return pl.BlockSpec((None, planes, rows // planes, LANES),
                            lambda i: (i // tiles_per_seq, 0, i % tiles_per_seq, 0))

    group_planes = [_store_planes(dilation) for _, dilation in ATT_GROUPS]
    n_slots = max(1, len(_multi_plane_groups()))
    weights = (norm_mix, w_gate_att, w_gate_pool, _spread_matrix(), w_att_branch, w_pool_branch, w_out,
               norm_ffn, w_gate, w_up, w_down, norm_final)
    return pl.pallas_call(
        _merge_ffn_kernel,
        name="merge_ffn",
        grid=(tokens // rows,),
        in_specs=[tile(d_model)]
        + [o_spec(planes) for planes in group_planes]
        + [stat_spec(planes) for planes, tiles in zip(group_planes, stats) for _ in tiles]
        + [tile(POOL_WIDTH)]
        + [whole(w) for w in weights],
        out_specs=tile(d_model),
        out_shape=jax.ShapeDtypeStruct((tokens, d_model), F32),
        scratch_shapes=[
            pltpu.VMEM((rows, ATT_OUT_WIDTH), BF16),
            pltpu.VMEM((n_slots, HEAD_PAIRS, rows, LANES), F32),
            pltpu.VMEM((n_slots, rows, LANES), F32),
        ],
        compiler_params=pltpu.CompilerParams(
            dimension_semantics=("arbitrary",), vmem_limit_bytes=V7X_VMEM_LIMIT_BYTES),
    )(x2, *outs, *[tile for tiles in stats for tile in tiles], pool_feat, *weights)


def _rope_tables(seq_len):
    inv_freq = ROPE_THETA ** (-np.arange(0, ROT_DIM, 2, dtype=np.float64) / ROT_DIM)
    ang = np.arange(seq_len, dtype=np.float64)[:, None] * inv_freq[None, :]
    cos, sin = np.cos(ang), np.sin(ang)
    zeros = np.zeros((seq_len, HEAD_DIM - ROT_DIM))
    half0 = np.zeros((seq_len, ROT_HALF))
    c_head = np.concatenate([cos, cos, zeros + 1.0], axis=1)
    lo_head = np.concatenate([-sin, half0, zeros], axis=1)
    hi_head = np.concatenate([half0, sin, zeros], axis=1)
    tab = np.stack([c_head, lo_head, hi_head])
    return jnp.asarray(np.concatenate([tab] * (LANES // HEAD_DIM), axis=2), dtype=F32)


def _spread_matrix():
    row = jnp.arange(LANES)[:, None]
    col = jnp.arange(N_ATT_GROUPS * ATT_OUT_WIDTH)[None, :]
    group = col // ATT_OUT_WIDTH
    head = (col % ATT_OUT_WIDTH) // HEAD_DIM
    stat = head // 2 + jnp.where(head % 2 == 0, HEAD_DIM, 0)
    hit = jnp.zeros((LANES, N_ATT_GROUPS * ATT_OUT_WIDTH), jnp.bool_)
    for piece in range(SPLIT_PIECES):
        hit = hit | (row == stat + HEAD_PAIRS * (SPLIT_PIECES * group + piece))
    return hit.astype(BF16)


def _band_bias():
    kb = jnp.arange(2 * Q_BLOCK)[:, None]
    qi = jnp.arange(Q_BLOCK)[None, :]
    dist = qi + Q_BLOCK - kb
    valid = (dist >= 0) & (dist <= WINDOW_KEYS)
    both = jnp.stack([valid, valid & (kb >= Q_BLOCK)])
    return jnp.where(both, 0.0, MASKED).astype(BF16)


def _stacked_identity():
    row = jnp.arange(2 * Q_BLOCK)[:, None]
    col = jnp.arange(LANES)[None, :]
    return (row % Q_BLOCK == col).astype(BF16)


def kernel(x, norm_mix, w_in, w_pool_group, pool_scale, w_att_branch, w_pool_branch, w_out,
           norm_ffn, w_ffn_gate, w_ffn_up, w_ffn_down, norm_final):
    batch, seq_len, d_model = x.shape
    depth = w_in.shape[0]
    tokens = batch * seq_len
    assert depth == 1, "stacked layers need the un-normalised residual between layers"
    bias = _band_bias()

    h = x.reshape(tokens, d_model)
    mix_width = QKV_WIDTH + POOL_WIDTH
    assert mix_width % d_model == 0 and w_in.shape[2] == mix_width + 2 * d_model
    gate_block = mix_width // d_model
    later_weights = [(w_in[0], d_model, gate_block), (w_in[0], d_model, gate_block + 1)] + [
        (w[0], w.shape[2], 0)
        for w in (w_att_branch, w_pool_branch, w_out, w_ffn_gate, w_ffn_up, w_ffn_down)]
    qkv0, qkv1, qkv2, pool_feat, *later_bf16 = _in_proj(
        h, norm_mix[0][None, :], w_in[0][:, :mix_width].astype(BF16),
        w_pool_group[0].astype(BF16), pool_scale[0][None, :], _rope_tables(seq_len),
        later_weights, batch, seq_len)
    wga, wgp, wab, wpb, wout, wgate, wup, wdown = later_bf16
    qkv_groups = (qkv0.reshape(batch, 1, seq_len, GROUP_QKV_WIDTH), qkv1, qkv2)
    eye = _stacked_identity()
    att = [_attention_group(qkv_groups[g], bias, eye, batch, seq_len, g)
           for g in range(N_ATT_GROUPS)]
    out = _merge_ffn(
        h, [a[-1] for a in att], [a[:-1] for a in att], pool_feat, norm_mix[0][None, :], wga, wgp,
        wab, wpb, wout, norm_ffn[0][None, :], wgate, wup, wdown, norm_final[None, :], seq_len)
    return out.reshape(batch, seq_len, d_model)
```

```python
import functools
import math

import jax
import jax.numpy as jnp
import numpy as np
from jax import lax
from jax.experimental import pallas as pl
from jax.experimental.pallas import tpu as pltpu

F32 = jnp.float32
BF16 = jnp.bfloat16

HEAD_DIM = 64
HEADS_PER_GROUP = 8
ATT_GROUPS = ((128, 1), (512, 4), (2048, 16))
N_ATT_GROUPS = len(ATT_GROUPS)
ATT_OUT_WIDTH = HEADS_PER_GROUP * HEAD_DIM
ATT_WIDTH = N_ATT_GROUPS * ATT_OUT_WIDTH
QKV_WIDTH = 3 * ATT_WIDTH
GROUP_QKV_WIDTH = 3 * ATT_OUT_WIDTH
ROT_DIM = HEAD_DIM // 4
ROT_HALF = ROT_DIM // 2
ROPE_THETA = 500000.0
POOL_WINDOWS = (2, 4, 8, 16)
POOL_GROUP_WIDTH = 128
POOL_WIDTH = len(POOL_WINDOWS) * POOL_GROUP_WIDTH
NORM_EPS = 1e-6
WINDOW_KEYS = 128
MASKED = -3.0e38
Q_SCALE = HEAD_DIM ** -0.5 * math.log2(math.e)

LANES = 128
BF16_SUBLANES = 16
MAX_STORE_STRIDE = 4
V7X_VMEM_BYTES = 64 * 1024 * 1024
V7X_VMEM_LIMIT_BYTES = V7X_VMEM_BYTES - 4 * 1024 * 1024

PROJ_ROWS = 512
PROJ_SUB_ROWS = 256
PROJ_COLS = ATT_OUT_WIDTH
assert PROJ_COLS == POOL_WIDTH
POOL_HALO = 16
ATT_ROWS = 2048
Q_BLOCK = 128
HEAD_PAIRS = HEADS_PER_GROUP // 2
SUM_LANE_OFFSET = 2 * HEAD_PAIRS
SCORE_LOOKAHEAD = 1
FFN_ROWS = 512
FFN_SUB_ROWS = 256
FFN_CHUNKS = ((0, 1024), (1024, 2048), (2048, 2816))
SPLIT_PIECES = 3


def _run_staggered(chains):
    live = list(enumerate(chains))
    tick = 0
    while live:
        for item in list(live):
            index, gen = item
            if tick >= index and next(gen, StopIteration) is StopIteration:
                live.remove(item)
        tick += 1


def _rms_norm(xf, gain):
    ms = jnp.mean(xf * xf, axis=-1, keepdims=True)
    return (xf * lax.rsqrt(ms + NORM_EPS)) * gain


def _in_proj_kernel(x_ref, gain_ref, w_ref, wpg_ref, pscale_ref, rope_ref, *rest,
                    tiles_per_seq, n_later):
    later_f32 = rest[:n_later]
    qkv0_ref, qkv1_ref, qkv2_ref, pool_ref, gate_ref = rest[n_later:n_later + 5]
    later_bf16 = rest[n_later + 5:2 * n_later + 5]
    (u_ref, u4_ref, u16_ref, uslab_ref, u4slab_ref, tab4_ref, tab16_ref,
     z0_ref, z1_ref, z2_ref, z3_ref) = rest[2 * n_later + 5:]
    rows, d_model = x_ref.shape
    seq_tile = pl.program_id(0) % tiles_per_seq
    n_slabs = d_model // LANES
    sub_rows = PROJ_SUB_ROWS
    n_sub = rows // sub_rows
    run4 = sub_rows // 4
    run16 = sub_rows // 16
    levels = (z0_ref, z1_ref, z2_ref, z3_ref)

    @pl.when(seq_tile == 0)
    def _():
        for ref in levels:
            ref[0:POOL_HALO, :] = jnp.zeros((POOL_HALO, POOL_WIDTH), F32)

    def project(lhs_ref, sub, col0):
        return jnp.dot(lhs_ref[sub], w_ref[:, col0:col0 + PROJ_COLS],
                       preferred_element_type=F32)

    gate0 = QKV_WIDTH + POOL_WIDTH
    for sub in range(n_sub):
        row0 = sub * sub_rows
        natural = slice(row0, row0 + sub_rows)
        uf = _rms_norm(x_ref[natural, :], gain_ref[...])
        u_ref[sub] = uf.astype(BF16)

        z0_ref[POOL_HALO + row0:POOL_HALO + row0 + sub_rows, :] = project(u_ref, sub, QKV_WIDTH)
        for c in range(gate_ref.shape[1] // PROJ_COLS):
            gate_ref[natural, c * PROJ_COLS:(c + 1) * PROJ_COLS] = (
                project(u_ref, sub, gate0 + c * PROJ_COLS))

        for s in range(n_slabs):
            uslab_ref[sub, s] = uf[:, s * LANES:(s + 1) * LANES]
        for r in range(4):
            dst = slice(r * run4, (r + 1) * run4)
            for s in range(n_slabs):
                piece = uslab_ref[sub, s, pl.ds(r, run4, stride=4), :]
                u4slab_ref[sub, s, dst, :] = piece
                u4_ref[sub, dst, s * LANES:(s + 1) * LANES] = piece.astype(BF16)
            for t in range(3):
                tab4_ref[sub, t, dst, :] = rope_ref[t, pl.ds(row0 + r, run4, stride=4), :]
        for run in range(16):
            src0 = (run // 4) * run4 + run % 4
            dst = slice(run * run16, (run + 1) * run16)
            for s in range(n_slabs):
                piece = u4slab_ref[sub, s, pl.ds(src0, run16, stride=4), :]
                u16_ref[sub, dst, s * LANES:(s + 1) * LANES] = piece.astype(BF16)
            for t in range(3):
                tab16_ref[sub, t, dst, :] = tab4_ref[sub, t, pl.ds(src0, run16, stride=4), :]

    cur = slice(POOL_HALO, POOL_HALO + rows)
    sums = []
    prev = z0_ref[cur, :]
    for lvl, ref in enumerate(levels):
        shift = 1 << lvl
        lo = lvl * POOL_GROUP_WIDTH
        shifted = ref[POOL_HALO - shift:POOL_HALO - shift + rows, lo:]
        prev = prev[:, (POOL_GROUP_WIDTH if lvl else 0):] + shifted
        sums.append(prev[:, 0:POOL_GROUP_WIDTH])
        if lvl + 1 < len(levels):
            levels[lvl + 1][cur, lo:] = prev
    pos = seq_tile * rows + lax.broadcasted_iota(jnp.int32, (rows, 1), 0)
    pooled = []
    for g, w in enumerate(POOL_WINDOWS):
        cols = slice(g * POOL_GROUP_WIDTH, (g + 1) * POOL_GROUP_WIDTH)
        count = jnp.minimum(pos + 1, w).astype(F32)
        pooled.append((sums[g] / count - z0_ref[cur, cols]).astype(BF16))
    for lvl, ref in enumerate(levels):
        lo = lvl * POOL_GROUP_WIDTH
        ref[0:POOL_HALO, lo:] = ref[rows:rows + POOL_HALO, lo:]

    def rope(xs, table):
        up = pltpu.roll(xs, LANES - ROT_HALF, axis=1)
        down = pltpu.roll(xs, ROT_HALF, axis=1)
        return xs * table(0) + up * table(1) + down * table(2)

    def store_natural(sub, col, part):
        qkv0_ref[sub * sub_rows:(sub + 1) * sub_rows, col:col + LANES] = part

    def store_by4(sub, col, part):
        for r in range(4):
            qkv1_ref[r, sub * run4:(sub + 1) * run4, col:col + LANES] = (
                part[r * run4:(r + 1) * run4])

    def store_by16(sub, col, part):
        for run in range(16):
            stream = 4 * (run % 4) + run // 4
            qkv2_ref[stream, sub * run16:(sub + 1) * run16, col:col + LANES] = (
                part[run * run16:(run + 1) * run16])

    for sub in range(n_sub):
        natural = slice(sub * sub_rows, (sub + 1) * sub_rows)
        variants = (
            (u_ref, lambda t: rope_ref[t, natural, :], store_natural),
            (u4_ref, lambda t: tab4_ref[sub, t], store_by4),
            (u16_ref, lambda t: tab16_ref[sub, t], store_by16),
        )
        for group, (lhs_ref, table, store) in enumerate(variants):
            for which in range(3):
                acc = project(lhs_ref, sub, which * ATT_WIDTH + group * ATT_OUT_WIDTH)
                for j in range(PROJ_COLS // LANES):
                    part = acc[:, j * LANES:(j + 1) * LANES]
                    if which == 0:
                        part = rope(part, table) * Q_SCALE
                    elif which == 1:
                        part = rope(part, table)
                    store(sub, which * ATT_OUT_WIDTH + j * LANES, part.astype(BF16))

    for g in range(len(POOL_WINDOWS)):
        cols = slice(g * POOL_GROUP_WIDTH, (g + 1) * POOL_GROUP_WIDTH)
        mapped = jnp.dot(pooled[g], wpg_ref[g], preferred_element_type=F32)
        pool_ref[:, cols] = (mapped * pscale_ref[:, cols]).astype(BF16)

    for src_ref, dst_ref in zip(later_f32, later_bf16):
        dst_ref[...] = src_ref[...].astype(BF16)


def _chunk_rows(total_rows, steps):
    chunk = BF16_SUBLANES * pl.cdiv(pl.cdiv(total_rows, steps), BF16_SUBLANES)
    while total_rows % chunk:
        chunk += BF16_SUBLANES
    return chunk


def _in_proj(x2, gain, w_in, w_pool_group, pool_scale, rope_tab, later_weights, batch, seq_len):
    tokens, d_model = x2.shape
    in_width = w_in.shape[1]
    gate_width = in_width - QKV_WIDTH - POOL_WIDTH
    rows = PROJ_ROWS
    n_sub = rows // PROJ_SUB_ROWS
    n_steps = tokens // rows
    tiles_per_seq = seq_len // rows
    const = dict(pipeline_mode=pl.Buffered(1))

    def stream_spec(dilation):
        return pl.BlockSpec((None, dilation, rows // dilation, GROUP_QKV_WIDTH),
                            lambda i: (i // tiles_per_seq, 0, i % tiles_per_seq, 0))

    def chunk_spec(weight):
        chunk = _chunk_rows(weight.shape[0], n_steps)
        last = weight.shape[0] // chunk - 1
        return pl.BlockSpec((chunk, weight.shape[1]), lambda i: (jnp.minimum(i, last), 0))

    later_specs = [chunk_spec(w) for w in later_weights]
    return pl.pallas_call(
        functools.partial(_in_proj_kernel, tiles_per_seq=tiles_per_seq,
                          n_later=len(later_weights)),
        name="in_proj",
        grid=(n_steps,),
        in_specs=[
            pl.BlockSpec((rows, d_model), lambda i: (i, 0)),
            pl.BlockSpec((1, d_model), lambda i: (0, 0), **const),
            pl.BlockSpec((d_model, in_width), lambda i: (0, 0), **const),
            pl.BlockSpec(w_pool_group.shape, lambda i: (0, 0, 0), **const),
            pl.BlockSpec((1, POOL_WIDTH), lambda i: (0, 0), **const),
            pl.BlockSpec((3, rows, LANES), lambda i: (0, i % tiles_per_seq, 0)),
        ] + later_specs,
        out_specs=[
            pl.BlockSpec((rows, GROUP_QKV_WIDTH), lambda i: (i, 0)),
            stream_spec(4),
            stream_spec(16),
            pl.BlockSpec((rows, POOL_WIDTH), lambda i: (i, 0)),
            pl.BlockSpec((rows, gate_width), lambda i: (i, 0)),
        ] + later_specs,
        out_shape=[
            jax.ShapeDtypeStruct((tokens, GROUP_QKV_WIDTH), BF16),
            jax.ShapeDtypeStruct((batch, 4, seq_len // 4, GROUP_QKV_WIDTH), BF16),
            jax.ShapeDtypeStruct((batch, 16, seq_len // 16, GROUP_QKV_WIDTH), BF16),
            jax.ShapeDtypeStruct((tokens, POOL_WIDTH), BF16),
            jax.ShapeDtypeStruct((tokens, gate_width), F32),
        ] + [jax.ShapeDtypeStruct(w.shape, BF16) for w in later_weights],
        scratch_shapes=[pltpu.VMEM((n_sub, PROJ_SUB_ROWS, d_model), BF16)] * 3
        + [pltpu.VMEM((n_sub, d_model // LANES, PROJ_SUB_ROWS, LANES), F32)] * 2
        + [pltpu.VMEM((n_sub, 3, PROJ_SUB_ROWS, LANES), F32)] * 2
        + [pltpu.VMEM((POOL_HALO + rows, POOL_WIDTH), F32) for _ in POOL_WINDOWS],
        compiler_params=pltpu.CompilerParams(
            dimension_semantics=("arbitrary",), vmem_limit_bytes=V7X_VMEM_LIMIT_BYTES),
    )(x2, gain, w_in, w_pool_group, pool_scale, rope_tab, *later_weights)


def _stat_lane(head):
    return head // 2 + (HEAD_DIM if head % 2 == 0 else 0)


def _store_planes(dilation):
    return max(1, dilation // MAX_STORE_STRIDE)


def _stat_tiles(dilation):
    return 2 if dilation == 1 else 1


def _sum_lane_offset(n_stat_tiles):
    return SUM_LANE_OFFSET if n_stat_tiles == 1 else 0


def _multi_plane_groups():
    return [g for g, (_, dilation) in enumerate(ATT_GROUPS) if _store_planes(dilation) > 1]


def _piece_shift(group, piece):
    return HEAD_PAIRS * (SPLIT_PIECES * group + piece)


def _attention_kernel(qkv_ref, bias_ref, eye_ref, *rest, single_step, n_stat_tiles):
    q_ref, k_ref, v_ref = (
        qkv_ref.at[:, :, part * ATT_OUT_WIDTH:(part + 1) * ATT_OUT_WIDTH] for part in range(3))
    stat_refs = rest[:n_stat_tiles]
    o_ref, kcarry_ref, vaug_ref = rest[n_stat_tiles:]
    dilation, rows, _ = q_ref.shape
    first = pl.program_id(1) == 0

    lane = lax.broadcasted_iota(jnp.int32, (Q_BLOCK, LANES), 1)
    low_half = lane < HEAD_DIM
    head_mask = (jnp.where(low_half, 1.0, 0.0).astype(BF16),
                 jnp.where(low_half, 0.0, 1.0).astype(BF16))
    low_rows = lax.broadcasted_iota(jnp.int32, (rows, LANES), 1) < HEAD_DIM
    one = jnp.ones((rows, LANES), BF16)
    nt_dims = (((1,), (1,)), ((), ()))
    zero_block = jnp.zeros((Q_BLOCK, ATT_OUT_WIDTH), BF16)

    def clear_carry():
        for st in range(dilation):
            kcarry_ref[st, 0:Q_BLOCK, :] = zero_block
            vaug_ref[0, st, 0:Q_BLOCK, :] = zero_block
            vaug_ref[1, st, 0:Q_BLOCK, :] = zero_block

    if single_step:
        clear_carry()
    else:
        pl.when(first)(clear_carry)

    for st in range(dilation):
        for pair in range(HEAD_PAIRS):
            cols = slice(pair * LANES, (pair + 1) * LANES)
            v_pair = v_ref[st, :, cols]
            vaug_ref[0, st, Q_BLOCK:Q_BLOCK + rows, cols] = jnp.where(low_rows, v_pair, one)
            vaug_ref[1, st, Q_BLOCK:Q_BLOCK + rows, cols] = jnp.where(low_rows, one, v_pair)

    def scores(st, sub, pair):
        cols = slice(pair * LANES, (pair + 1) * LANES)
        q_pair = q_ref[st, sub * Q_BLOCK:(sub + 1) * Q_BLOCK, cols]
        if sub == 0:
            k_pair = jnp.concatenate(
                [kcarry_ref[st, :, cols], k_ref[st, 0:Q_BLOCK, cols]], axis=0)
            bias_idx = 1 if single_step else jnp.where(first, 1, 0)
        else:
            k_pair = k_ref[st, (sub - 1) * Q_BLOCK:(sub + 1) * Q_BLOCK, cols]
            bias_idx = 0
        q_both = jnp.concatenate([q_pair * head_mask[0], q_pair * head_mask[1]], axis=0)
        q_both = jnp.concatenate([q_both, eye_ref[...]], axis=1)
        k_aug = jnp.concatenate([k_pair, bias_ref[bias_idx]], axis=1)
        return lax.dot_general(q_both, k_aug, nt_dims, preferred_element_type=F32)

    def finish(st, sub, pair, s_both, stats):
        cols = slice(pair * LANES, (pair + 1) * LANES)
        band = slice(sub * Q_BLOCK, (sub + 2) * Q_BLOCK)
        planes = _store_planes(dilation)
        stride = dilation // planes
        start = sub * Q_BLOCK * stride + st // planes
        tokens = pl.ds(start, Q_BLOCK, stride=stride) if stride > 1 else pl.ds(start, Q_BLOCK)
        plane = st % planes
        pvs = []
        for half in range(2):
            s = s_both[half * Q_BLOCK:(half + 1) * Q_BLOCK]
            m = jnp.max(s, axis=-1, keepdims=True)
            p = jnp.exp2(s - m).astype(BF16)
            pv = jnp.dot(p, vaug_ref[half, st, band, cols], preferred_element_type=F32)
            pvs.append(pv)
            stat_lane = _stat_lane(2 * pair + half)
            stats[0] = jnp.where(lane == stat_lane, m, stats[0])
            stats[-1] = jnp.where(lane == stat_lane + _sum_lane_offset(len(stats)), pv, stats[-1])
        o_ref[pair, plane, tokens, :] = jnp.where(low_half, pvs[0], pvs[1])
        if pair == HEAD_PAIRS - 1:
            for stat_ref, tile in zip(stat_refs, stats):
                stat_ref[plane, tokens, :] = tile
        return stats

    units = [(st, sub, pair) for st in range(dilation) for sub in range(rows // Q_BLOCK)
             for pair in range(HEAD_PAIRS)]
    ahead = [scores(*unit) for unit in units[:SCORE_LOOKAHEAD]]
    stats = None
    for index, unit in enumerate(units):
        s_both = ahead.pop(0)
        if index + SCORE_LOOKAHEAD < len(units):
            ahead.append(scores(*units[index + SCORE_LOOKAHEAD]))
        if unit[2] == 0:
            stats = [jnp.ones((Q_BLOCK, LANES), F32) for _ in stat_refs]
        stats = finish(*unit, s_both, stats)

    for st in range(dilation):
        kcarry_ref[st, 0:Q_BLOCK, :] = k_ref[st, rows - Q_BLOCK:rows, :]
        for half in range(2):
            vaug_ref[half, st, 0:Q_BLOCK, :] = vaug_ref[half, st, rows:rows + Q_BLOCK, :]


def _attention_group(qkv, bias, eye, batch, seq_len, group):
    _, dilation = ATT_GROUPS[group]
    stream_len = seq_len // dilation
    rows = max(Q_BLOCK, ATT_ROWS // dilation)
    planes = _store_planes(dilation)
    plane_rows = rows * dilation // planes
    single_step = rows == stream_len

    qkv_spec = pl.BlockSpec((None, dilation, rows, GROUP_QKV_WIDTH), lambda b, m: (b, 0, m, 0))

    stat_spec = pl.BlockSpec((None, planes, plane_rows, LANES), lambda b, m: (b, 0, m, 0))
    stat_shape = jax.ShapeDtypeStruct((batch, planes, seq_len // planes, LANES), F32)
    n_stat_tiles = _stat_tiles(dilation)
    return pl.pallas_call(
        functools.partial(_attention_kernel, single_step=single_step,
                          n_stat_tiles=n_stat_tiles),
        name=f"attention_g{group}",
        grid=(batch, stream_len // rows),
        in_specs=[
            qkv_spec,
            pl.BlockSpec(bias.shape, lambda b, m: (0, 0, 0), pipeline_mode=pl.Buffered(1)),
            pl.BlockSpec(eye.shape, lambda b, m: (0, 0), pipeline_mode=pl.Buffered(1)),
        ],
        out_specs=[stat_spec] * n_stat_tiles + [
            pl.BlockSpec((None, HEAD_PAIRS, planes, plane_rows, LANES),
                         lambda b, m: (b, 0, 0, m, 0)),
        ],
        out_shape=[stat_shape] * n_stat_tiles + [
            jax.ShapeDtypeStruct((batch, HEAD_PAIRS, planes, seq_len // planes, LANES), F32),
        ],
        scratch_shapes=[
            pltpu.VMEM((dilation, Q_BLOCK, ATT_OUT_WIDTH), BF16),
            pltpu.VMEM((2, dilation, Q_BLOCK + rows, ATT_OUT_WIDTH), BF16),
        ],
        compiler_params=pltpu.CompilerParams(
            dimension_semantics=("arbitrary", "arbitrary"),
            vmem_limit_bytes=V7X_VMEM_LIMIT_BYTES),
    )(qkv, bias, eye)


def _merge_ffn_kernel(x_ref, o0_ref, o1_ref, o2_ref, *rest):
    stat_refs = []
    for _, dilation in ATT_GROUPS:
        stat_refs.append(rest[:_stat_tiles(dilation)])
        rest = rest[_stat_tiles(dilation):]
    (pool_ref, gate_ref, spread_ref,
     wab_ref, wpb_ref, wout_ref, gffn_ref, wg_ref, wu_ref, wd_ref, gfin_ref,
     out_ref, att_ref, onat_ref, snat_ref) = rest
    d_model = x_ref.shape[1]
    o_refs = (o0_ref, o1_ref, o2_ref)
    lane = lax.broadcasted_iota(jnp.int32, (FFN_SUB_ROWS, LANES), 1)
    stat_lanes = (lane % HEAD_DIM) < HEAD_PAIRS

    slots = {g: slot for slot, g in enumerate(_multi_plane_groups())}

    def o_dst(g, pair):
        return onat_ref.at[slots[g], pair] if g in slots else None

    def stat_dst(g):
        return snat_ref.at[slots[g]] if g in slots else None

    def chain(sub):
        rows = slice(sub * FFN_SUB_ROWS, (sub + 1) * FFN_SUB_ROWS)

        def in_token_order(src_ref, dst_ref):
            planes = src_ref.shape[0]
            if planes == 1:
                return src_ref[0, rows, :]
            per_plane = FFN_SUB_ROWS // planes
            for plane in range(planes):
                dst_ref[pl.ds(sub * FFN_SUB_ROWS + plane, per_plane, stride=planes), :] = (
                    src_ref[plane, sub * per_plane:(sub + 1) * per_plane, :])
            return dst_ref[rows, :]

        ms, ls = [], []
        for g, tiles in enumerate(stat_refs):
            ms.append(in_token_order(tiles[0], stat_dst(g)))
            offset = _sum_lane_offset(len(tiles))
            if offset:
                ls.append(pltpu.roll(ms[-1], LANES - offset, axis=1))
            else:
                ls.append(in_token_order(tiles[-1], stat_dst(g)))
        top = jnp.maximum(jnp.maximum(ms[0], ms[1]), ms[2])
        es = [jnp.exp2(m - top) for m in ms]
        den = es[0] * ls[0] + es[1] * ls[1] + es[2] * ls[2]

        packed = jnp.zeros((FFN_SUB_ROWS, LANES), F32)
        for g in range(N_ATT_GROUPS):
            rest = jnp.where(stat_lanes, es[g] / den, 0.0)
            for piece in range(SPLIT_PIECES):
                part = rest.astype(BF16).astype(F32)
                rest = rest - part
                shift = _piece_shift(g, piece)
                packed = packed + (pltpu.roll(part, shift, axis=1) if shift else part)
        spread = jnp.dot(packed.astype(BF16), spread_ref[...], preferred_element_type=F32)
        yield

        for pair in range(HEAD_PAIRS):
            acc = jnp.zeros((FFN_SUB_ROWS, LANES), F32)
            for g in range(N_ATT_GROUPS):
                col = g * ATT_OUT_WIDTH + pair * LANES
                o_pair = in_token_order(o_refs[g].at[pair], o_dst(g, pair))
                acc = acc + spread[:, col:col + LANES] * o_pair
            att_ref[rows, pair * LANES:(pair + 1) * LANES] = acc.astype(BF16)
        yield

        y_att = jnp.dot(att_ref[rows, :], wab_ref[...], preferred_element_type=F32)
        y_pool = jnp.dot(pool_ref[rows, :], wpb_ref[...], preferred_element_type=F32)
        yield
        merged = (jax.nn.sigmoid(gate_ref[rows, 0:d_model]) * y_att
                  + jax.nn.sigmoid(gate_ref[rows, d_model:2 * d_model]) * y_pool)
        yield
        h = x_ref[rows, :] + jnp.dot(merged.astype(BF16), wout_ref[...],
                                     preferred_element_type=F32)
        yield
        f = _rms_norm(h, gffn_ref[...]).astype(BF16)
        yield
        ffn = jnp.zeros((FFN_SUB_ROWS, d_model), F32)
        for lo, hi in FFN_CHUNKS:
            gate = jnp.dot(f, wg_ref[:, lo:hi], preferred_element_type=F32)
            up = jnp.dot(f, wu_ref[:, lo:hi], preferred_element_type=F32)
            yield
            hidden = (jax.nn.silu(gate) * up).astype(BF16)
            yield
            ffn = ffn + jnp.dot(hidden, wd_ref[lo:hi, :], preferred_element_type=F32)
            yield
        out_ref[rows, :] = _rms_norm(h + ffn, gfin_ref[...])

    _run_staggered([chain(sub) for sub in range(x_ref.shape[0] // FFN_SUB_ROWS)])


def _merge_ffn(x2, outs, stats, pool_feat, gates, w_att_branch, w_pool_branch, w_out, norm_ffn,
               w_gate, w_up, w_down, norm_final, seq_len):
    tokens, d_model = x2.shape
    rows = FFN_ROWS
    tiles_per_seq = seq_len // rows

    def tile(width):
        return pl.BlockSpec((rows, width), lambda i: (i, 0))

    def whole(arr):
        return pl.BlockSpec(arr.shape, lambda i: (0, 0), pipeline_mode=pl.Buffered(1))

    def o_spec(planes):
        return pl.BlockSpec((None, HEAD_PAIRS, planes, rows // planes, LANES),
                            lambda i: (i // tiles_per_seq, 0, 0, i % tiles_per_seq, 0))

    def stat_spec(planes):
        return pl.BlockSpec((None, planes, rows // planes, LANES),
                            lambda i: (i // tiles_per_seq, 0, i % tiles_per_seq, 0))

    group_planes = [_store_planes(dilation) for _, dilation in ATT_GROUPS]
    n_slots = max(1, len(_multi_plane_groups()))
    weights = (_spread_matrix(), w_att_branch, w_pool_branch, w_out, norm_ffn, w_gate, w_up,
               w_down, norm_final)
    return pl.pallas_call(
        _merge_ffn_kernel,
        name="merge_ffn",
        grid=(tokens // rows,),
        in_specs=[tile(d_model)]
        + [o_spec(planes) for planes in group_planes]
        + [stat_spec(planes) for planes, tiles in zip(group_planes, stats) for _ in tiles]
        + [tile(POOL_WIDTH), tile(gates.shape[1])]
        + [whole(w) for w in weights],
        out_specs=tile(d_model),
        out_shape=jax.ShapeDtypeStruct((tokens, d_model), F32),
        scratch_shapes=[
            pltpu.VMEM((rows, ATT_OUT_WIDTH), BF16),
            pltpu.VMEM((n_slots, HEAD_PAIRS, rows, LANES), F32),
            pltpu.VMEM((n_slots, rows, LANES), F32),
        ],
        compiler_params=pltpu.CompilerParams(
            dimension_semantics=("arbitrary",), vmem_limit_bytes=V7X_VMEM_LIMIT_BYTES),
    )(x2, *outs, *[tile for tiles in stats for tile in tiles], pool_feat, gates, *weights)


def _rope_tables(seq_len):
    inv_freq = ROPE_THETA ** (-np.arange(0, ROT_DIM, 2, dtype=np.float64) / ROT_DIM)
    ang = np.arange(seq_len, dtype=np.float64)[:, None] * inv_freq[None, :]
    cos, sin = np.cos(ang), np.sin(ang)
    zeros = np.zeros((seq_len, HEAD_DIM - ROT_DIM))
    half0 = np.zeros((seq_len, ROT_HALF))
    c_head = np.concatenate([cos, cos, zeros + 1.0], axis=1)
    lo_head = np.concatenate([-sin, half0, zeros], axis=1)
    hi_head = np.concatenate([half0, sin, zeros], axis=1)
    tab = np.stack([c_head, lo_head, hi_head])
    return jnp.asarray(np.concatenate([tab] * (LANES // HEAD_DIM), axis=2), dtype=F32)


def _spread_matrix():
    row = jnp.arange(LANES)[:, None]
    col = jnp.arange(N_ATT_GROUPS * ATT_OUT_WIDTH)[None, :]
    group = col // ATT_OUT_WIDTH
    head = (col % ATT_OUT_WIDTH) // HEAD_DIM
    stat = head // 2 + jnp.where(head % 2 == 0, HEAD_DIM, 0)
    hit = jnp.zeros((LANES, N_ATT_GROUPS * ATT_OUT_WIDTH), jnp.bool_)
    for piece in range(SPLIT_PIECES):
        hit = hit | (row == stat + HEAD_PAIRS * (SPLIT_PIECES * group + piece))
    return hit.astype(BF16)


def _band_bias():
    kb = jnp.arange(2 * Q_BLOCK)[:, None]
    qi = jnp.arange(Q_BLOCK)[None, :]
    dist = qi + Q_BLOCK - kb
    valid = (dist >= 0) & (dist <= WINDOW_KEYS)
    both = jnp.stack([valid, valid & (kb >= Q_BLOCK)])
    return jnp.where(both, 0.0, MASKED).astype(BF16)


def _stacked_identity():
    row = jnp.arange(2 * Q_BLOCK)[:, None]
    col = jnp.arange(LANES)[None, :]
    return (row % Q_BLOCK == col).astype(BF16)


def kernel(x, norm_mix, w_in, w_pool_group, pool_scale, w_att_branch, w_pool_branch, w_out,
           norm_ffn, w_ffn_gate, w_ffn_up, w_ffn_down, norm_final):
    batch, seq_len, d_model = x.shape
    depth = w_in.shape[0]
    tokens = batch * seq_len
    assert depth == 1, "stacked layers need the un-normalised residual between layers"
    bias = _band_bias()

    h = x.reshape(tokens, d_model)
    later_weights = (w_att_branch[0], w_pool_branch[0], w_out[0], w_ffn_gate[0], w_ffn_up[0],
                     w_ffn_down[0])
    qkv0, qkv1, qkv2, pool_feat, gates, *later_bf16 = _in_proj(
        h, norm_mix[0][None, :], w_in[0].astype(BF16), w_pool_group[0].astype(BF16),
        pool_scale[0][None, :], _rope_tables(seq_len), later_weights, batch, seq_len)
    wab, wpb, wout, wgate, wup, wdown = later_bf16
    qkv_groups = (qkv0.reshape(batch, 1, seq_len, GROUP_QKV_WIDTH), qkv1, qkv2)
    eye = _stacked_identity()
    att = [_attention_group(qkv_groups[g], bias, eye, batch, seq_len, g)
           for g in range(N_ATT_GROUPS)]
    out = _merge_ffn(
        h, [a[-1] for a in att], [a[:-1] for a in att], pool_feat, gates,
        wab, wpb, wout, norm_ffn[0][None, :], wgate, wup, wdown, norm_final[None, :], seq_len)
    return out.reshape(batch, seq_len, d_model)
```
